```python
import jax
import jax.numpy as jnp
from jax import lax
import numpy as np

D_MODEL = 1024
BATCH = 2
SEQ = 8192
DEPTH = 4

GRID_W = 64
CTX_LEN = 256
N_MIXERS = 3
N_CONV_LAYERS = (DEPTH + N_MIXERS - 1) // N_MIXERS
N_ATTN_LAYERS = (DEPTH + N_MIXERS - 2) // N_MIXERS
N_LRU_LAYERS = DEPTH // N_MIXERS
N_MOD = 6
EPS = 1e-6

CONV_WIDTH = 31

HEAD_DIM = 64
N_HEADS = D_MODEL // HEAD_DIM
N_KV_HEADS = 4
GQA_GROUP = N_HEADS // N_KV_HEADS
Q_DIM = N_HEADS * HEAD_DIM
KV_DIM = N_KV_HEADS * HEAD_DIM
WINDOW = 128
ATTN_BLOCK = 128
ROPE_BASE = 10000.0
ROPE_PAIRS = HEAD_DIM // 4

D_RNN = D_MODEL
LRU_BLOCKS = 8
LRU_BLOCK_DIM = D_RNN // LRU_BLOCKS
LRU_CONV_W = 4
LRU_C = 8.0

N_EXPERTS = 16
N_GROUPS = 4
EXPERTS_PER_GROUP = N_EXPERTS // N_GROUPS
TOP_K = 2
D_EXPERT = 512
MOE_BLOCK = 128

kernel_name = 'hybrid_conv_swa_rglru_moe_dit'


def rmsnorm(x, g):
    x32 = x.astype(jnp.float32)
    y = x32 * lax.rsqrt(jnp.mean(x32 * x32, axis=-1, keepdims=True) + EPS)
    return (y * g.astype(jnp.float32)).astype(x.dtype)


def layernorm(x, g, b):
    x32 = x.astype(jnp.float32)
    mu = jnp.mean(x32, axis=-1, keepdims=True)
    xc = x32 - mu
    var = jnp.mean(xc * xc, axis=-1, keepdims=True)
    return (xc * lax.rsqrt(var + EPS) * g.astype(jnp.float32) + b.astype(jnp.float32)).astype(x.dtype)


def modulate(h, shift, scale):
    return h * (1.0 + scale) + shift


def depthwise_conv(u, w, b, pad):
    y = lax.conv_general_dilated(u, w[:, None, :].astype(u.dtype), window_strides=(1,), padding=[pad],
                                 dimension_numbers=('NWC', 'WIO', 'NWC'), feature_group_count=u.shape[-1])
    return y + b.astype(u.dtype)


def conformer_conv(h, pw1, pw1_b, dw, dw_b, ln_g, ln_b, pw2, pw2_b):
    u = h @ pw1 + pw1_b
    u = u[..., :D_MODEL] * jax.nn.sigmoid(u[..., D_MODEL:])
    half = (CONV_WIDTH - 1) // 2
    u = depthwise_conv(u, dw, dw_b, (half, half))
    u = jax.nn.silu(layernorm(u, ln_g, ln_b))
    return u @ pw2 + pw2_b


def axial_rope_tables(rows, cols):
    inv = ROPE_BASE ** (-jnp.arange(ROPE_PAIRS, dtype=jnp.float32) / ROPE_PAIRS)
    ar = rows.astype(jnp.float32)[:, None] * inv
    ac = cols.astype(jnp.float32)[:, None] * inv
    ang = jnp.concatenate([ar, ar, ac, ac], axis=-1)
    return jnp.cos(ang), jnp.sin(ang)


def apply_rope(x, cos, sin):
    xr = x.reshape(x.shape[:-1] + (2, 2, ROPE_PAIRS))
    rot = jnp.stack([-xr[..., 1, :], xr[..., 0, :]], axis=-2).reshape(x.shape)
    return x * cos.astype(x.dtype) + rot * sin.astype(x.dtype)


def sink_softmax_av(parts, sink):
    sink_b = sink.astype(jnp.float32).reshape(1, N_KV_HEADS, GQA_GROUP, 1)
    m = sink_b
    for s, _ in parts:
        m = jnp.maximum(m, jnp.max(s, axis=-1))
    den = jnp.exp(sink_b - m)
    out = 0.0
    for s, v in parts:
        p = jnp.exp(s - m[..., None])
        den = den + jnp.sum(p, axis=-1)
        out = out + jnp.einsum('bkgqs,bskd->bqkgd', p, v.astype(jnp.float32))
    return out / jnp.transpose(den, (0, 3, 1, 2))[..., None]


def windowed_sink_gqa(hc, hl, w_qkv, w_o, sink, cos, sin):
    B, S, _ = hl.shape
    L = hc.shape[1]
    nb = S // ATTN_BLOCK
    scale = HEAD_DIM ** -0.5

    def qkv(h):
        t = h @ w_qkv
        n = h.shape[1]
        q = t[..., :Q_DIM].reshape(B, n, N_KV_HEADS, GQA_GROUP, HEAD_DIM) * scale
        k = t[..., Q_DIM:Q_DIM + KV_DIM].reshape(B, n, N_KV_HEADS, HEAD_DIM)
        v = t[..., Q_DIM + KV_DIM:].reshape(B, n, N_KV_HEADS, HEAD_DIM)
        return q, k, v

    qc, kc, vc = qkv(hc)
    ql, kl, vl = qkv(hl)
    ql = apply_rope(ql, cos[None, :, None, None, :], sin[None, :, None, None, :])
    kl = apply_rope(kl, cos[None, :, None, :], sin[None, :, None, :])

    s_cc = jnp.einsum('bqkgd,bckd->bkgqc', qc, kc, preferred_element_type=jnp.float32)
    oc = sink_softmax_av([(s_cc, vc)], sink).reshape(B, L, Q_DIM)

    span = ATTN_BLOCK + 2 * WINDOW
    kp = jnp.pad(kl, ((0, 0), (WINDOW, WINDOW), (0, 0), (0, 0)))
    vp = jnp.pad(vl, ((0, 0), (WINDOW, WINDOW), (0, 0), (0, 0)))
    qb = jnp.swapaxes(ql.reshape(B, nb, ATTN_BLOCK, N_KV_HEADS, GQA_GROUP, HEAD_DIM), 0, 1)
    rel = jnp.arange(ATTN_BLOCK)[:, None] - (jnp.arange(span)[None, :] - WINDOW)
    band = jnp.abs(rel) <= WINDOW

    def block(args):
        j, q = args
        start = j * ATTN_BLOCK
        k = lax.dynamic_slice_in_dim(kp, start, span, axis=1)
        v = lax.dynamic_slice_in_dim(vp, start, span, axis=1)
        kpos = start - WINDOW + jnp.arange(span)
        valid = band & ((kpos >= 0) & (kpos < S))[None, :]
        s_ll = jnp.einsum('bqkgd,bskd->bkgqs', q, k, preferred_element_type=jnp.float32)
        s_ll = jnp.where(valid, s_ll, -jnp.inf)
        s_lc = jnp.einsum('bqkgd,bckd->bkgqc', q, kc, preferred_element_type=jnp.float32)
        return sink_softmax_av([(s_ll, v), (s_lc, vc)], sink)

    ol = lax.map(block, (jnp.arange(nb), qb))
    ol = jnp.swapaxes(ol, 0, 1).reshape(B, S, Q_DIM)
    return oc.astype(hc.dtype) @ w_o, ol.astype(hl.dtype) @ w_o


def _linear_combine(e1, e2):
    a1, b1 = e1
    a2, b2 = e2
    return a1 * a2, a2 * b1 + b2


def rglru_scan(u, wa, ba, wx, bx, lam, h0):
    B, N, _ = u.shape
    ub = u.reshape(B, N, LRU_BLOCKS, LRU_BLOCK_DIM)
    r = jax.nn.sigmoid(jnp.einsum('bnhi,hij->bnhj', ub, wa, preferred_element_type=jnp.float32).reshape(B, N, D_RNN) + ba)
    gi = jax.nn.sigmoid(jnp.einsum('bnhi,hij->bnhj', ub, wx, preferred_element_type=jnp.float32).reshape(B, N, D_RNN) + bx)
    log_a = -LRU_C * jax.nn.softplus(-lam.astype(jnp.float32)) * r
    a = jnp.exp(log_a)
    b = jnp.sqrt(-jnp.expm1(2.0 * log_a)) * (gi * u.astype(jnp.float32))
    b = b.at[:, 0].add(a[:, 0] * h0)
    _, h = lax.associative_scan(_linear_combine, (a, b), axis=1)
    return h, h[:, -1]


def _orient(t, backward):
    return t[:, ::-1] if backward else t


def bidir_rglru(hc, hl, w_in, conv_w, conv_b, wa, ba, wx, bx, lam, w_out):
    def branches(h):
        t = h @ w_in
        return t[..., :D_RNN], jax.nn.gelu(t[..., D_RNN:])

    uc, gc = branches(hc)
    ul, gl = branches(hl)
    B = hl.shape[0]
    yc = jnp.zeros(uc.shape, jnp.float32)
    yl = jnp.zeros(ul.shape, jnp.float32)
    for d in range(2):
        back = d == 1
        pad = (LRU_CONV_W - 1, 0)
        cc = depthwise_conv(_orient(uc, back), conv_w[d], conv_b[d], pad)
        cl = depthwise_conv(_orient(ul, back), conv_w[d], conv_b[d], pad)
        hc_seq, h_ctx = rglru_scan(cc, wa[d], ba[d], wx[d], bx[d], lam[d], jnp.zeros((B, D_RNN), jnp.float32))
        hl_seq, _ = rglru_scan(cl, wa[d], ba[d], wx[d], bx[d], lam[d], h_ctx)
        yc = yc + _orient(hc_seq, back)
        yl = yl + _orient(hl_seq, back)
    oc = (yc * gc.astype(jnp.float32)).astype(hc.dtype) @ w_out
    ol = (yl * gl.astype(jnp.float32)).astype(hl.dtype) @ w_out
    return oc, ol


def route(ht, w_router, router_bias):
    T = ht.shape[0]
    s = jax.nn.sigmoid(jnp.matmul(ht, w_router, preferred_element_type=jnp.float32))
    sel = (s + router_bias.astype(jnp.float32)).reshape(T, N_GROUPS, EXPERTS_PER_GROUP)
    group_score = jnp.sum(lax.top_k(sel, 2)[0], axis=-1)
    grp = jnp.argmax(group_score, axis=-1)
    _, loc = lax.top_k(sel[jnp.arange(T), grp], TOP_K)
    idx = grp[:, None] * EXPERTS_PER_GROUP + loc
    w = jnp.take_along_axis(s, idx, axis=1)
    return idx.astype(jnp.int32), w / jnp.sum(w, axis=-1, keepdims=True)


def moe(h, w_router, router_bias, w1, w3, w2):
    shp = h.shape
    ht = h.reshape(-1, shp[-1])
    T = ht.shape[0]
    idx, gate = route(ht, w_router, router_bias)
    A = T * TOP_K
    flat_e = idx.reshape(-1)
    flat_t = jnp.repeat(jnp.arange(T, dtype=jnp.int32), TOP_K)
    flat_g = gate.reshape(-1)
    order = jnp.argsort(flat_e)
    se = flat_e[order]
    counts = jnp.bincount(flat_e, length=N_EXPERTS)
    padded = (counts + MOE_BLOCK - 1) // MOE_BLOCK * MOE_BLOCK
    pend = jnp.cumsum(padded)
    pstart = pend - padded
    cstart = jnp.cumsum(counts) - counts
    dest = pstart[se] + jnp.arange(A, dtype=jnp.int32) - cstart[se]
    n_slots = -(-A // MOE_BLOCK) * MOE_BLOCK + N_EXPERTS * MOE_BLOCK
    n_blocks = n_slots // MOE_BLOCK
    slot_tok = jnp.zeros((n_slots,), jnp.int32).at[dest].set(flat_t[order])
    slot_gate = jnp.zeros((n_slots,), jnp.float32).at[dest].set(flat_g[order])
    blk_e = jnp.minimum(jnp.searchsorted(pend, jnp.arange(n_blocks, dtype=jnp.int32) * MOE_BLOCK, side='right'),
                        N_EXPERTS - 1)

    def expert_block(args):
        e, tok, g = args
        xb = ht[tok]
        hid = jax.nn.silu(xb @ w1[e]) * (xb @ w3[e])
        return (hid @ w2[e]) * g[:, None].astype(ht.dtype)

    yb = lax.map(expert_block, (blk_e, slot_tok.reshape(n_blocks, MOE_BLOCK), slot_gate.reshape(n_blocks, MOE_BLOCK)))
    out = jnp.zeros_like(ht).at[slot_tok].add(yb.reshape(n_slots, -1))
    return out.reshape(shp)


def setup_inputs(seed: int = 0) -> dict:
    key = jax.random.key(seed)
    keys = iter(jax.random.split(key, 48))

    def nrm(shape, scale):
        return jax.random.normal(next(keys), shape, jnp.float32) * scale

    D = D_MODEL
    nC, nA, nL = N_CONV_LAYERS, N_ATTN_LAYERS, N_LRU_LAYERS
    u = jax.random.uniform(next(keys), (nL, 2, D_RNN), jnp.float32, 0.9, 0.999)
    p = u ** (1.0 / LRU_C)
    return {
        'x': nrm((BATCH, SEQ, D), 1.0),
        'c': nrm((BATCH, D), 1.0),
        'ctx': nrm((BATCH, CTX_LEN, D), 1.0),
        'c_ctx': nrm((D,), 1.0),
        'w_mod': nrm((DEPTH, D, N_MOD * D), 0.5 * D ** -0.5),
        'b_mod': nrm((DEPTH, N_MOD * D), 0.02),
        'norm_mix': 1.0 + nrm((DEPTH, D), 0.02),
        'norm_ffn': 1.0 + nrm((DEPTH, D), 0.02),
        'norm_f': 1.0 + nrm((D,), 0.02),
        'conv_pw1': nrm((nC, D, 2 * D), D ** -0.5),
        'conv_pw1_b': nrm((nC, 2 * D), 0.02),
        'conv_dw': nrm((nC, CONV_WIDTH, D), CONV_WIDTH ** -0.5),
        'conv_dw_b': nrm((nC, D), 0.02),
        'conv_ln_g': 1.0 + nrm((nC, D), 0.02),
        'conv_ln_b': nrm((nC, D), 0.02),
        'conv_pw2': nrm((nC, D, D), D ** -0.5),
        'conv_pw2_b': nrm((nC, D), 0.02),
        'attn_w_qkv': nrm((nA, D, Q_DIM + 2 * KV_DIM), D ** -0.5),
        'attn_w_o': nrm((nA, Q_DIM, D), Q_DIM ** -0.5),
        'attn_sink': nrm((nA, N_HEADS), 0.5),
        'lru_w_in': nrm((nL, D, 2 * D_RNN), D ** -0.5),
        'lru_conv_w': nrm((nL, 2, LRU_CONV_W, D_RNN), LRU_CONV_W ** -0.5),
        'lru_conv_b': nrm((nL, 2, D_RNN), 0.02),
        'lru_wa': nrm((nL, 2, LRU_BLOCKS, LRU_BLOCK_DIM, LRU_BLOCK_DIM), LRU_BLOCK_DIM ** -0.5),
        'lru_ba': nrm((nL, 2, D_RNN), 0.02),
        'lru_wx': nrm((nL, 2, LRU_BLOCKS, LRU_BLOCK_DIM, LRU_BLOCK_DIM), LRU_BLOCK_DIM ** -0.5),
        'lru_bx': nrm((nL, 2, D_RNN), 0.02),
        'lru_lam': jnp.log(p) - jnp.log1p(-p),
        'lru_w_out': nrm((nL, D_RNN, D), D_RNN ** -0.5),
        'moe_w_router': nrm((D, N_EXPERTS), D ** -0.5),
        'moe_router_bias': nrm((N_EXPERTS,), 0.01),
        'moe_w1': nrm((DEPTH, N_EXPERTS, D, D_EXPERT), D ** -0.5),
        'moe_w3': nrm((DEPTH, N_EXPERTS, D, D_EXPERT), D ** -0.5),
        'moe_w2': nrm((DEPTH, N_EXPERTS, D_EXPERT, D), D_EXPERT ** -0.5),
    }


def reference(x, c, ctx, c_ctx, w_mod, b_mod, norm_mix, norm_ffn, norm_f,
              conv_pw1, conv_pw1_b, conv_dw, conv_dw_b, conv_ln_g, conv_ln_b, conv_pw2, conv_pw2_b,
              attn_w_qkv, attn_w_o, attn_sink,
              lru_w_in, lru_conv_w, lru_conv_b, lru_wa, lru_ba, lru_wx, lru_bx, lru_lam, lru_w_out,
              moe_w_router, moe_router_bias, moe_w1, moe_w3, moe_w2):
    B, S, _ = x.shape
    L = ctx.shape[1]
    ROWS = S // GRID_W
    rows = jnp.repeat(jnp.arange(ROWS, dtype=jnp.int32), GRID_W)
    cols = jnp.tile(jnp.arange(GRID_W, dtype=jnp.int32), ROWS)
    cos, sin = axial_rope_tables(rows, cols)
    sc = jax.nn.silu(c)
    scc = jax.nn.silu(c_ctx)
    xl, xc = x, ctx
    for i in range(DEPTH):
        last = i == DEPTH - 1
        kind, slot = i % N_MIXERS, i // N_MIXERS
        mod_l = jnp.split((sc @ w_mod[i] + b_mod[i])[:, None, :], N_MOD, axis=-1)
        mod_c = jnp.split(scc @ w_mod[i] + b_mod[i], N_MOD, axis=-1)
        hl = modulate(rmsnorm(xl, norm_mix[i]), mod_l[0], mod_l[1])
        hc = None if (last and kind == 0) else modulate(rmsnorm(xc, norm_mix[i]), mod_c[0], mod_c[1])
        if kind == 0:
            prm = (conv_pw1[slot], conv_pw1_b[slot], conv_dw[slot], conv_dw_b[slot],
                   conv_ln_g[slot], conv_ln_b[slot], conv_pw2[slot], conv_pw2_b[slot])
            yl = conformer_conv(hl, *prm)
            yc = None if last else conformer_conv(hc, *prm)
        elif kind == 1:
            yc, yl = windowed_sink_gqa(hc, hl, attn_w_qkv[slot], attn_w_o[slot], attn_sink[slot], cos, sin)
        else:
            yc, yl = bidir_rglru(hc, hl, lru_w_in[slot], lru_conv_w[slot], lru_conv_b[slot], lru_wa[slot],
                                 lru_ba[slot], lru_wx[slot], lru_bx[slot], lru_lam[slot], lru_w_out[slot])
        xl = xl + mod_l[2] * yl
        moe_prm = (moe_w_router, moe_router_bias, moe_w1[i], moe_w3[i], moe_w2[i])
        hl2 = modulate(rmsnorm(xl, norm_ffn[i]), mod_l[3], mod_l[4])
        if last:
            xl = xl + mod_l[5] * moe(hl2, *moe_prm)
        else:
            xc = xc + mod_c[2] * yc
            hc2 = modulate(rmsnorm(xc, norm_ffn[i]), mod_c[3], mod_c[4])
            y2 = moe(jnp.concatenate([hc2, hl2], axis=1), *moe_prm)
            xc = xc + mod_c[5] * y2[:, :L]
            xl = xl + mod_l[5] * y2[:, L:]
    return rmsnorm(xl, norm_f)
```

```python
import functools

import jax
import jax.numpy as jnp
from jax import lax
from jax.experimental import pallas as pl
from jax.experimental.pallas import tpu as pltpu

F32 = jnp.float32
BF16 = jnp.bfloat16
I32 = jnp.int32

EPS = 1e-6
N_MOD = 6
N_MIXERS = 3
GRID_W = 64
CONV_WIDTH = 31
CONV_HALF = (CONV_WIDTH - 1) // 2
HEAD_DIM = 64
N_KV_HEADS = 4
GQA_GROUP = 4
WINDOW = 128
ROPE_BASE = 10000.0
ROPE_PAIRS = HEAD_DIM // 4
LRU_BLOCKS = 8
LRU_CONV_W = 4
LRU_C = 8.0
N_EXPERTS = 16
N_GROUPS = 4
EXPERTS_PER_GROUP = 4
TOP_K = 2

LANES = 128
SUBLANES = 8
TR = 256
TQ = 128
HALO = 16
MOE_BLK = 256
NEG = -1e30
VMEM_LIMIT = 56 * 1024 * 1024


def _params(sem, vmem=VMEM_LIMIT):
    return pltpu.CompilerParams(dimension_semantics=sem, vmem_limit_bytes=vmem)


def _norm_mod(x, g, shift, scale):
    y = x * lax.rsqrt(jnp.mean(x * x, axis=-1, keepdims=True) + EPS)
    return (y * g) * (1.0 + scale) + shift


def _seg_mod_spec(d):
    return pl.BlockSpec((1, 1, N_MOD, d), lambda b, t: (b, jnp.minimum(t, 1), 0, 0))


def _row_spec(tr, d):
    return pl.BlockSpec((1, tr, d), lambda b, t: (b, t, 0))


def _full_spec(shape):
    nd = len(shape)
    return pl.BlockSpec(shape, lambda b, t: (0,) * nd)


def _mod_kernel(c_ref, w_ref, b_ref, o_ref):
    c = c_ref[...]
    sc = c * jax.nn.sigmoid(c)
    o_ref[0] = jnp.dot(sc.astype(BF16), w_ref[0].astype(BF16), preferred_element_type=F32) + b_ref[0]


def _modulation(c, c_ctx, w_mod, b_mod):
    depth, d, nout = w_mod.shape
    b = c.shape[0]
    assert b + 1 <= SUBLANES
    rows = jnp.zeros((SUBLANES, d), F32).at[:b].set(c).at[b].set(c_ctx)
    tn = 1536
    raw = pl.pallas_call(
        _mod_kernel,
        grid=(depth, nout // tn),
        in_specs=[pl.BlockSpec((SUBLANES, d), lambda i, j: (0, 0)),
                  pl.BlockSpec((1, d, tn), lambda i, j: (i, 0, j)),
                  pl.BlockSpec((1, 1, tn), lambda i, j: (i, 0, j))],
        out_specs=pl.BlockSpec((1, SUBLANES, tn), lambda i, j: (i, 0, j)),
        out_shape=jax.ShapeDtypeStruct((depth, SUBLANES, nout), F32),
        compiler_params=_params(("arbitrary", "arbitrary")),
        name="modulation",
    )(rows, w_mod, b_mod.reshape(depth, 1, nout))
    raw = raw.reshape(depth, SUBLANES, N_MOD, d)
    lat = raw[:, :b]
    ctx = jnp.broadcast_to(raw[:, b][:, None], lat.shape)
    return jnp.stack([ctx, lat], axis=2)


def _conv_a_kernel(x_ref, mod_ref, g_ref, w_ref, b_ref, o_ref):
    d = x_ref.shape[-1]
    m = mod_ref[0, 0]
    h = _norm_mod(x_ref[0], g_ref[...], m[0:1], m[1:2])
    u = jnp.dot(h.astype(BF16), w_ref[...], preferred_element_type=F32) + b_ref[...]
    o_ref[0] = u[:, :d] * jax.nn.sigmoid(u[:, d:])


def _conv_b_kernel(x_ref, mod_ref, gp_ref, gc_ref, gn_ref, dw_ref, dwb_ref, lng_ref, lnb_ref,
                   w_ref, b_ref, o_ref, gbuf, cbuf):
    t = pl.program_id(1)
    nt = pl.num_programs(1)
    tr, d = gc_ref.shape[1], gc_ref.shape[2]
    seg_first = t <= 1
    seg_last = jnp.logical_or(t == 0, t == nt - 1)
    gbuf[0, 0:HALO, :] = jnp.where(seg_first, 0.0, gp_ref[0])
    gbuf[0, HALO:HALO + tr, :] = gc_ref[0]
    gbuf[0, HALO + tr:HALO + tr + HALO, :] = jnp.where(seg_last, 0.0, gn_ref[0])
    span = tr + 2 * HALO - SUBLANES
    for r in range(1, SUBLANES):
        gbuf[r, 0:span, :] = gbuf[0, r:r + span, :]

    rc = 32
    base = HALO - CONV_HALF

    def chunk(i, carry):
        r0 = pl.multiple_of(i * rc, rc)
        for c in range(d // LANES):
            cs = slice(c * LANES, (c + 1) * LANES)
            acc = jnp.zeros((rc, LANES), F32)
            for k in range(CONV_WIDTH):
                off = base + k
                rows = pl.ds(r0 + (off // SUBLANES) * SUBLANES, rc)
                acc = acc + dw_ref[k:k + 1, cs] * gbuf[off % SUBLANES, rows, cs]
            cbuf[pl.ds(r0, rc), cs] = acc
        return carry

    lax.fori_loop(0, tr // rc, chunk, 0)

    u = cbuf[...] + dwb_ref[...]
    mu = jnp.mean(u, axis=-1, keepdims=True)
    uc = u - mu
    var = jnp.mean(uc * uc, axis=-1, keepdims=True)
    v = uc * lax.rsqrt(var + EPS) * lng_ref[...] + lnb_ref[...]
    v = v * jax.nn.sigmoid(v)
    y = jnp.dot(v.astype(BF16), w_ref[...], preferred_element_type=F32) + b_ref[...]
    o_ref[0] = x_ref[0] + mod_ref[0, 0][2:3] * y


def _conv_mixer(x, mods, g, pw1, pw1_b, dw, dw_b, ln_g, ln_b, pw2, pw2_b):
    b, n, d = x.shape
    nt = n // TR
    grid = (b, nt)
    glu = pl.pallas_call(
        _conv_a_kernel, grid=grid,
        in_specs=[_row_spec(TR, d), _seg_mod_spec(d), _full_spec((1, d)),
                  _full_spec((d, 2 * d)), _full_spec((1, 2 * d))],
        out_specs=_row_spec(TR, d),
        out_shape=jax.ShapeDtypeStruct((b, n, d), F32),
        compiler_params=_params(("parallel", "parallel")),
        name="conv_glu",
    )(x, mods, g.reshape(1, d), pw1.astype(BF16), pw1_b.reshape(1, 2 * d))

    hb = TR // HALO
    nh = n // HALO
    dwp = jnp.zeros((32, d), F32).at[:CONV_WIDTH].set(dw)
    return pl.pallas_call(
        _conv_b_kernel, grid=grid,
        in_specs=[_row_spec(TR, d), _seg_mod_spec(d),
                  pl.BlockSpec((1, HALO, d), lambda bb, t: (bb, jnp.maximum(t * hb - 1, 0), 0)),
                  _row_spec(TR, d),
                  pl.BlockSpec((1, HALO, d), lambda bb, t: (bb, jnp.minimum((t + 1) * hb, nh - 1), 0)),
                  _full_spec((32, d)), _full_spec((1, d)), _full_spec((1, d)), _full_spec((1, d)),
                  _full_spec((d, d)), _full_spec((1, d))],
        out_specs=_row_spec(TR, d),
        out_shape=jax.ShapeDtypeStruct((b, n, d), F32),
        scratch_shapes=[pltpu.VMEM((SUBLANES, TR + 2 * HALO, d), F32), pltpu.VMEM((TR, d), F32)],
        compiler_params=_params(("parallel", "parallel")),
        name="conv_dw_pw2",
    )(x, mods, glu, glu, glu, dwp, dw_b.reshape(1, d), ln_g.reshape(1, d), ln_b.reshape(1, d),
      pw2.astype(BF16), pw2_b.reshape(1, d))


def _rope_tables(n, ctx_len):
    pos = jnp.arange(n - ctx_len, dtype=I32)
    inv = ROPE_BASE ** (-jnp.arange(ROPE_PAIRS, dtype=F32) / ROPE_PAIRS)
    ar = (pos // GRID_W).astype(F32)[:, None] * inv
    ac = (pos % GRID_W).astype(F32)[:, None] * inv
    ang = jnp.concatenate([ar, ar, ac, ac], axis=-1)
    sign = jnp.tile(jnp.concatenate([-jnp.ones((ROPE_PAIRS,), F32), jnp.ones((ROPE_PAIRS,), F32)]), 2)
    cos = jnp.concatenate([jnp.ones((ctx_len, HEAD_DIM), F32), jnp.cos(ang)], axis=0)
    sin = jnp.concatenate([jnp.zeros((ctx_len, HEAD_DIM), F32), jnp.sin(ang) * sign], axis=0)
    return jnp.tile(cos, (1, 2)), jnp.tile(sin, (1, 2))


def _attn_a_kernel(x_ref, mod_ref, g_ref, w_ref, cos_ref, sin_ref, q_ref, k_ref, v_ref):
    d = x_ref.shape[-1]
    nk = k_ref.shape[-1]
    m = mod_ref[0, 0]
    h = _norm_mod(x_ref[0], g_ref[...], m[0:1], m[1:2])
    t = jnp.dot(h.astype(BF16), w_ref[...], preferred_element_type=F32)
    cos = cos_ref[...]
    sin = sin_ref[...]
    lane = lax.broadcasted_iota(I32, cos.shape, 1)
    first_half = (lane % (2 * ROPE_PAIRS)) < ROPE_PAIRS

    def rope(xg):
        partner = jnp.where(first_half, pltpu.roll(xg, LANES - ROPE_PAIRS, 1), pltpu.roll(xg, ROPE_PAIRS, 1))
        return xg * cos + partner * sin

    for j in range(d // LANES):
        cs = slice(j * LANES, (j + 1) * LANES)
        q_ref[0, :, cs] = rope(t[:, cs]).astype(BF16)
    for j in range(nk // LANES):
        cs = slice(j * LANES, (j + 1) * LANES)
        k_ref[0, :, cs] = rope(t[:, d + j * LANES:d + (j + 1) * LANES]).astype(BF16)
    v_ref[0] = t[:, d + nk:].astype(BF16)


def _attn_b_kernel(sink_ref, x_ref, mod_ref, q_ref, kc_ref, vc_ref, k0_ref, k1_ref, k2_ref,
                   v0_ref, v1_ref, v2_ref, wo_ref, o_ref, *, ctx_len, n_rows):
    t = pl.program_id(1)
    tq = q_ref.shape[1]
    nctx = kc_ref.shape[1]
    nkeys = nctx + 3 * tq

    rq = t * tq + lax.broadcasted_iota(I32, (tq, nkeys), 0)
    col = lax.broadcasted_iota(I32, (tq, nkeys), 1)
    rk = (t - 1) * tq + (col - nctx)
    win_ok = (rk >= ctx_len) & (rk < n_rows) & (jnp.abs(rq - rk) <= WINDOW) & (t * tq >= ctx_len)
    bias = jnp.where((col < nctx) | win_ok, 0.0, NEG).astype(F32)

    lane = lax.broadcasted_iota(I32, (tq, LANES), 1)
    lo = lane < HEAD_DIM

    out_cols = []
    for g in range(N_KV_HEADS):
        gs = slice(g * LANES, (g + 1) * LANES)
        kg = jnp.concatenate([kc_ref[0, :, gs], k0_ref[0, :, gs], k1_ref[0, :, gs], k2_ref[0, :, gs]], axis=0)
        vg = jnp.concatenate([vc_ref[0, :, gs], v0_ref[0, :, gs], v1_ref[0, :, gs], v2_ref[0, :, gs]], axis=0)
        qa = q_ref[0, :, (2 * g) * LANES:(2 * g + 1) * LANES]
        qb = q_ref[0, :, (2 * g + 1) * LANES:(2 * g + 2) * LANES]
        zero = jnp.zeros_like(qa)
        q4 = jnp.concatenate([jnp.where(lo, qa, zero), jnp.where(lo, zero, qa),
                              jnp.where(lo, qb, zero), jnp.where(lo, zero, qb)], axis=0)
        s = lax.dot_general(q4, kg, (((1,), (1,)), ((), ())), preferred_element_type=F32)
        ps, dens = [], []
        for hh in range(GQA_GROUP):
            sk = sink_ref[g * GQA_GROUP + hh]
            sh = s[hh * tq:(hh + 1) * tq] + bias
            mx = jnp.maximum(jnp.max(sh, axis=-1, keepdims=True), sk)
            p = jnp.exp(sh - mx)
            dens.append(jnp.sum(p, axis=-1, keepdims=True) + jnp.exp(sk - mx))
            ps.append(p.astype(BF16))
        pv = jnp.dot(jnp.concatenate(ps, axis=0), vg, preferred_element_type=F32)
        oh = [pv[hh * tq:(hh + 1) * tq] / dens[hh] for hh in range(GQA_GROUP)]
        out_cols.append(jnp.where(lo, oh[0], oh[1]))
        out_cols.append(jnp.where(lo, oh[2], oh[3]))
    o = jnp.concatenate(out_cols, axis=1).astype(BF16)
    y = jnp.dot(o, wo_ref[...], preferred_element_type=F32)
    o_ref[0] = x_ref[0] + mod_ref[0, 0][2:3] * y


def _attn_mixer(x, mods, g, w_qkv, w_o, sink, ctx_len):
    b, n, d = x.shape
    kvd = N_KV_HEADS * HEAD_DIM
    assert ctx_len == TR and d == 2 * N_KV_HEADS * LANES
    scale = HEAD_DIM ** -0.5

    def dup(w):
        w = w.reshape(d, N_KV_HEADS, HEAD_DIM)
        return jnp.concatenate([w, w], axis=-1).reshape(d, N_KV_HEADS * LANES)

    nk = N_KV_HEADS * LANES
    w_all = jnp.concatenate([w_qkv[:, :d] * scale, dup(w_qkv[:, d:d + kvd]), dup(w_qkv[:, d + kvd:])],
                            axis=1).astype(BF16)
    cos, sin = _rope_tables(n, ctx_len)
    q, k, v = pl.pallas_call(
        _attn_a_kernel, grid=(b, n // TR),
        in_specs=[_row_spec(TR, d), _seg_mod_spec(d), _full_spec((1, d)), _full_spec((d, d + 2 * nk)),
                  pl.BlockSpec((TR, LANES), lambda bb, t: (t, 0)),
                  pl.BlockSpec((TR, LANES), lambda bb, t: (t, 0))],
        out_specs=[_row_spec(TR, d), _row_spec(TR, nk), _row_spec(TR, nk)],
        out_shape=[jax.ShapeDtypeStruct((b, n, d), BF16), jax.ShapeDtypeStruct((b, n, nk), BF16),
                   jax.ShapeDtypeStruct((b, n, nk), BF16)],
        compiler_params=_params(("parallel", "parallel")),
        name="attn_qkv",
    )(x, mods, g.reshape(1, d), w_all, cos, sin)

    ntq = n // TQ
    per_seg = TR // TQ

    def win(off):
        return pl.BlockSpec((1, TQ, nk), lambda bb, t, s: (bb, jnp.clip(t + off, 0, ntq - 1), 0))

    ctx_spec = pl.BlockSpec((1, ctx_len, nk), lambda bb, t, s: (bb, 0, 0))
    grid_spec = pltpu.PrefetchScalarGridSpec(
        num_scalar_prefetch=1, grid=(b, ntq),
        in_specs=[pl.BlockSpec((1, TQ, d), lambda bb, t, s: (bb, t, 0)),
                  pl.BlockSpec((1, 1, N_MOD, d), lambda bb, t, s: (bb, jnp.minimum(t // per_seg, 1), 0, 0)),
                  pl.BlockSpec((1, TQ, d), lambda bb, t, s: (bb, t, 0)),
                  ctx_spec, ctx_spec, win(-1), win(0), win(1), win(-1), win(0), win(1),
                  pl.BlockSpec((d, d), lambda bb, t, s: (0, 0))],
        out_specs=pl.BlockSpec((1, TQ, d), lambda bb, t, s: (bb, t, 0)))
    return pl.pallas_call(
        functools.partial(_attn_b_kernel, ctx_len=ctx_len, n_rows=n),
        grid_spec=grid_spec,
        out_shape=jax.ShapeDtypeStruct((b, n, d), F32),
        compiler_params=_params(("parallel", "parallel")),
        name="attn_core",
    )(sink.astype(F32), x, mods, q, k, v, k, k, k, v, v, v, w_o.astype(BF16))


def _lru_a_kernel(x_ref, mod_ref, g_ref, w_ref, u_ref, gg_ref):
    d = x_ref.shape[-1]
    m = mod_ref[0, 0]
    h = _norm_mod(x_ref[0], g_ref[...], m[0:1], m[1:2])
    t = jnp.dot(h.astype(BF16), w_ref[...], preferred_element_type=F32)
    u_ref[0] = t[:, :d]
    gg_ref[0] = jax.nn.gelu(t[:, d:])


def _lru_gates_scan(back, t, u_ref, cw_ref, cb_ref, wg_ref, ba_ref, bx_ref, lam_ref,
                    ubuf, halo, hcar, abuf, bbuf, hbuf):
    tr, d = u_ref.shape[1], u_ref.shape[2]
    nb = d // LRU_BLOCKS

    @pl.when(t <= 1)
    def _():
        halo[...] = jnp.zeros_like(halo)

    @pl.when(t == 0)
    def _():
        hcar[...] = jnp.zeros_like(hcar)

    u = u_ref[0]
    if not back:
        ubuf[0:SUBLANES, :] = halo[...]
        ubuf[SUBLANES:SUBLANES + tr, :] = u
        halo[...] = u[tr - SUBLANES:, :]
        taps = [ubuf[SUBLANES - (LRU_CONV_W - 1) + k:SUBLANES - (LRU_CONV_W - 1) + k + tr, :]
                for k in range(LRU_CONV_W)]
    else:
        ubuf[0:tr, :] = u
        ubuf[tr:tr + SUBLANES, :] = halo[...]
        halo[...] = u[:SUBLANES, :]
        taps = [ubuf[(LRU_CONV_W - 1) - k:(LRU_CONV_W - 1) - k + tr, :] for k in range(LRU_CONV_W)]
    cc = cb_ref[...] + taps[0] * cw_ref[0:1, :]
    for k in range(1, LRU_CONV_W):
        cc = cc + taps[k] * cw_ref[k:k + 1, :]

    ccb = cc.astype(BF16)
    lam = lam_ref[...]
    neg_c_softplus = -LRU_C * (jnp.maximum(-lam, 0.0) + jnp.log(1.0 + jnp.exp(-jnp.abs(lam))))
    for blk in range(LRU_BLOCKS):
        cs = slice(blk * nb, (blk + 1) * nb)
        z = jnp.dot(ccb[:, cs], wg_ref[blk], preferred_element_type=F32)
        r = jax.nn.sigmoid(z[:, :nb] + ba_ref[:, cs])
        gi = jax.nn.sigmoid(z[:, nb:] + bx_ref[:, cs])
        log_a = neg_c_softplus[:, cs] * r
        a = jnp.exp(log_a)
        abuf[:, cs] = a
        bbuf[:, cs] = jnp.sqrt(1.0 - a * a) * (gi * cc[:, cs])

    row = lax.broadcasted_iota(I32, (SUBLANES, d), 0)
    nchunk = tr // SUBLANES

    def chunk(i, h):
        ci = (nchunk - 1 - i) if back else i
        r0 = pl.multiple_of(ci * SUBLANES, SUBLANES)
        a = abuf[pl.ds(r0, SUBLANES), :]
        b = bbuf[pl.ds(r0, SUBLANES), :]
        for k in (1, 2, 4):
            if back:
                sh, ok = SUBLANES - k, row < SUBLANES - k
            else:
                sh, ok = k, row >= k
            a_s = pltpu.roll(a, sh, 0)
            b_s = pltpu.roll(b, sh, 0)
            b = jnp.where(ok, a * b_s + b, b)
            a = jnp.where(ok, a * a_s, a)
        hh = a * h + b
        hbuf[pl.ds(r0, SUBLANES), :] = hh
        return hh[0:1, :] if back else hh[SUBLANES - 1:SUBLANES, :]

    hcar[0:1, :] = lax.fori_loop(0, nchunk, chunk, hcar[0:1, :])


def _lru_fwd_kernel(u_ref, cw_ref, cb_ref, wg_ref, ba_ref, bx_ref, lam_ref, hf_ref,
                    ubuf, halo, hcar, abuf, bbuf, hbuf):
    _lru_gates_scan(False, pl.program_id(1), u_ref, cw_ref, cb_ref, wg_ref, ba_ref, bx_ref, lam_ref,
                    ubuf, halo, hcar, abuf, bbuf, hbuf)
    hf_ref[0] = hbuf[...]


def _lru_bwd_kernel(u_ref, cw_ref, cb_ref, wg_ref, ba_ref, bx_ref, lam_ref, hf_ref, gg_ref, x_ref, mod_ref,
                    wo_ref, o_ref, ubuf, halo, hcar, abuf, bbuf, hbuf):
    _lru_gates_scan(True, pl.program_id(1), u_ref, cw_ref, cb_ref, wg_ref, ba_ref, bx_ref, lam_ref,
                    ubuf, halo, hcar, abuf, bbuf, hbuf)
    y = (hf_ref[0] + hbuf[...]) * gg_ref[0]
    out = jnp.dot(y.astype(BF16), wo_ref[...], preferred_element_type=F32)
    o_ref[0] = x_ref[0] + mod_ref[0, 0][2:3] * out


def _lru_mixer(x, mods, g, w_in, conv_w, conv_b, wa, ba, wx, bx, lam, w_out):
    b, n, d = x.shape
    nt = n // TR
    nb = d // LRU_BLOCKS
    u, gg = pl.pallas_call(
        _lru_a_kernel, grid=(b, nt),
        in_specs=[_row_spec(TR, d), _seg_mod_spec(d), _full_spec((1, d)), _full_spec((d, 2 * d))],
        out_specs=[_row_spec(TR, d), _row_spec(TR, d)],
        out_shape=[jax.ShapeDtypeStruct((b, n, d), F32), jax.ShapeDtypeStruct((b, n, d), F32)],
        compiler_params=_params(("parallel", "parallel")),
        name="lru_in",
    )(x, mods, g.reshape(1, d), w_in.astype(BF16))

    wg = jnp.concatenate([wa, wx], axis=-1).astype(BF16)
    scratch = [pltpu.VMEM((TR + SUBLANES, d), F32), pltpu.VMEM((SUBLANES, d), F32),
               pltpu.VMEM((SUBLANES, d), F32), pltpu.VMEM((TR, d), F32), pltpu.VMEM((TR, d), F32),
               pltpu.VMEM((TR, d), F32)]

    def gate_specs(order):
        return [pl.BlockSpec((1, TR, d), lambda bb, t: (bb, order(t), 0)),
                _full_spec((LRU_CONV_W, d)), _full_spec((1, d)), _full_spec((LRU_BLOCKS, nb, 2 * nb)),
                _full_spec((1, d)), _full_spec((1, d)), _full_spec((1, d))]

    def gate_args(dd):
        return (u, conv_w[dd], conv_b[dd].reshape(1, d), wg[dd], ba[dd].reshape(1, d),
                bx[dd].reshape(1, d), lam[dd].reshape(1, d))

    hf = pl.pallas_call(
        _lru_fwd_kernel, grid=(b, nt),
        in_specs=gate_specs(lambda t: t),
        out_specs=_row_spec(TR, d),
        out_shape=jax.ShapeDtypeStruct((b, n, d), F32),
        scratch_shapes=scratch,
        compiler_params=_params(("arbitrary", "arbitrary")),
        name="lru_fwd",
    )(*gate_args(0))

    def rev(t):
        return jnp.where(t == 0, 0, nt - t)

    def rev_spec():
        return pl.BlockSpec((1, TR, d), lambda bb, t: (bb, rev(t), 0))

    return pl.pallas_call(
        _lru_bwd_kernel, grid=(b, nt),
        in_specs=gate_specs(rev) + [rev_spec(), rev_spec(), rev_spec(),
                                    pl.BlockSpec((1, 1, N_MOD, d), lambda bb, t: (bb, jnp.minimum(t, 1), 0, 0)),
                                    _full_spec((d, d))],
        out_specs=rev_spec(),
        out_shape=jax.ShapeDtypeStruct((b, n, d), F32),
        scratch_shapes=scratch,
        compiler_params=_params(("arbitrary", "arbitrary")),
        name="lru_bwd",
    )(*gate_args(1), hf, gg, x, mods, w_out.astype(BF16))


def _route_kernel(x_ref, mod_ref, g_ref, wr_ref, rb_ref, h_ref, ri_ref, gt_ref, cnt_ref, carry):
    first = jnp.logical_and(pl.program_id(0) == 0, pl.program_id(1) == 0)
    tr = x_ref.shape[1]

    @pl.when(first)
    def _():
        carry[...] = jnp.zeros_like(carry)

    m = mod_ref[0, 0]
    h = _norm_mod(x_ref[0], g_ref[...], m[3:4], m[4:5])
    h_ref[0] = h
    logit = lax.dot_general(wr_ref[...], h.astype(BF16), (((1,), (1,)), ((), ())),
                            preferred_element_type=F32)
    s = jax.nn.sigmoid(logit)
    sel = s + rb_ref[...]
    sj = [s[SUBLANES * j:SUBLANES * (j + 1)] for j in range(EXPERTS_PER_GROUP)]
    cj = [sel[SUBLANES * j:SUBLANES * (j + 1)] for j in range(EXPERTS_PER_GROUP)]

    hi1, lo1 = jnp.maximum(cj[0], cj[1]), jnp.minimum(cj[0], cj[1])
    hi2, lo2 = jnp.maximum(cj[2], cj[3]), jnp.minimum(cj[2], cj[3])
    top1 = jnp.maximum(hi1, hi2)
    top2 = jnp.maximum(jnp.minimum(hi1, hi2), jnp.maximum(lo1, lo2))
    row = lax.broadcasted_iota(I32, (SUBLANES, tr), 0)
    gscore = jnp.where(row < N_GROUPS, top1 + top2, -jnp.inf)
    gmax = jnp.max(gscore, axis=0, keepdims=True)
    grp = jnp.min(jnp.where(gscore == gmax, row, SUBLANES), axis=0, keepdims=True)
    pick = row == grp
    c = [jnp.sum(jnp.where(pick, v, 0.0), axis=0, keepdims=True) for v in cj]
    w = [jnp.sum(jnp.where(pick, v, 0.0), axis=0, keepdims=True) for v in sj]

    def argtop(vals):
        best, bi, bw = vals[0], jnp.zeros_like(grp), w[0]
        for j in range(1, EXPERTS_PER_GROUP):
            better = vals[j] > best
            best = jnp.where(better, vals[j], best)
            bi = jnp.where(better, j, bi)
            bw = jnp.where(better, w[j], bw)
        return bi, bw

    i0, w0 = argtop(c)
    i1, w1 = argtop([jnp.where(i0 == j, -jnp.inf, c[j]) for j in range(EXPERTS_PER_GROUP)])
    e0 = grp * EXPERTS_PER_GROUP + i0
    e1 = grp * EXPERTS_PER_GROUP + i1
    wsum = w0 + w1
    g0 = w0 / wsum
    g1 = w1 / wsum

    erow = lax.broadcasted_iota(I32, (N_EXPERTS, tr), 0)
    member = jnp.logical_or(erow == e0, erow == e1)
    upper = (lax.broadcasted_iota(I32, (tr, tr), 0) < lax.broadcasted_iota(I32, (tr, tr), 1))
    before = jnp.dot(member.astype(BF16), upper.astype(BF16), preferred_element_type=F32) + carry[...]
    p0 = jnp.sum(jnp.where(erow == e0, before, 0.0), axis=0, keepdims=True)
    p1 = jnp.sum(jnp.where(erow == e1, before, 0.0), axis=0, keepdims=True)
    total = carry[...] + jnp.sum(member.astype(F32), axis=1, keepdims=True)
    carry[...] = total
    cnt_ref[...] = total.astype(I32)

    irow = lax.broadcasted_iota(I32, (SUBLANES, tr), 0)
    ri_ref[0] = jnp.where(irow == 0, e0, jnp.where(irow == 1, e1, jnp.where(
        irow == 2, p0.astype(I32), jnp.where(irow == 3, p1.astype(I32), 0))))
    grow = lax.broadcasted_iota(I32, (LANES, tr), 0)
    gmat = jnp.where(grow == 0, g0, jnp.where(grow == 1, g1, 0.0))
    gt_ref[0] = gmat.T


def _dispatch_kernel(slot_ref, zrow_ref, h_ref, xs_ref, zbuf, sem, zsem, *, n_tok, chunk):
    zbuf[...] = jnp.zeros_like(zbuf)
    zsrc = zbuf.at[pl.ds(0, 1)]
    for e in range(N_EXPERTS):
        first, npad = zrow_ref[e], zrow_ref[N_EXPERTS + e]

        def zero_row(i, carry, first=first):
            pltpu.make_async_copy(zsrc, xs_ref.at[pl.ds(first + i, 1)], zsem).start()
            return carry

        def zero_wait(i, carry):
            pltpu.make_async_copy(zsrc, xs_ref.at[pl.ds(0, 1)], zsem).wait()
            return carry

        lax.fori_loop(0, npad, zero_row, 0)
        lax.fori_loop(0, npad, zero_wait, 0)

    nchunks = n_tok // chunk

    def wait_chunk(par):
        for _ in range(TOP_K):
            pltpu.make_async_copy(h_ref.at[pl.ds(0, chunk)], xs_ref.at[pl.ds(0, chunk)], sem.at[par]).wait()

    def body(c, carry):
        par = c % 2

        @pl.when(c >= 2)
        def _():
            wait_chunk(par)

        def issue(i, carry2):
            tok = c * chunk + i
            for k in range(TOP_K):
                pltpu.make_async_copy(h_ref.at[pl.ds(tok, 1)], xs_ref.at[pl.ds(slot_ref[k * n_tok + tok], 1)],
                                      sem.at[par]).start()
            return carry2

        lax.fori_loop(0, chunk, issue, 0)
        return carry

    lax.fori_loop(0, nchunks, body, 0)
    for c in range(max(nchunks - 2, 0), nchunks):
        wait_chunk(c % 2)


def _expert_kernel(be_ref, nu_ref, xs_ref, w1_ref, w3_ref, w2_ref, ys_ref):
    @pl.when(pl.program_id(0) < nu_ref[0])
    def _():
        xb = xs_ref[...].astype(BF16)
        a = jnp.dot(xb, w1_ref[0].astype(BF16), preferred_element_type=F32)
        b = jnp.dot(xb, w3_ref[0].astype(BF16), preferred_element_type=F32)
        hid = (a * jax.nn.sigmoid(a)) * b
        ys_ref[...] = jnp.dot(hid.astype(BF16), w2_ref[0].astype(BF16), preferred_element_type=F32)


def _combine_kernel(slot_ref, x_ref, mod_ref, gt_ref, nf_ref, ys_ref, o_ref, ybuf, sem,
                    *, n_tok, rows_per_batch, tile_off, final):
    bb, t = pl.program_id(0), pl.program_id(1)
    nt = pl.num_programs(1)
    tr = x_ref.shape[1]
    step = bb * nt + t
    nsteps = pl.num_programs(0) * nt

    def tok_base(s):
        return (s // nt) * rows_per_batch + (s % nt + tile_off) * tr

    def issue(s):
        par = s % 2
        base = tok_base(s)

        def one(i, carry):
            for k in range(TOP_K):
                pltpu.make_async_copy(ys_ref.at[pl.ds(slot_ref[k * n_tok + base + i], 1)],
                                      ybuf.at[par, k, pl.ds(i, 1)], sem.at[par]).start()
            return carry

        lax.fori_loop(0, tr, one, 0)

    @pl.when(step == 0)
    def _():
        issue(step)

    @pl.when(step + 1 < nsteps)
    def _():
        issue(step + 1)

    par = step % 2
    for k in range(TOP_K):
        pltpu.make_async_copy(ys_ref.at[pl.ds(0, tr)], ybuf.at[par, k], sem.at[par]).wait()

    gt = gt_ref[0]
    y = gt[:, 0:1] * ybuf[par, 0] + gt[:, 1:2] * ybuf[par, 1]
    out = x_ref[0] + mod_ref[0, 0][5:6] * y
    if final:
        out = out * lax.rsqrt(jnp.mean(out * out, axis=-1, keepdims=True) + EPS) * nf_ref[...]
    o_ref[0] = out


def _moe(x, mods, g, wr_perm, rb_perm, w1, w3, w2, norm_f, final, ctx_len):
    b, n, d = x.shape
    nt = n // TR
    n_tok = b * n
    de = w1.shape[-1]
    h, rint, gtab, counts = pl.pallas_call(
        _route_kernel, grid=(b, nt),
        in_specs=[_row_spec(TR, d), _seg_mod_spec(d), _full_spec((1, d)),
                  _full_spec((SUBLANES * EXPERTS_PER_GROUP, d)), _full_spec((SUBLANES * EXPERTS_PER_GROUP, 1))],
        out_specs=[_row_spec(TR, d),
                   pl.BlockSpec((1, SUBLANES, TR), lambda bb, t: (bb * nt + t, 0, 0)),
                   _row_spec(TR, LANES),
                   pl.BlockSpec((N_EXPERTS, 1), lambda bb, t: (0, 0))],
        out_shape=[jax.ShapeDtypeStruct((b, n, d), F32), jax.ShapeDtypeStruct((b * nt, SUBLANES, TR), I32),
                   jax.ShapeDtypeStruct((b, n, LANES), F32), jax.ShapeDtypeStruct((N_EXPERTS, 1), I32)],
        scratch_shapes=[pltpu.VMEM((N_EXPERTS, 1), F32)],
        compiler_params=_params(("arbitrary", "arbitrary")),
        name="moe_route",
    )(x, mods, g.reshape(1, d), wr_perm, rb_perm)

    counts = counts[:, 0]
    padded = (counts + MOE_BLK - 1) // MOE_BLK * MOE_BLK
    pend = jnp.cumsum(padded)
    pstart = pend - padded
    n_slots = n_tok * TOP_K + N_EXPERTS * MOE_BLK
    n_blocks = n_slots // MOE_BLK
    n_used = (pend[-1] // MOE_BLK).astype(I32).reshape(1)
    blk_e = jnp.minimum(jnp.searchsorted(pend, jnp.arange(n_blocks, dtype=I32) * MOE_BLK, side='right'),
                        N_EXPERTS - 1).astype(I32)
    idx = jnp.swapaxes(rint[:, 0:2, :], 0, 1).reshape(TOP_K, n_tok)
    pos = jnp.swapaxes(rint[:, 2:4, :], 0, 1).reshape(TOP_K, n_tok)
    slots = (pstart.astype(I32)[idx] + pos).reshape(TOP_K * n_tok)
    zrow = jnp.concatenate([pstart + counts, padded - counts]).astype(I32)

    chunk = 256
    xs = pl.pallas_call(
        functools.partial(_dispatch_kernel, n_tok=n_tok, chunk=chunk),
        grid_spec=pltpu.PrefetchScalarGridSpec(
            num_scalar_prefetch=2, grid=(1,),
            in_specs=[pl.BlockSpec(memory_space=pl.ANY)],
            out_specs=pl.BlockSpec(memory_space=pl.ANY),
            scratch_shapes=[pltpu.VMEM((SUBLANES, d), F32), pltpu.SemaphoreType.DMA((2,)),
                            pltpu.SemaphoreType.DMA]),
        out_shape=jax.ShapeDtypeStruct((n_slots, d), F32),
        compiler_params=_params(("arbitrary",)),
        name="moe_dispatch",
    )(slots, zrow, h.reshape(n_tok, d))

    def wspec(shape):
        return pl.BlockSpec((1,) + shape, lambda i, be, nu: (be[i], 0, 0))

    def blk_spec():
        return pl.BlockSpec((MOE_BLK, d), lambda i, be, nu: (jnp.minimum(i, nu[0] - 1), 0))

    ys = pl.pallas_call(
        _expert_kernel,
        grid_spec=pltpu.PrefetchScalarGridSpec(
            num_scalar_prefetch=2, grid=(n_blocks,),
            in_specs=[blk_spec(), wspec((d, de)), wspec((d, de)), wspec((de, d))],
            out_specs=blk_spec()),
        out_shape=jax.ShapeDtypeStruct((n_slots, d), F32),
        compiler_params=_params(("arbitrary",)),
        name="moe_experts",
    )(blk_e, n_used, xs, w1, w3, w2)

    tile_off = ctx_len // TR if final else 0
    nt_out = nt - tile_off
    out_rows = nt_out * TR

    def rows(width):
        return pl.BlockSpec((1, TR, width), lambda bb, t, s: (bb, t + tile_off, 0))

    return pl.pallas_call(
        functools.partial(_combine_kernel, n_tok=n_tok, rows_per_batch=n, tile_off=tile_off, final=final),
        grid_spec=pltpu.PrefetchScalarGridSpec(
            num_scalar_prefetch=1, grid=(b, nt_out),
            in_specs=[rows(d),
                      pl.BlockSpec((1, 1, N_MOD, d), lambda bb, t, s: (bb, jnp.minimum(t + tile_off, 1), 0, 0)),
                      rows(LANES), pl.BlockSpec((1, d), lambda bb, t, s: (0, 0)),
                      pl.BlockSpec(memory_space=pl.ANY)],
            out_specs=pl.BlockSpec((1, TR, d), lambda bb, t, s: (bb, t, 0)),
            scratch_shapes=[pltpu.VMEM((2, TOP_K, TR, d), F32), pltpu.SemaphoreType.DMA((2,))]),
        out_shape=jax.ShapeDtypeStruct((b, out_rows, d), F32),
        compiler_params=_params(("arbitrary", "arbitrary")),
        name="moe_combine",
    )(slots, x, mods, gtab, norm_f.reshape(1, d), ys)


def kernel(x, c, ctx, c_ctx, w_mod, b_mod, norm_mix, norm_ffn, norm_f, conv_pw1, conv_pw1_b, conv_dw, conv_dw_b, conv_ln_g, conv_ln_b, conv_pw2, conv_pw2_b, attn_w_qkv, attn_w_o, attn_sink, lru_w_in, lru_conv_w, lru_conv_b, lru_wa, lru_ba, lru_wx, lru_bx, lru_lam, lru_w_out, moe_w_router, moe_router_bias, moe_w1, moe_w3, moe_w2):
    b, s, d = x.shape
    ctx_len = ctx.shape[1]
    depth = w_mod.shape[0]
    assert ctx_len == TR and s % TR == 0 and d % LANES == 0
    assert moe_w_router.shape[1] == N_EXPERTS

    mods = _modulation(c, c_ctx, w_mod, b_mod)
    xs = jnp.concatenate([ctx, x], axis=1)

    wr = jnp.zeros((EXPERTS_PER_GROUP, SUBLANES, d), F32).at[:, :N_GROUPS].set(
        moe_w_router.T.reshape(N_GROUPS, EXPERTS_PER_GROUP, d).swapaxes(0, 1))
    wr = wr.reshape(EXPERTS_PER_GROUP * SUBLANES, d).astype(BF16)
    rb = jnp.zeros((EXPERTS_PER_GROUP, SUBLANES), F32).at[:, :N_GROUPS].set(
        moe_router_bias.astype(F32).reshape(N_GROUPS, EXPERTS_PER_GROUP).T).reshape(-1, 1)

    for i in range(depth):
        kind, slot = i % N_MIXERS, i // N_MIXERS
        m = mods[i]
        if kind == 0:
            xs = _conv_mixer(xs, m, norm_mix[i], conv_pw1[slot], conv_pw1_b[slot], conv_dw[slot], conv_dw_b[slot],
                             conv_ln_g[slot], conv_ln_b[slot], conv_pw2[slot], conv_pw2_b[slot])
        elif kind == 1:
            xs = _attn_mixer(xs, m, norm_mix[i], attn_w_qkv[slot], attn_w_o[slot], attn_sink[slot], ctx_len)
        else:
            xs = _lru_mixer(xs, m, norm_mix[i], lru_w_in[slot], lru_conv_w[slot], lru_conv_b[slot], lru_wa[slot],
                            lru_ba[slot], lru_wx[slot], lru_bx[slot], lru_lam[slot], lru_w_out[slot])
        xs = _moe(xs, m, norm_ffn[i], wr, rb, moe_w1[i], moe_w3[i], moe_w2[i], norm_f, i == depth - 1, ctx_len)
    return xs
```

```python
import functools

import jax
import jax.numpy as jnp
from jax import lax
from jax.experimental import pallas as pl
from jax.experimental.pallas import tpu as pltpu

F32 = jnp.float32
BF16 = jnp.bfloat16
I32 = jnp.int32

EPS = 1e-6
N_MOD = 6
N_MIXERS = 3
GRID_W = 64
CONV_WIDTH = 31
CONV_HALF = (CONV_WIDTH - 1) // 2
HEAD_DIM = 64
N_KV_HEADS = 4
GQA_GROUP = 4
WINDOW = 128
ROPE_BASE = 10000.0
ROPE_PAIRS = HEAD_DIM // 4
LRU_BLOCKS = 8
LRU_CONV_W = 4
LRU_C = 8.0
N_EXPERTS = 16
N_GROUPS = 4
EXPERTS_PER_GROUP = 4
TOP_K = 2

LANES = 128
SUBLANES = 8
TR = 256
TQ = 128
HALO = 16
MOE_BLK = 256
NEG = -1e30
VMEM_LIMIT = 56 * 1024 * 1024


def _params(sem, vmem=VMEM_LIMIT):
    return pltpu.CompilerParams(dimension_semantics=sem, vmem_limit_bytes=vmem)


def _norm_mod(x, g, shift, scale):
    y = x * lax.rsqrt(jnp.mean(x * x, axis=-1, keepdims=True) + EPS)
    return (y * g) * (1.0 + scale) + shift


def _seg_mod_spec(d):
    return pl.BlockSpec((1, 1, N_MOD, d), lambda b, t: (b, jnp.minimum(t, 1), 0, 0))


def _row_spec(tr, d):
    return pl.BlockSpec((1, tr, d), lambda b, t: (b, t, 0))


def _full_spec(shape):
    nd = len(shape)
    return pl.BlockSpec(shape, lambda b, t: (0,) * nd)


def _mod_kernel(c_ref, w_ref, b_ref, o_ref):
    c = c_ref[...]
    sc = c * jax.nn.sigmoid(c)
    o_ref[0] = jnp.dot(sc.astype(BF16), w_ref[0].astype(BF16), preferred_element_type=F32) + b_ref[0]


def _modulation(c, c_ctx, w_mod, b_mod):
    depth, d, nout = w_mod.shape
    b = c.shape[0]
    assert b + 1 <= SUBLANES
    rows = jnp.zeros((SUBLANES, d), F32).at[:b].set(c).at[b].set(c_ctx)
    tn = 1536
    raw = pl.pallas_call(
        _mod_kernel,
        grid=(depth, nout // tn),
        in_specs=[pl.BlockSpec((SUBLANES, d), lambda i, j: (0, 0)),
                  pl.BlockSpec((1, d, tn), lambda i, j: (i, 0, j)),
                  pl.BlockSpec((1, 1, tn), lambda i, j: (i, 0, j))],
        out_specs=pl.BlockSpec((1, SUBLANES, tn), lambda i, j: (i, 0, j)),
        out_shape=jax.ShapeDtypeStruct((depth, SUBLANES, nout), F32),
        compiler_params=_params(("arbitrary", "arbitrary")),
        name="modulation",
    )(rows, w_mod, b_mod.reshape(depth, 1, nout))
    raw = raw.reshape(depth, SUBLANES, N_MOD, d)
    lat = raw[:, :b]
    ctx = jnp.broadcast_to(raw[:, b][:, None], lat.shape)
    return jnp.stack([ctx, lat], axis=2)


def _conv_a_kernel(x_ref, mod_ref, g_ref, w_ref, b_ref, o_ref):
    d = x_ref.shape[-1]
    m = mod_ref[0, 0]
    h = _norm_mod(x_ref[0], g_ref[...], m[0:1], m[1:2])
    u = jnp.dot(h.astype(BF16), w_ref[...], preferred_element_type=F32) + b_ref[...]
    o_ref[0] = u[:, :d] * jax.nn.sigmoid(u[:, d:])


def _conv_b_kernel(x_ref, mod_ref, gp_ref, gc_ref, gn_ref, dw_ref, dwb_ref, lng_ref, lnb_ref,
                   w_ref, b_ref, o_ref, gbuf, cbuf):
    t = pl.program_id(1)
    nt = pl.num_programs(1)
    tr, d = gc_ref.shape[1], gc_ref.shape[2]
    seg_first = t <= 1
    seg_last = jnp.logical_or(t == 0, t == nt - 1)
    gbuf[0, 0:HALO, :] = jnp.where(seg_first, 0.0, gp_ref[0])
    gbuf[0, HALO:HALO + tr, :] = gc_ref[0]
    gbuf[0, HALO + tr:HALO + tr + HALO, :] = jnp.where(seg_last, 0.0, gn_ref[0])
    span = tr + 2 * HALO - SUBLANES
    for r in range(1, SUBLANES):
        gbuf[r, 0:span, :] = gbuf[0, r:r + span, :]

    rc = 32
    base = HALO - CONV_HALF

    def chunk(i, carry):
        r0 = pl.multiple_of(i * rc, rc)
        for c in range(d // LANES):
            cs = slice(c * LANES, (c + 1) * LANES)
            acc = jnp.zeros((rc, LANES), F32)
            for k in range(CONV_WIDTH):
                off = base + k
                rows = pl.ds(r0 + (off // SUBLANES) * SUBLANES, rc)
                acc = acc + dw_ref[k:k + 1, cs] * gbuf[off % SUBLANES, rows, cs]
            cbuf[pl.ds(r0, rc), cs] = acc
        return carry

    lax.fori_loop(0, tr // rc, chunk, 0)

    u = cbuf[...] + dwb_ref[...]
    mu = jnp.mean(u, axis=-1, keepdims=True)
    uc = u - mu
    var = jnp.mean(uc * uc, axis=-1, keepdims=True)
    v = uc * lax.rsqrt(var + EPS) * lng_ref[...] + lnb_ref[...]
    v = v * jax.nn.sigmoid(v)
    y = jnp.dot(v.astype(BF16), w_ref[...], preferred_element_type=F32) + b_ref[...]
    o_ref[0] = x_ref[0] + mod_ref[0, 0][2:3] * y


def _conv_mixer(x, mods, g, pw1, pw1_b, dw, dw_b, ln_g, ln_b, pw2, pw2_b):
    b, n, d = x.shape
    nt = n // TR
    grid = (b, nt)
    glu = pl.pallas_call(
        _conv_a_kernel, grid=grid,
        in_specs=[_row_spec(TR, d), _seg_mod_spec(d), _full_spec((1, d)),
                  _full_spec((d, 2 * d)), _full_spec((1, 2 * d))],
        out_specs=_row_spec(TR, d),
        out_shape=jax.ShapeDtypeStruct((b, n, d), F32),
        compiler_params=_params(("parallel", "parallel")),
        name="conv_glu",
    )(x, mods, g.reshape(1, d), pw1.astype(BF16), pw1_b.reshape(1, 2 * d))

    hb = TR // HALO
    nh = n // HALO
    dwp = jnp.zeros((32, d), F32).at[:CONV_WIDTH].set(dw)
    return pl.pallas_call(
        _conv_b_kernel, grid=grid,
        in_specs=[_row_spec(TR, d), _seg_mod_spec(d),
                  pl.BlockSpec((1, HALO, d), lambda bb, t: (bb, jnp.maximum(t * hb - 1, 0), 0)),
                  _row_spec(TR, d),
                  pl.BlockSpec((1, HALO, d), lambda bb, t: (bb, jnp.minimum((t + 1) * hb, nh - 1), 0)),
                  _full_spec((32, d)), _full_spec((1, d)), _full_spec((1, d)), _full_spec((1, d)),
                  _full_spec((d, d)), _full_spec((1, d))],
        out_specs=_row_spec(TR, d),
        out_shape=jax.ShapeDtypeStruct((b, n, d), F32),
        scratch_shapes=[pltpu.VMEM((SUBLANES, TR + 2 * HALO, d), F32), pltpu.VMEM((TR, d), F32)],
        compiler_params=_params(("parallel", "parallel")),
        name="conv_dw_pw2",
    )(x, mods, glu, glu, glu, dwp, dw_b.reshape(1, d), ln_g.reshape(1, d), ln_b.reshape(1, d),
      pw2.astype(BF16), pw2_b.reshape(1, d))


def _rope_tables(n, ctx_len):
    pos = jnp.arange(n - ctx_len, dtype=I32)
    inv = ROPE_BASE ** (-jnp.arange(ROPE_PAIRS, dtype=F32) / ROPE_PAIRS)
    ar = (pos // GRID_W).astype(F32)[:, None] * inv
    ac = (pos % GRID_W).astype(F32)[:, None] * inv
    ang = jnp.concatenate([ar, ar, ac, ac], axis=-1)
    sign = jnp.tile(jnp.concatenate([-jnp.ones((ROPE_PAIRS,), F32), jnp.ones((ROPE_PAIRS,), F32)]), 2)
    cos = jnp.concatenate([jnp.ones((ctx_len, HEAD_DIM), F32), jnp.cos(ang)], axis=0)
    sin = jnp.concatenate([jnp.zeros((ctx_len, HEAD_DIM), F32), jnp.sin(ang) * sign], axis=0)
    return jnp.tile(cos, (1, 2)), jnp.tile(sin, (1, 2))


def _attn_a_kernel(x_ref, mod_ref, g_ref, w_ref, cos_ref, sin_ref, q_ref, k_ref, v_ref):
    d = x_ref.shape[-1]
    nk = k_ref.shape[-1]
    m = mod_ref[0, 0]
    h = _norm_mod(x_ref[0], g_ref[...], m[0:1], m[1:2])
    t = jnp.dot(h.astype(BF16), w_ref[...], preferred_element_type=F32)
    cos = cos_ref[...]
    sin = sin_ref[...]
    lane = lax.broadcasted_iota(I32, cos.shape, 1)
    first_half = (lane % (2 * ROPE_PAIRS)) < ROPE_PAIRS

    def rope(xg):
        partner = jnp.where(first_half, pltpu.roll(xg, LANES - ROPE_PAIRS, 1), pltpu.roll(xg, ROPE_PAIRS, 1))
        return xg * cos + partner * sin

    for j in range(d // LANES):
        cs = slice(j * LANES, (j + 1) * LANES)
        q_ref[0, :, cs] = rope(t[:, cs]).astype(BF16)
    for j in range(nk // LANES):
        cs = slice(j * LANES, (j + 1) * LANES)
        k_ref[0, :, cs] = rope(t[:, d + j * LANES:d + (j + 1) * LANES]).astype(BF16)
    v_ref[0] = t[:, d + nk:].astype(BF16)


def _attn_b_kernel(sink_ref, x_ref, mod_ref, q_ref, kc_ref, vc_ref, k0_ref, k1_ref, k2_ref,
                   v0_ref, v1_ref, v2_ref, wo_ref, o_ref, *, ctx_len, n_rows):
    t = pl.program_id(1)
    tq = q_ref.shape[1]
    nctx = kc_ref.shape[1]
    nkeys = nctx + 3 * tq

    rq = t * tq + lax.broadcasted_iota(I32, (tq, nkeys), 0)
    col = lax.broadcasted_iota(I32, (tq, nkeys), 1)
    rk = (t - 1) * tq + (col - nctx)
    win_ok = (rk >= ctx_len) & (rk < n_rows) & (jnp.abs(rq - rk) <= WINDOW) & (t * tq >= ctx_len)
    bias = jnp.where((col < nctx) | win_ok, 0.0, NEG).astype(F32)

    lane = lax.broadcasted_iota(I32, (tq, LANES), 1)
    lo = lane < HEAD_DIM

    out_cols = []
    for g in range(N_KV_HEADS):
        gs = slice(g * LANES, (g + 1) * LANES)
        kg = jnp.concatenate([kc_ref[0, :, gs], k0_ref[0, :, gs], k1_ref[0, :, gs], k2_ref[0, :, gs]], axis=0)
        vg = jnp.concatenate([vc_ref[0, :, gs], v0_ref[0, :, gs], v1_ref[0, :, gs], v2_ref[0, :, gs]], axis=0)
        qa = q_ref[0, :, (2 * g) * LANES:(2 * g + 1) * LANES]
        qb = q_ref[0, :, (2 * g + 1) * LANES:(2 * g + 2) * LANES]
        zero = jnp.zeros_like(qa)
        q4 = jnp.concatenate([jnp.where(lo, qa, zero), jnp.where(lo, zero, qa),
                              jnp.where(lo, qb, zero), jnp.where(lo, zero, qb)], axis=0)
        s = lax.dot_general(q4, kg, (((1,), (1,)), ((), ())), preferred_element_type=F32)
        ps, dens = [], []
        for hh in range(GQA_GROUP):
            sk = sink_ref[g * GQA_GROUP + hh]
            sh = s[hh * tq:(hh + 1) * tq] + bias
            mx = jnp.maximum(jnp.max(sh, axis=-1, keepdims=True), sk)
            p = jnp.exp(sh - mx)
            dens.append(jnp.sum(p, axis=-1, keepdims=True) + jnp.exp(sk - mx))
            ps.append(p.astype(BF16))
        pv = jnp.dot(jnp.concatenate(ps, axis=0), vg, preferred_element_type=F32)
        oh = [pv[hh * tq:(hh + 1) * tq] / dens[hh] for hh in range(GQA_GROUP)]
        out_cols.append(jnp.where(lo, oh[0], oh[1]))
        out_cols.append(jnp.where(lo, oh[2], oh[3]))
    o = jnp.concatenate(out_cols, axis=1).astype(BF16)
    y = jnp.dot(o, wo_ref[...], preferred_element_type=F32)
    o_ref[0] = x_ref[0] + mod_ref[0, 0][2:3] * y


def _attn_mixer(x, mods, g, w_qkv, w_o, sink, ctx_len):
    b, n, d = x.shape
    kvd = N_KV_HEADS * HEAD_DIM
    assert ctx_len == TR and d == 2 * N_KV_HEADS * LANES
    scale = HEAD_DIM ** -0.5

    def dup(w):
        w = w.reshape(d, N_KV_HEADS, HEAD_DIM)
        return jnp.concatenate([w, w], axis=-1).reshape(d, N_KV_HEADS * LANES)

    nk = N_KV_HEADS * LANES
    w_all = jnp.concatenate([w_qkv[:, :d] * scale, dup(w_qkv[:, d:d + kvd]), dup(w_qkv[:, d + kvd:])],
                            axis=1).astype(BF16)
    cos, sin = _rope_tables(n, ctx_len)
    q, k, v = pl.pallas_call(
        _attn_a_kernel, grid=(b, n // TR),
        in_specs=[_row_spec(TR, d), _seg_mod_spec(d), _full_spec((1, d)), _full_spec((d, d + 2 * nk)),
                  pl.BlockSpec((TR, LANES), lambda bb, t: (t, 0)),
                  pl.BlockSpec((TR, LANES), lambda bb, t: (t, 0))],
        out_specs=[_row_spec(TR, d), _row_spec(TR, nk), _row_spec(TR, nk)],
        out_shape=[jax.ShapeDtypeStruct((b, n, d), BF16), jax.ShapeDtypeStruct((b, n, nk), BF16),
                   jax.ShapeDtypeStruct((b, n, nk), BF16)],
        compiler_params=_params(("parallel", "parallel")),
        name="attn_qkv",
    )(x, mods, g.reshape(1, d), w_all, cos, sin)

    ntq = n // TQ
    per_seg = TR // TQ

    def win(off):
        return pl.BlockSpec((1, TQ, nk), lambda bb, t, s: (bb, jnp.clip(t + off, 0, ntq - 1), 0))

    ctx_spec = pl.BlockSpec((1, ctx_len, nk), lambda bb, t, s: (bb, 0, 0))
    grid_spec = pltpu.PrefetchScalarGridSpec(
        num_scalar_prefetch=1, grid=(b, ntq),
        in_specs=[pl.BlockSpec((1, TQ, d), lambda bb, t, s: (bb, t, 0)),
                  pl.BlockSpec((1, 1, N_MOD, d), lambda bb, t, s: (bb, jnp.minimum(t // per_seg, 1), 0, 0)),
                  pl.BlockSpec((1, TQ, d), lambda bb, t, s: (bb, t, 0)),
                  ctx_spec, ctx_spec, win(-1), win(0), win(1), win(-1), win(0), win(1),
                  pl.BlockSpec((d, d), lambda bb, t, s: (0, 0))],
        out_specs=pl.BlockSpec((1, TQ, d), lambda bb, t, s: (bb, t, 0)))
    return pl.pallas_call(
        functools.partial(_attn_b_kernel, ctx_len=ctx_len, n_rows=n),
        grid_spec=grid_spec,
        out_shape=jax.ShapeDtypeStruct((b, n, d), F32),
        compiler_params=_params(("parallel", "parallel")),
        name="attn_core",
    )(sink.astype(F32), x, mods, q, k, v, k, k, k, v, v, v, w_o.astype(BF16))


def _lru_a_kernel(x_ref, mod_ref, g_ref, w_ref, u_ref, gg_ref):
    d = x_ref.shape[-1]
    m = mod_ref[0, 0]
    h = _norm_mod(x_ref[0], g_ref[...], m[0:1], m[1:2])
    t = jnp.dot(h.astype(BF16), w_ref[...], preferred_element_type=F32)
    u_ref[0] = t[:, :d]
    gg_ref[0] = jax.nn.gelu(t[:, d:])


def _lru_gates_scan(back, t, u_ref, cw_ref, cb_ref, wg_ref, ba_ref, bx_ref, lam_ref,
                    ubuf, halo, hcar, abuf, bbuf, hbuf):
    tr, d = u_ref.shape[1], u_ref.shape[2]
    nb = d // LRU_BLOCKS

    @pl.when(t <= 1)
    def _():
        halo[...] = jnp.zeros_like(halo)

    @pl.when(t == 0)
    def _():
        hcar[...] = jnp.zeros_like(hcar)

    u = u_ref[0]
    if not back:
        ubuf[0:SUBLANES, :] = halo[...]
        ubuf[SUBLANES:SUBLANES + tr, :] = u
        halo[...] = u[tr - SUBLANES:, :]
        taps = [ubuf[SUBLANES - (LRU_CONV_W - 1) + k:SUBLANES - (LRU_CONV_W - 1) + k + tr, :]
                for k in range(LRU_CONV_W)]
    else:
        ubuf[0:tr, :] = u
        ubuf[tr:tr + SUBLANES, :] = halo[...]
        halo[...] = u[:SUBLANES, :]
        taps = [ubuf[(LRU_CONV_W - 1) - k:(LRU_CONV_W - 1) - k + tr, :] for k in range(LRU_CONV_W)]
    cc = cb_ref[...] + taps[0] * cw_ref[0:1, :]
    for k in range(1, LRU_CONV_W):
        cc = cc + taps[k] * cw_ref[k:k + 1, :]

    ccb = cc.astype(BF16)
    lam = lam_ref[...]
    neg_c_softplus = -LRU_C * (jnp.maximum(-lam, 0.0) + jnp.log(1.0 + jnp.exp(-jnp.abs(lam))))
    for blk in range(LRU_BLOCKS):
        cs = slice(blk * nb, (blk + 1) * nb)
        z = jnp.dot(ccb[:, cs], wg_ref[blk], preferred_element_type=F32)
        r = jax.nn.sigmoid(z[:, :nb] + ba_ref[:, cs])
        gi = jax.nn.sigmoid(z[:, nb:] + bx_ref[:, cs])
        log_a = neg_c_softplus[:, cs] * r
        a = jnp.exp(log_a)
        abuf[:, cs] = a
        bbuf[:, cs] = jnp.sqrt(1.0 - a * a) * (gi * cc[:, cs])

    row = lax.broadcasted_iota(I32, (SUBLANES, d), 0)
    nchunk = tr // SUBLANES

    def chunk(i, h):
        ci = (nchunk - 1 - i) if back else i
        r0 = pl.multiple_of(ci * SUBLANES, SUBLANES)
        a = abuf[pl.ds(r0, SUBLANES), :]
        b = bbuf[pl.ds(r0, SUBLANES), :]
        for k in (1, 2, 4):
            if back:
                sh, ok = SUBLANES - k, row < SUBLANES - k
            else:
                sh, ok = k, row >= k
            a_s = pltpu.roll(a, sh, 0)
            b_s = pltpu.roll(b, sh, 0)
            b = jnp.where(ok, a * b_s + b, b)
            a = jnp.where(ok, a * a_s, a)
        hh = a * h + b
        hbuf[pl.ds(r0, SUBLANES), :] = hh
        return hh[0:1, :] if back else hh[SUBLANES - 1:SUBLANES, :]

    hcar[0:1, :] = lax.fori_loop(0, nchunk, chunk, hcar[0:1, :])


def _lru_fwd_kernel(u_ref, cw_ref, cb_ref, wg_ref, ba_ref, bx_ref, lam_ref, hf_ref,
                    ubuf, halo, hcar, abuf, bbuf, hbuf):
    _lru_gates_scan(False, pl.program_id(1), u_ref, cw_ref, cb_ref, wg_ref, ba_ref, bx_ref, lam_ref,
                    ubuf, halo, hcar, abuf, bbuf, hbuf)
    hf_ref[0] = hbuf[...]


def _lru_bwd_kernel(u_ref, cw_ref, cb_ref, wg_ref, ba_ref, bx_ref, lam_ref, hf_ref, gg_ref, x_ref, mod_ref,
                    wo_ref, o_ref, ubuf, halo, hcar, abuf, bbuf, hbuf):
    _lru_gates_scan(True, pl.program_id(1), u_ref, cw_ref, cb_ref, wg_ref, ba_ref, bx_ref, lam_ref,
                    ubuf, halo, hcar, abuf, bbuf, hbuf)
    y = (hf_ref[0] + hbuf[...]) * gg_ref[0]
    out = jnp.dot(y.astype(BF16), wo_ref[...], preferred_element_type=F32)
    o_ref[0] = x_ref[0] + mod_ref[0, 0][2:3] * out


def _lru_mixer(x, mods, g, w_in, conv_w, conv_b, wa, ba, wx, bx, lam, w_out):
    b, n, d = x.shape
    nt = n // TR
    nb = d // LRU_BLOCKS
    u, gg = pl.pallas_call(
        _lru_a_kernel, grid=(b, nt),
        in_specs=[_row_spec(TR, d), _seg_mod_spec(d), _full_spec((1, d)), _full_spec((d, 2 * d))],
        out_specs=[_row_spec(TR, d), _row_spec(TR, d)],
        out_shape=[jax.ShapeDtypeStruct((b, n, d), F32), jax.ShapeDtypeStruct((b, n, d), F32)],
        compiler_params=_params(("parallel", "parallel")),
        name="lru_in",
    )(x, mods, g.reshape(1, d), w_in.astype(BF16))

    wg = jnp.concatenate([wa, wx], axis=-1).astype(BF16)
    scratch = [pltpu.VMEM((TR + SUBLANES, d), F32), pltpu.VMEM((SUBLANES, d), F32),
               pltpu.VMEM((SUBLANES, d), F32), pltpu.VMEM((TR, d), F32), pltpu.VMEM((TR, d), F32),
               pltpu.VMEM((TR, d), F32)]

    def gate_specs(order):
        return [pl.BlockSpec((1, TR, d), lambda bb, t: (bb, order(t), 0)),
                _full_spec((LRU_CONV_W, d)), _full_spec((1, d)), _full_spec((LRU_BLOCKS, nb, 2 * nb)),
                _full_spec((1, d)), _full_spec((1, d)), _full_spec((1, d))]

    def gate_args(dd):
        return (u, conv_w[dd], conv_b[dd].reshape(1, d), wg[dd], ba[dd].reshape(1, d),
                bx[dd].reshape(1, d), lam[dd].reshape(1, d))

    hf = pl.pallas_call(
        _lru_fwd_kernel, grid=(b, nt),
        in_specs=gate_specs(lambda t: t),
        out_specs=_row_spec(TR, d),
        out_shape=jax.ShapeDtypeStruct((b, n, d), F32),
        scratch_shapes=scratch,
        compiler_params=_params(("arbitrary", "arbitrary")),
        name="lru_fwd",
    )(*gate_args(0))

    def rev(t):
        return jnp.where(t == 0, 0, nt - t)

    def rev_spec():
        return pl.BlockSpec((1, TR, d), lambda bb, t: (bb, rev(t), 0))

    return pl.pallas_call(
        _lru_bwd_kernel, grid=(b, nt),
        in_specs=gate_specs(rev) + [rev_spec(), rev_spec(), rev_spec(),
                                    pl.BlockSpec((1, 1, N_MOD, d), lambda bb, t: (bb, jnp.minimum(t, 1), 0, 0)),
                                    _full_spec((d, d))],
        out_specs=rev_spec(),
        out_shape=jax.ShapeDtypeStruct((b, n, d), F32),
        scratch_shapes=scratch,
        compiler_params=_params(("arbitrary", "arbitrary")),
        name="lru_bwd",
    )(*gate_args(1), hf, gg, x, mods, w_out.astype(BF16))


def _route_kernel(x_ref, mod_ref, g_ref, wr_ref, rb_ref, h_ref, e0_ref, e1_ref, p0_ref, p1_ref, gt_ref, cnt_ref,
                  carry):
    first = jnp.logical_and(pl.program_id(0) == 0, pl.program_id(1) == 0)
    tr = x_ref.shape[1]

    @pl.when(first)
    def _():
        carry[...] = jnp.zeros_like(carry)

    m = mod_ref[0, 0]
    h = _norm_mod(x_ref[0], g_ref[...], m[3:4], m[4:5])
    h_ref[0] = h
    logit = lax.dot_general(wr_ref[...], h.astype(BF16), (((1,), (1,)), ((), ())),
                            preferred_element_type=F32)
    s = jax.nn.sigmoid(logit)
    sel = s + rb_ref[...]
    sj = [s[SUBLANES * j:SUBLANES * (j + 1)] for j in range(EXPERTS_PER_GROUP)]
    cj = [sel[SUBLANES * j:SUBLANES * (j + 1)] for j in range(EXPERTS_PER_GROUP)]

    hi1, lo1 = jnp.maximum(cj[0], cj[1]), jnp.minimum(cj[0], cj[1])
    hi2, lo2 = jnp.maximum(cj[2], cj[3]), jnp.minimum(cj[2], cj[3])
    top1 = jnp.maximum(hi1, hi2)
    top2 = jnp.maximum(jnp.minimum(hi1, hi2), jnp.maximum(lo1, lo2))
    row = lax.broadcasted_iota(I32, (SUBLANES, tr), 0)
    gscore = jnp.where(row < N_GROUPS, top1 + top2, -jnp.inf)
    gmax = jnp.max(gscore, axis=0, keepdims=True)
    grp = jnp.min(jnp.where(gscore == gmax, row, SUBLANES), axis=0, keepdims=True)
    pick = row == grp
    c = [jnp.sum(jnp.where(pick, v, 0.0), axis=0, keepdims=True) for v in cj]
    w = [jnp.sum(jnp.where(pick, v, 0.0), axis=0, keepdims=True) for v in sj]

    def argtop(vals):
        best, bi, bw = vals[0], jnp.zeros_like(grp), w[0]
        for j in range(1, EXPERTS_PER_GROUP):
            better = vals[j] > best
            best = jnp.where(better, vals[j], best)
            bi = jnp.where(better, j, bi)
            bw = jnp.where(better, w[j], bw)
        return bi, bw

    i0, w0 = argtop(c)
    i1, w1 = argtop([jnp.where(i0 == j, -jnp.inf, c[j]) for j in range(EXPERTS_PER_GROUP)])
    e0 = grp * EXPERTS_PER_GROUP + i0
    e1 = grp * EXPERTS_PER_GROUP + i1
    wsum = w0 + w1
    g0 = w0 / wsum
    g1 = w1 / wsum

    erow = lax.broadcasted_iota(I32, (N_EXPERTS, tr), 0)
    member = jnp.logical_or(erow == e0, erow == e1)
    upper = (lax.broadcasted_iota(I32, (tr, tr), 0) < lax.broadcasted_iota(I32, (tr, tr), 1))
    before = jnp.dot(member.astype(BF16), upper.astype(BF16), preferred_element_type=F32) + carry[...]
    p0 = jnp.sum(jnp.where(erow == e0, before, 0.0), axis=0, keepdims=True)
    p1 = jnp.sum(jnp.where(erow == e1, before, 0.0), axis=0, keepdims=True)
    total = carry[...] + jnp.sum(member.astype(F32), axis=1, keepdims=True)
    carry[...] = total
    cnt_ref[...] = total.astype(I32)

    e0_ref[0] = e0
    e1_ref[0] = e1
    p0_ref[0] = p0.astype(I32)
    p1_ref[0] = p1.astype(I32)
    grow = lax.broadcasted_iota(I32, (LANES, tr), 0)
    gmat = jnp.where(grow == 0, g0, jnp.where(grow == 1, g1, 0.0))
    gt_ref[0] = gmat.T


def _dispatch_kernel(slot_ref, zrow_ref, nu_ref, h_ref, xs_ref, ring, zbuf, sem, zsem, *, n_tok, n_blocks):
    s = pl.program_id(0)
    ns = pl.num_programs(0)
    tr = h_ref.shape[0]
    blk = zbuf.shape[0]
    par = s % 2

    @pl.when(s == 0)
    def _():
        zbuf[...] = jnp.zeros_like(zbuf)
        zsrc = zbuf.at[pl.ds(0, 1)]
        for e in range(N_EXPERTS):
            first, npad = zrow_ref[e], zrow_ref[N_EXPERTS + e]

            def zero_row(i, carry, first=first):
                pltpu.make_async_copy(zsrc, xs_ref.at[pl.ds(first + i, 1)], zsem).start()
                return carry

            def zero_row_wait(i, carry):
                pltpu.make_async_copy(zsrc, xs_ref.at[pl.ds(0, 1)], zsem).wait()
                return carry

            lax.fori_loop(0, npad, zero_row, 0)
            lax.fori_loop(0, npad, zero_row_wait, 0)

        def zero_blk(i, carry):
            pltpu.make_async_copy(zbuf, xs_ref.at[pl.ds(pl.multiple_of(i * blk, blk), blk)], zsem).start()
            return carry

        def zero_blk_wait(i, carry):
            pltpu.make_async_copy(zbuf, xs_ref.at[pl.ds(0, blk)], zsem).wait()
            return carry

        lax.fori_loop(nu_ref[0], n_blocks, zero_blk, 0)
        lax.fori_loop(nu_ref[0], n_blocks, zero_blk_wait, 0)

    def wait_ring(p):
        for _ in range(TOP_K):
            pltpu.make_async_copy(ring.at[p], xs_ref.at[pl.ds(0, tr)], sem.at[p]).wait()

    @pl.when(s >= 2)
    def _():
        wait_ring(par)

    ring[par] = h_ref[...]
    base = s * tr

    def issue(i, carry):
        for k in range(TOP_K):
            pltpu.make_async_copy(ring.at[par, pl.ds(i, 1)], xs_ref.at[pl.ds(slot_ref[k * n_tok + base + i], 1)],
                                  sem.at[par]).start()
        return carry

    lax.fori_loop(0, tr, issue, 0, unroll=8)

    @pl.when(s == ns - 1)
    def _():
        wait_ring(par)

        @pl.when(ns >= 2)
        def _():
            wait_ring(1 - par)


def _expert_kernel(be_ref, nu_ref, xs_ref, w1_ref, w3_ref, w2_ref, ys_ref, w1b, w3b, w2b):
    i = pl.program_id(0)
    used = i < nu_ref[0]
    new_expert = jnp.logical_or(i == 0, be_ref[i] != be_ref[jnp.maximum(i - 1, 0)])

    @pl.when(jnp.logical_and(used, new_expert))
    def _():
        w1b[...] = w1_ref[0, 0].astype(BF16)
        w3b[...] = w3_ref[0, 0].astype(BF16)
        w2b[...] = w2_ref[0, 0].astype(BF16)

    @pl.when(used)
    def _():
        xb = xs_ref[...].astype(BF16)
        a = jnp.dot(xb, w1b[...], preferred_element_type=F32)
        b = jnp.dot(xb, w3b[...], preferred_element_type=F32)
        hid = (a * jax.nn.sigmoid(a)) * b
        ys_ref[...] = jnp.dot(hid.astype(BF16), w2b[...], preferred_element_type=F32)

    @pl.when(jnp.logical_not(used))
    def _():
        ys_ref[...] = jnp.zeros_like(ys_ref)


def _combine_kernel(slot_ref, x_ref, mod_ref, gt_ref, nf_ref, ys_ref, o_ref, ybuf, sem,
                    *, n_tok, rows_per_batch, tile_off, final):
    bb, t = pl.program_id(0), pl.program_id(1)
    nt = pl.num_programs(1)
    tr = x_ref.shape[1]
    step = bb * nt + t
    nsteps = pl.num_programs(0) * nt

    def tok_base(s):
        return (s // nt) * rows_per_batch + (s % nt + tile_off) * tr

    def issue(s):
        par = s % 2
        base = tok_base(s)

        def one(i, carry):
            for k in range(TOP_K):
                pltpu.make_async_copy(ys_ref.at[pl.ds(slot_ref[k * n_tok + base + i], 1)],
                                      ybuf.at[par, k, pl.ds(i, 1)], sem.at[par]).start()
            return carry

        lax.fori_loop(0, tr, one, 0, unroll=8)

    @pl.when(step == 0)
    def _():
        issue(step)

    @pl.when(step + 1 < nsteps)
    def _():
        issue(step + 1)

    par = step % 2
    for k in range(TOP_K):
        pltpu.make_async_copy(ys_ref.at[pl.ds(0, tr)], ybuf.at[par, k], sem.at[par]).wait()

    gt = gt_ref[0]
    y = gt[:, 0:1] * ybuf[par, 0] + gt[:, 1:2] * ybuf[par, 1]
    out = x_ref[0] + mod_ref[0, 0][5:6] * y
    if final:
        out = out * lax.rsqrt(jnp.mean(out * out, axis=-1, keepdims=True) + EPS) * nf_ref[...]
    o_ref[0] = out


def _moe(x, mods, g, wr_perm, rb_perm, layer, w1, w3, w2, norm_f, final, ctx_len):
    b, n, d = x.shape
    nt = n // TR
    n_tok = b * n
    de = w1.shape[-1]

    def lane_major():
        return pl.BlockSpec((1, 1, TR), lambda bb, t: (bb * nt + t, 0, 0))

    lane_major_shape = jax.ShapeDtypeStruct((b * nt, 1, TR), I32)
    h, e0, e1, p0, p1, gtab, counts = pl.pallas_call(
        _route_kernel, grid=(b, nt),
        in_specs=[_row_spec(TR, d), _seg_mod_spec(d), _full_spec((1, d)),
                  _full_spec((SUBLANES * EXPERTS_PER_GROUP, d)), _full_spec((SUBLANES * EXPERTS_PER_GROUP, 1))],
        out_specs=[_row_spec(TR, d), lane_major(), lane_major(), lane_major(), lane_major(),
                   _row_spec(TR, LANES),
                   pl.BlockSpec((N_EXPERTS, 1), lambda bb, t: (0, 0))],
        out_shape=[jax.ShapeDtypeStruct((b, n, d), F32), lane_major_shape, lane_major_shape, lane_major_shape,
                   lane_major_shape,
                   jax.ShapeDtypeStruct((b, n, LANES), F32), jax.ShapeDtypeStruct((N_EXPERTS, 1), I32)],
        scratch_shapes=[pltpu.VMEM((N_EXPERTS, 1), F32)],
        compiler_params=_params(("arbitrary", "arbitrary")),
        name="moe_route",
    )(x, mods, g.reshape(1, d), wr_perm, rb_perm)

    counts = counts[:, 0]
    padded = (counts + MOE_BLK - 1) // MOE_BLK * MOE_BLK
    pend = jnp.cumsum(padded)
    pstart = (pend - padded).astype(I32)
    n_slots = n_tok * TOP_K + N_EXPERTS * MOE_BLK
    n_blocks = n_slots // MOE_BLK
    n_used = (pend[-1] // MOE_BLK).astype(I32).reshape(1)
    blk_start = jnp.arange(n_blocks, dtype=I32) * MOE_BLK
    blk_e = jnp.minimum(jnp.sum((pend[None, :] <= blk_start[:, None]).astype(I32), axis=1), N_EXPERTS - 1)

    def slot_of(e, p):
        start = jnp.zeros_like(e)
        for j in range(N_EXPERTS):
            start = jnp.where(e == j, pstart[j], start)
        return (start + p).reshape(n_tok)

    slots = jnp.concatenate([slot_of(e0, p0), slot_of(e1, p1)])
    zrow = jnp.concatenate([pstart + counts, padded - counts]).astype(I32)

    xs = pl.pallas_call(
        functools.partial(_dispatch_kernel, n_tok=n_tok, n_blocks=n_blocks),
        grid_spec=pltpu.PrefetchScalarGridSpec(
            num_scalar_prefetch=3, grid=(n_tok // TR,),
            in_specs=[pl.BlockSpec((TR, d), lambda s, sl, zr, nu: (s, 0))],
            out_specs=pl.BlockSpec(memory_space=pl.ANY),
            scratch_shapes=[pltpu.VMEM((2, TR, d), F32), pltpu.VMEM((MOE_BLK, d), F32),
                            pltpu.SemaphoreType.DMA((2,)), pltpu.SemaphoreType.DMA]),
        out_shape=jax.ShapeDtypeStruct((n_slots, d), F32),
        compiler_params=_params(("arbitrary",)),
        name="moe_dispatch",
    )(slots, zrow, n_used, h.reshape(n_tok, d))

    def wspec(shape):
        return pl.BlockSpec((1, 1) + shape, lambda i, be, nu: (layer, be[i], 0, 0))

    def blk_spec():
        return pl.BlockSpec((MOE_BLK, d), lambda i, be, nu: (i, 0))

    ys = pl.pallas_call(
        _expert_kernel,
        grid_spec=pltpu.PrefetchScalarGridSpec(
            num_scalar_prefetch=2, grid=(n_blocks,),
            in_specs=[blk_spec(), wspec((d, de)), wspec((d, de)), wspec((de, d))],
            out_specs=blk_spec(),
            scratch_shapes=[pltpu.VMEM((d, de), BF16), pltpu.VMEM((d, de), BF16), pltpu.VMEM((de, d), BF16)]),
        out_shape=jax.ShapeDtypeStruct((n_slots, d), F32),
        compiler_params=_params(("arbitrary",)),
        name="moe_experts",
    )(blk_e, n_used, xs, w1, w3, w2)

    tile_off = ctx_len // TR if final else 0
    nt_out = nt - tile_off
    out_rows = nt_out * TR

    def rows(width):
        return pl.BlockSpec((1, TR, width), lambda bb, t, s: (bb, t + tile_off, 0))

    return pl.pallas_call(
        functools.partial(_combine_kernel, n_tok=n_tok, rows_per_batch=n, tile_off=tile_off, final=final),
        grid_spec=pltpu.PrefetchScalarGridSpec(
            num_scalar_prefetch=1, grid=(b, nt_out),
            in_specs=[rows(d),
                      pl.BlockSpec((1, 1, N_MOD, d), lambda bb, t, s: (bb, jnp.minimum(t + tile_off, 1), 0, 0)),
                      rows(LANES), pl.BlockSpec((1, d), lambda bb, t, s: (0, 0)),
                      pl.BlockSpec(memory_space=pl.ANY)],
            out_specs=pl.BlockSpec((1, TR, d), lambda bb, t, s: (bb, t, 0)),
            scratch_shapes=[pltpu.VMEM((2, TOP_K, TR, d), F32), pltpu.SemaphoreType.DMA((2,))]),
        out_shape=jax.ShapeDtypeStruct((b, out_rows, d), F32),
        compiler_params=_params(("arbitrary", "arbitrary")),
        name="moe_combine",
    )(slots, x, mods, gtab, norm_f.reshape(1, d), ys)


def kernel(x, c, ctx, c_ctx, w_mod, b_mod, norm_mix, norm_ffn, norm_f, conv_pw1, conv_pw1_b, conv_dw, conv_dw_b, conv_ln_g, conv_ln_b, conv_pw2, conv_pw2_b, attn_w_qkv, attn_w_o, attn_sink, lru_w_in, lru_conv_w, lru_conv_b, lru_wa, lru_ba, lru_wx, lru_bx, lru_lam, lru_w_out, moe_w_router, moe_router_bias, moe_w1, moe_w3, moe_w2):
    b, s, d = x.shape
    ctx_len = ctx.shape[1]
    depth = w_mod.shape[0]
    assert ctx_len == TR and s % TR == 0 and d % LANES == 0
    assert moe_w_router.shape[1] == N_EXPERTS

    mods = _modulation(c, c_ctx, w_mod, b_mod)
    xs = jnp.concatenate([ctx, x], axis=1)

    wr = jnp.zeros((EXPERTS_PER_GROUP, SUBLANES, d), F32).at[:, :N_GROUPS].set(
        moe_w_router.T.reshape(N_GROUPS, EXPERTS_PER_GROUP, d).swapaxes(0, 1))
    wr = wr.reshape(EXPERTS_PER_GROUP * SUBLANES, d).astype(BF16)
    rb = jnp.zeros((EXPERTS_PER_GROUP, SUBLANES), F32).at[:, :N_GROUPS].set(
        moe_router_bias.astype(F32).reshape(N_GROUPS, EXPERTS_PER_GROUP).T).reshape(-1, 1)

    for i in range(depth):
        kind, slot = i % N_MIXERS, i // N_MIXERS
        m = mods[i]
        if kind == 0:
            xs = _conv_mixer(xs, m, norm_mix[i], conv_pw1[slot], conv_pw1_b[slot], conv_dw[slot], conv_dw_b[slot],
                             conv_ln_g[slot], conv_ln_b[slot], conv_pw2[slot], conv_pw2_b[slot])
        elif kind == 1:
            xs = _attn_mixer(xs, m, norm_mix[i], attn_w_qkv[slot], attn_w_o[slot], attn_sink[slot], ctx_len)
        else:
            xs = _lru_mixer(xs, m, norm_mix[i], lru_w_in[slot], lru_conv_w[slot], lru_conv_b[slot], lru_wa[slot],
                            lru_ba[slot], lru_wx[slot], lru_bx[slot], lru_lam[slot], lru_w_out[slot])
        xs = _moe(xs, m, norm_ffn[i], wr, rb, i, moe_w1, moe_w3, moe_w2, norm_f, i == depth - 1, ctx_len)
    return xs
```

```python
import functools

import jax
import jax.numpy as jnp
from jax import lax
from jax.experimental import pallas as pl
from jax.experimental.pallas import tpu as pltpu

F32 = jnp.float32
BF16 = jnp.bfloat16
I32 = jnp.int32

EPS = 1e-6
N_MOD = 6
N_MIXERS = 3
GRID_W = 64
CONV_WIDTH = 31
CONV_HALF = (CONV_WIDTH - 1) // 2
HEAD_DIM = 64
N_KV_HEADS = 4
GQA_GROUP = 4
WINDOW = 128
ROPE_BASE = 10000.0
ROPE_PAIRS = HEAD_DIM // 4
LRU_BLOCKS = 8
LRU_CONV_W = 4
LRU_C = 8.0
N_EXPERTS = 16
N_GROUPS = 4
EXPERTS_PER_GROUP = 4
TOP_K = 2

LANES = 128
SUBLANES = 8
TR = 256
TQ = 128
HALO = 16
MOE_BLK = 256
NEG = -1e30
VMEM_LIMIT = 56 * 1024 * 1024


def _params(sem, vmem=VMEM_LIMIT):
    return pltpu.CompilerParams(dimension_semantics=sem, vmem_limit_bytes=vmem)


def _norm_mod(x, g, shift, scale):
    y = x * lax.rsqrt(jnp.mean(x * x, axis=-1, keepdims=True) + EPS)
    return (y * g) * (1.0 + scale) + shift


def _seg_mod_spec(d):
    return pl.BlockSpec((1, 1, N_MOD, d), lambda b, t: (b, jnp.minimum(t, 1), 0, 0))


def _row_spec(tr, d):
    return pl.BlockSpec((1, tr, d), lambda b, t: (b, t, 0))


def _full_spec(shape):
    nd = len(shape)
    return pl.BlockSpec(shape, lambda b, t: (0,) * nd)


def _mod_kernel(c_ref, w_ref, b_ref, o_ref):
    c = c_ref[...]
    sc = c * jax.nn.sigmoid(c)
    o_ref[0] = jnp.dot(sc.astype(BF16), w_ref[0].astype(BF16), preferred_element_type=F32) + b_ref[0]


def _modulation(c, c_ctx, w_mod, b_mod):
    depth, d, nout = w_mod.shape
    b = c.shape[0]
    assert b + 1 <= SUBLANES
    rows = jnp.zeros((SUBLANES, d), F32).at[:b].set(c).at[b].set(c_ctx)
    tn = 1536
    raw = pl.pallas_call(
        _mod_kernel,
        grid=(depth, nout // tn),
        in_specs=[pl.BlockSpec((SUBLANES, d), lambda i, j: (0, 0)),
                  pl.BlockSpec((1, d, tn), lambda i, j: (i, 0, j)),
                  pl.BlockSpec((1, 1, tn), lambda i, j: (i, 0, j))],
        out_specs=pl.BlockSpec((1, SUBLANES, tn), lambda i, j: (i, 0, j)),
        out_shape=jax.ShapeDtypeStruct((depth, SUBLANES, nout), F32),
        compiler_params=_params(("arbitrary", "arbitrary")),
        name="modulation",
    )(rows, w_mod, b_mod.reshape(depth, 1, nout))
    raw = raw.reshape(depth, SUBLANES, N_MOD, d)
    lat = raw[:, :b]
    ctx = jnp.broadcast_to(raw[:, b][:, None], lat.shape)
    return jnp.stack([ctx, lat], axis=2)


def _conv_a_kernel(x_ref, mod_ref, g_ref, w_ref, b_ref, o_ref):
    d = x_ref.shape[-1]
    m = mod_ref[0, 0]
    h = _norm_mod(x_ref[0], g_ref[...], m[0:1], m[1:2])
    u = jnp.dot(h.astype(BF16), w_ref[...], preferred_element_type=F32) + b_ref[...]
    o_ref[0] = u[:, :d] * jax.nn.sigmoid(u[:, d:])


def _conv_b_kernel(x_ref, mod_ref, gp_ref, gc_ref, gn_ref, dw_ref, dwb_ref, lng_ref, lnb_ref,
                   w_ref, b_ref, o_ref, gbuf, cbuf):
    t = pl.program_id(1)
    nt = pl.num_programs(1)
    tr, d = gc_ref.shape[1], gc_ref.shape[2]
    seg_first = t <= 1
    seg_last = jnp.logical_or(t == 0, t == nt - 1)
    gbuf[0, 0:HALO, :] = jnp.where(seg_first, 0.0, gp_ref[0])
    gbuf[0, HALO:HALO + tr, :] = gc_ref[0]
    gbuf[0, HALO + tr:HALO + tr + HALO, :] = jnp.where(seg_last, 0.0, gn_ref[0])
    span = tr + 2 * HALO - SUBLANES
    for r in range(1, SUBLANES):
        gbuf[r, 0:span, :] = gbuf[0, r:r + span, :]

    rc = 32
    base = HALO - CONV_HALF

    def chunk(i, carry):
        r0 = pl.multiple_of(i * rc, rc)
        for c in range(d // LANES):
            cs = slice(c * LANES, (c + 1) * LANES)
            acc = jnp.zeros((rc, LANES), F32)
            for k in range(CONV_WIDTH):
                off = base + k
                rows = pl.ds(r0 + (off // SUBLANES) * SUBLANES, rc)
                acc = acc + dw_ref[k:k + 1, cs] * gbuf[off % SUBLANES, rows, cs]
            cbuf[pl.ds(r0, rc), cs] = acc
        return carry

    lax.fori_loop(0, tr // rc, chunk, 0)

    u = cbuf[...] + dwb_ref[...]
    mu = jnp.mean(u, axis=-1, keepdims=True)
    uc = u - mu
    var = jnp.mean(uc * uc, axis=-1, keepdims=True)
    v = uc * lax.rsqrt(var + EPS) * lng_ref[...] + lnb_ref[...]
    v = v * jax.nn.sigmoid(v)
    y = jnp.dot(v.astype(BF16), w_ref[...], preferred_element_type=F32) + b_ref[...]
    o_ref[0] = x_ref[0] + mod_ref[0, 0][2:3] * y


def _conv_mixer(x, mods, g, pw1, pw1_b, dw, dw_b, ln_g, ln_b, pw2, pw2_b):
    b, n, d = x.shape
    nt = n // TR
    grid = (b, nt)
    glu = pl.pallas_call(
        _conv_a_kernel, grid=grid,
        in_specs=[_row_spec(TR, d), _seg_mod_spec(d), _full_spec((1, d)),
                  _full_spec((d, 2 * d)), _full_spec((1, 2 * d))],
        out_specs=_row_spec(TR, d),
        out_shape=jax.ShapeDtypeStruct((b, n, d), F32),
        compiler_params=_params(("parallel", "parallel")),
        name="conv_glu",
    )(x, mods, g.reshape(1, d), pw1.astype(BF16), pw1_b.reshape(1, 2 * d))

    hb = TR // HALO
    nh = n // HALO
    dwp = jnp.zeros((32, d), F32).at[:CONV_WIDTH].set(dw)
    return pl.pallas_call(
        _conv_b_kernel, grid=grid,
        in_specs=[_row_spec(TR, d), _seg_mod_spec(d),
                  pl.BlockSpec((1, HALO, d), lambda bb, t: (bb, jnp.maximum(t * hb - 1, 0), 0)),
                  _row_spec(TR, d),
                  pl.BlockSpec((1, HALO, d), lambda bb, t: (bb, jnp.minimum((t + 1) * hb, nh - 1), 0)),
                  _full_spec((32, d)), _full_spec((1, d)), _full_spec((1, d)), _full_spec((1, d)),
                  _full_spec((d, d)), _full_spec((1, d))],
        out_specs=_row_spec(TR, d),
        out_shape=jax.ShapeDtypeStruct((b, n, d), F32),
        scratch_shapes=[pltpu.VMEM((SUBLANES, TR + 2 * HALO, d), F32), pltpu.VMEM((TR, d), F32)],
        compiler_params=_params(("parallel", "parallel")),
        name="conv_dw_pw2",
    )(x, mods, glu, glu, glu, dwp, dw_b.reshape(1, d), ln_g.reshape(1, d), ln_b.reshape(1, d),
      pw2.astype(BF16), pw2_b.reshape(1, d))


def _rope_tables(n, ctx_len):
    pos = jnp.arange(n - ctx_len, dtype=I32)
    inv = ROPE_BASE ** (-jnp.arange(ROPE_PAIRS, dtype=F32) / ROPE_PAIRS)
    ar = (pos // GRID_W).astype(F32)[:, None] * inv
    ac = (pos % GRID_W).astype(F32)[:, None] * inv
    ang = jnp.concatenate([ar, ar, ac, ac], axis=-1)
    sign = jnp.tile(jnp.concatenate([-jnp.ones((ROPE_PAIRS,), F32), jnp.ones((ROPE_PAIRS,), F32)]), 2)
    cos = jnp.concatenate([jnp.ones((ctx_len, HEAD_DIM), F32), jnp.cos(ang)], axis=0)
    sin = jnp.concatenate([jnp.zeros((ctx_len, HEAD_DIM), F32), jnp.sin(ang) * sign], axis=0)
    return jnp.tile(cos, (1, 2)), jnp.tile(sin, (1, 2))


def _attn_a_kernel(x_ref, mod_ref, g_ref, w_ref, cos_ref, sin_ref, q_ref, k_ref, v_ref):
    d = x_ref.shape[-1]
    nk = k_ref.shape[-1]
    m = mod_ref[0, 0]
    h = _norm_mod(x_ref[0], g_ref[...], m[0:1], m[1:2])
    t = jnp.dot(h.astype(BF16), w_ref[...], preferred_element_type=F32)
    cos = cos_ref[...]
    sin = sin_ref[...]
    lane = lax.broadcasted_iota(I32, cos.shape, 1)
    first_half = (lane % (2 * ROPE_PAIRS)) < ROPE_PAIRS

    def rope(xg):
        partner = jnp.where(first_half, pltpu.roll(xg, LANES - ROPE_PAIRS, 1), pltpu.roll(xg, ROPE_PAIRS, 1))
        return xg * cos + partner * sin

    for j in range(d // LANES):
        cs = slice(j * LANES, (j + 1) * LANES)
        q_ref[0, :, cs] = rope(t[:, cs]).astype(BF16)
    for j in range(nk // LANES):
        cs = slice(j * LANES, (j + 1) * LANES)
        k_ref[0, :, cs] = rope(t[:, d + j * LANES:d + (j + 1) * LANES]).astype(BF16)
    v_ref[0] = t[:, d + nk:].astype(BF16)


def _attn_b_kernel(sink_ref, x_ref, mod_ref, q_ref, kc_ref, vc_ref, k0_ref, k1_ref, k2_ref,
                   v0_ref, v1_ref, v2_ref, wo_ref, o_ref, *, ctx_len, n_rows):
    t = pl.program_id(1)
    tq = q_ref.shape[1]
    nctx = kc_ref.shape[1]
    nkeys = nctx + 3 * tq

    rq = t * tq + lax.broadcasted_iota(I32, (tq, nkeys), 0)
    col = lax.broadcasted_iota(I32, (tq, nkeys), 1)
    rk = (t - 1) * tq + (col - nctx)
    win_ok = (rk >= ctx_len) & (rk < n_rows) & (jnp.abs(rq - rk) <= WINDOW) & (t * tq >= ctx_len)
    bias = jnp.where((col < nctx) | win_ok, 0.0, NEG).astype(F32)

    lane = lax.broadcasted_iota(I32, (tq, LANES), 1)
    lo = lane < HEAD_DIM

    out_cols = []
    for g in range(N_KV_HEADS):
        gs = slice(g * LANES, (g + 1) * LANES)
        kg = jnp.concatenate([kc_ref[0, :, gs], k0_ref[0, :, gs], k1_ref[0, :, gs], k2_ref[0, :, gs]], axis=0)
        vg = jnp.concatenate([vc_ref[0, :, gs], v0_ref[0, :, gs], v1_ref[0, :, gs], v2_ref[0, :, gs]], axis=0)
        qa = q_ref[0, :, (2 * g) * LANES:(2 * g + 1) * LANES]
        qb = q_ref[0, :, (2 * g + 1) * LANES:(2 * g + 2) * LANES]
        zero = jnp.zeros_like(qa)
        q4 = jnp.concatenate([jnp.where(lo, qa, zero), jnp.where(lo, zero, qa),
                              jnp.where(lo, qb, zero), jnp.where(lo, zero, qb)], axis=0)
        s = lax.dot_general(q4, kg, (((1,), (1,)), ((), ())), preferred_element_type=F32)
        ps, dens = [], []
        for hh in range(GQA_GROUP):
            sk = sink_ref[g * GQA_GROUP + hh]
            sh = s[hh * tq:(hh + 1) * tq] + bias
            mx = jnp.maximum(jnp.max(sh, axis=-1, keepdims=True), sk)
            p = jnp.exp(sh - mx)
            dens.append(jnp.sum(p, axis=-1, keepdims=True) + jnp.exp(sk - mx))
            ps.append(p.astype(BF16))
        pv = jnp.dot(jnp.concatenate(ps, axis=0), vg, preferred_element_type=F32)
        oh = [pv[hh * tq:(hh + 1) * tq] / dens[hh] for hh in range(GQA_GROUP)]
        out_cols.append(jnp.where(lo, oh[0], oh[1]))
        out_cols.append(jnp.where(lo, oh[2], oh[3]))
    o = jnp.concatenate(out_cols, axis=1).astype(BF16)
    y = jnp.dot(o, wo_ref[...], preferred_element_type=F32)
    o_ref[0] = x_ref[0] + mod_ref[0, 0][2:3] * y


def _attn_mixer(x, mods, g, w_qkv, w_o, sink, ctx_len):
    b, n, d = x.shape
    kvd = N_KV_HEADS * HEAD_DIM
    assert ctx_len == TR and d == 2 * N_KV_HEADS * LANES
    scale = HEAD_DIM ** -0.5

    def dup(w):
        w = w.reshape(d, N_KV_HEADS, HEAD_DIM)
        return jnp.concatenate([w, w], axis=-1).reshape(d, N_KV_HEADS * LANES)

    nk = N_KV_HEADS * LANES
    w_all = jnp.concatenate([w_qkv[:, :d] * scale, dup(w_qkv[:, d:d + kvd]), dup(w_qkv[:, d + kvd:])],
                            axis=1).astype(BF16)
    cos, sin = _rope_tables(n, ctx_len)
    q, k, v = pl.pallas_call(
        _attn_a_kernel, grid=(b, n // TR),
        in_specs=[_row_spec(TR, d), _seg_mod_spec(d), _full_spec((1, d)), _full_spec((d, d + 2 * nk)),
                  pl.BlockSpec((TR, LANES), lambda bb, t: (t, 0)),
                  pl.BlockSpec((TR, LANES), lambda bb, t: (t, 0))],
        out_specs=[_row_spec(TR, d), _row_spec(TR, nk), _row_spec(TR, nk)],
        out_shape=[jax.ShapeDtypeStruct((b, n, d), BF16), jax.ShapeDtypeStruct((b, n, nk), BF16),
                   jax.ShapeDtypeStruct((b, n, nk), BF16)],
        compiler_params=_params(("parallel", "parallel")),
        name="attn_qkv",
    )(x, mods, g.reshape(1, d), w_all, cos, sin)

    ntq = n // TQ
    per_seg = TR // TQ

    def win(off):
        return pl.BlockSpec((1, TQ, nk), lambda bb, t, s: (bb, jnp.clip(t + off, 0, ntq - 1), 0))

    ctx_spec = pl.BlockSpec((1, ctx_len, nk), lambda bb, t, s: (bb, 0, 0))
    grid_spec = pltpu.PrefetchScalarGridSpec(
        num_scalar_prefetch=1, grid=(b, ntq),
        in_specs=[pl.BlockSpec((1, TQ, d), lambda bb, t, s: (bb, t, 0)),
                  pl.BlockSpec((1, 1, N_MOD, d), lambda bb, t, s: (bb, jnp.minimum(t // per_seg, 1), 0, 0)),
                  pl.BlockSpec((1, TQ, d), lambda bb, t, s: (bb, t, 0)),
                  ctx_spec, ctx_spec, win(-1), win(0), win(1), win(-1), win(0), win(1),
                  pl.BlockSpec((d, d), lambda bb, t, s: (0, 0))],
        out_specs=pl.BlockSpec((1, TQ, d), lambda bb, t, s: (bb, t, 0)))
    return pl.pallas_call(
        functools.partial(_attn_b_kernel, ctx_len=ctx_len, n_rows=n),
        grid_spec=grid_spec,
        out_shape=jax.ShapeDtypeStruct((b, n, d), F32),
        compiler_params=_params(("parallel", "parallel")),
        name="attn_core",
    )(sink.astype(F32), x, mods, q, k, v, k, k, k, v, v, v, w_o.astype(BF16))


def _lru_a_kernel(x_ref, mod_ref, g_ref, w_ref, u_ref, gg_ref):
    d = x_ref.shape[-1]
    m = mod_ref[0, 0]
    h = _norm_mod(x_ref[0], g_ref[...], m[0:1], m[1:2])
    t = jnp.dot(h.astype(BF16), w_ref[...], preferred_element_type=F32)
    u_ref[0] = t[:, :d]
    gg_ref[0] = jax.nn.gelu(t[:, d:])


def _lru_gates_scan(back, t, u_ref, cw_ref, cb_ref, wg_ref, ba_ref, bx_ref, lam_ref,
                    ubuf, halo, hcar, abuf, bbuf, hbuf):
    tr, d = u_ref.shape[1], u_ref.shape[2]
    nb = d // LRU_BLOCKS

    @pl.when(t <= 1)
    def _():
        halo[...] = jnp.zeros_like(halo)

    @pl.when(t == 0)
    def _():
        hcar[...] = jnp.zeros_like(hcar)

    u = u_ref[0]
    if not back:
        ubuf[0:SUBLANES, :] = halo[...]
        ubuf[SUBLANES:SUBLANES + tr, :] = u
        halo[...] = u[tr - SUBLANES:, :]
        taps = [ubuf[SUBLANES - (LRU_CONV_W - 1) + k:SUBLANES - (LRU_CONV_W - 1) + k + tr, :]
                for k in range(LRU_CONV_W)]
    else:
        ubuf[0:tr, :] = u
        ubuf[tr:tr + SUBLANES, :] = halo[...]
        halo[...] = u[:SUBLANES, :]
        taps = [ubuf[(LRU_CONV_W - 1) - k:(LRU_CONV_W - 1) - k + tr, :] for k in range(LRU_CONV_W)]
    cc = cb_ref[...] + taps[0] * cw_ref[0:1, :]
    for k in range(1, LRU_CONV_W):
        cc = cc + taps[k] * cw_ref[k:k + 1, :]

    ccb = cc.astype(BF16)
    lam = lam_ref[...]
    neg_c_softplus = -LRU_C * (jnp.maximum(-lam, 0.0) + jnp.log(1.0 + jnp.exp(-jnp.abs(lam))))
    for blk in range(LRU_BLOCKS):
        cs = slice(blk * nb, (blk + 1) * nb)
        z = jnp.dot(ccb[:, cs], wg_ref[blk], preferred_element_type=F32)
        r = jax.nn.sigmoid(z[:, :nb] + ba_ref[:, cs])
        gi = jax.nn.sigmoid(z[:, nb:] + bx_ref[:, cs])
        log_a = neg_c_softplus[:, cs] * r
        a = jnp.exp(log_a)
        abuf[:, cs] = a
        bbuf[:, cs] = jnp.sqrt(1.0 - a * a) * (gi * cc[:, cs])

    row = lax.broadcasted_iota(I32, (SUBLANES, d), 0)
    nchunk = tr // SUBLANES

    def chunk(i, h):
        ci = (nchunk - 1 - i) if back else i
        r0 = pl.multiple_of(ci * SUBLANES, SUBLANES)
        a = abuf[pl.ds(r0, SUBLANES), :]
        b = bbuf[pl.ds(r0, SUBLANES), :]
        for k in (1, 2, 4):
            if back:
                sh, ok = SUBLANES - k, row < SUBLANES - k
            else:
                sh, ok = k, row >= k
            a_s = pltpu.roll(a, sh, 0)
            b_s = pltpu.roll(b, sh, 0)
            b = jnp.where(ok, a * b_s + b, b)
            a = jnp.where(ok, a * a_s, a)
        hh = a * h + b
        hbuf[pl.ds(r0, SUBLANES), :] = hh
        return hh[0:1, :] if back else hh[SUBLANES - 1:SUBLANES, :]

    hcar[0:1, :] = lax.fori_loop(0, nchunk, chunk, hcar[0:1, :])


def _lru_fwd_kernel(u_ref, cw_ref, cb_ref, wg_ref, ba_ref, bx_ref, lam_ref, hf_ref,
                    ubuf, halo, hcar, abuf, bbuf, hbuf):
    _lru_gates_scan(False, pl.program_id(1), u_ref, cw_ref, cb_ref, wg_ref, ba_ref, bx_ref, lam_ref,
                    ubuf, halo, hcar, abuf, bbuf, hbuf)
    hf_ref[0] = hbuf[...]


def _lru_bwd_kernel(u_ref, cw_ref, cb_ref, wg_ref, ba_ref, bx_ref, lam_ref, hf_ref, gg_ref, x_ref, mod_ref,
                    wo_ref, o_ref, ubuf, halo, hcar, abuf, bbuf, hbuf):
    _lru_gates_scan(True, pl.program_id(1), u_ref, cw_ref, cb_ref, wg_ref, ba_ref, bx_ref, lam_ref,
                    ubuf, halo, hcar, abuf, bbuf, hbuf)
    y = (hf_ref[0] + hbuf[...]) * gg_ref[0]
    out = jnp.dot(y.astype(BF16), wo_ref[...], preferred_element_type=F32)
    o_ref[0] = x_ref[0] + mod_ref[0, 0][2:3] * out


def _lru_mixer(x, mods, g, w_in, conv_w, conv_b, wa, ba, wx, bx, lam, w_out):
    b, n, d = x.shape
    nt = n // TR
    nb = d // LRU_BLOCKS
    u, gg = pl.pallas_call(
        _lru_a_kernel, grid=(b, nt),
        in_specs=[_row_spec(TR, d), _seg_mod_spec(d), _full_spec((1, d)), _full_spec((d, 2 * d))],
        out_specs=[_row_spec(TR, d), _row_spec(TR, d)],
        out_shape=[jax.ShapeDtypeStruct((b, n, d), F32), jax.ShapeDtypeStruct((b, n, d), F32)],
        compiler_params=_params(("parallel", "parallel")),
        name="lru_in",
    )(x, mods, g.reshape(1, d), w_in.astype(BF16))

    wg = jnp.concatenate([wa, wx], axis=-1).astype(BF16)
    scratch = [pltpu.VMEM((TR + SUBLANES, d), F32), pltpu.VMEM((SUBLANES, d), F32),
               pltpu.VMEM((SUBLANES, d), F32), pltpu.VMEM((TR, d), F32), pltpu.VMEM((TR, d), F32),
               pltpu.VMEM((TR, d), F32)]

    def gate_specs(order):
        return [pl.BlockSpec((1, TR, d), lambda bb, t: (bb, order(t), 0)),
                _full_spec((LRU_CONV_W, d)), _full_spec((1, d)), _full_spec((LRU_BLOCKS, nb, 2 * nb)),
                _full_spec((1, d)), _full_spec((1, d)), _full_spec((1, d))]

    def gate_args(dd):
        return (u, conv_w[dd], conv_b[dd].reshape(1, d), wg[dd], ba[dd].reshape(1, d),
                bx[dd].reshape(1, d), lam[dd].reshape(1, d))

    hf = pl.pallas_call(
        _lru_fwd_kernel, grid=(b, nt),
        in_specs=gate_specs(lambda t: t),
        out_specs=_row_spec(TR, d),
        out_shape=jax.ShapeDtypeStruct((b, n, d), F32),
        scratch_shapes=scratch,
        compiler_params=_params(("arbitrary", "arbitrary")),
        name="lru_fwd",
    )(*gate_args(0))

    def rev(t):
        return jnp.where(t == 0, 0, nt - t)

    def rev_spec():
        return pl.BlockSpec((1, TR, d), lambda bb, t: (bb, rev(t), 0))

    return pl.pallas_call(
        _lru_bwd_kernel, grid=(b, nt),
        in_specs=gate_specs(rev) + [rev_spec(), rev_spec(), rev_spec(),
                                    pl.BlockSpec((1, 1, N_MOD, d), lambda bb, t: (bb, jnp.minimum(t, 1), 0, 0)),
                                    _full_spec((d, d))],
        out_specs=rev_spec(),
        out_shape=jax.ShapeDtypeStruct((b, n, d), F32),
        scratch_shapes=scratch,
        compiler_params=_params(("arbitrary", "arbitrary")),
        name="lru_bwd",
    )(*gate_args(1), hf, gg, x, mods, w_out.astype(BF16))


def _store_token_tiles(ref, mat):
    rows = mat.shape[0]
    for s in range(SUBLANES):
        ref[pl.ds(s, rows, stride=SUBLANES), :] = mat[:, s * LANES:(s + 1) * LANES]


def _load_token_tiles(ref, rows):
    return jnp.concatenate([ref[pl.ds(s, rows, stride=SUBLANES), :] for s in range(SUBLANES)], axis=1)


def _tile_rows(ref, first_token, n_tokens):
    return ref.at[pl.ds(pl.multiple_of(first_token * SUBLANES, SUBLANES), n_tokens * SUBLANES)]
def _route_kernel(x_ref, mod_ref, g_ref, wr_ref, rb_ref, h_ref, e0_ref, e1_ref, p0_ref, p1_ref, gt_ref, cnt_ref,
                  carry):
    first = jnp.logical_and(pl.program_id(0) == 0, pl.program_id(1) == 0)
    tr = x_ref.shape[1]

    @pl.when(first)
    def _():
        carry[...] = jnp.zeros_like(carry)

    m = mod_ref[0, 0]
    h = _norm_mod(x_ref[0], g_ref[...], m[3:4], m[4:5])
    _store_token_tiles(h_ref, h)
    logit = lax.dot_general(wr_ref[...], h.astype(BF16), (((1,), (1,)), ((), ())),
                            preferred_element_type=F32)
    s = jax.nn.sigmoid(logit)
    sel = s + rb_ref[...]
    sj = [s[SUBLANES * j:SUBLANES * (j + 1)] for j in range(EXPERTS_PER_GROUP)]
    cj = [sel[SUBLANES * j:SUBLANES * (j + 1)] for j in range(EXPERTS_PER_GROUP)]

    hi1, lo1 = jnp.maximum(cj[0], cj[1]), jnp.minimum(cj[0], cj[1])
    hi2, lo2 = jnp.maximum(cj[2], cj[3]), jnp.minimum(cj[2], cj[3])
    top1 = jnp.maximum(hi1, hi2)
    top2 = jnp.maximum(jnp.minimum(hi1, hi2), jnp.maximum(lo1, lo2))
    row = lax.broadcasted_iota(I32, (SUBLANES, tr), 0)
    gscore = jnp.where(row < N_GROUPS, top1 + top2, -jnp.inf)
    gmax = jnp.max(gscore, axis=0, keepdims=True)
    grp = jnp.min(jnp.where(gscore == gmax, row, SUBLANES), axis=0, keepdims=True)
    pick = row == grp
    c = [jnp.sum(jnp.where(pick, v, 0.0), axis=0, keepdims=True) for v in cj]
    w = [jnp.sum(jnp.where(pick, v, 0.0), axis=0, keepdims=True) for v in sj]

    def argtop(vals):
        best, bi, bw = vals[0], jnp.zeros_like(grp), w[0]
        for j in range(1, EXPERTS_PER_GROUP):
            better = vals[j] > best
            best = jnp.where(better, vals[j], best)
            bi = jnp.where(better, j, bi)
            bw = jnp.where(better, w[j], bw)
        return bi, bw

    i0, w0 = argtop(c)
    i1, w1 = argtop([jnp.where(i0 == j, -jnp.inf, c[j]) for j in range(EXPERTS_PER_GROUP)])
    e0 = grp * EXPERTS_PER_GROUP + i0
    e1 = grp * EXPERTS_PER_GROUP + i1
    wsum = w0 + w1
    g0 = w0 / wsum
    g1 = w1 / wsum

    erow = lax.broadcasted_iota(I32, (N_EXPERTS, tr), 0)
    member = jnp.logical_or(erow == e0, erow == e1)
    upper = (lax.broadcasted_iota(I32, (tr, tr), 0) < lax.broadcasted_iota(I32, (tr, tr), 1))
    before = jnp.dot(member.astype(BF16), upper.astype(BF16), preferred_element_type=F32) + carry[...]
    p0 = jnp.sum(jnp.where(erow == e0, before, 0.0), axis=0, keepdims=True)
    p1 = jnp.sum(jnp.where(erow == e1, before, 0.0), axis=0, keepdims=True)
    total = carry[...] + jnp.sum(member.astype(F32), axis=1, keepdims=True)
    carry[...] = total
    cnt_ref[...] = total.astype(I32)

    e0_ref[0] = e0
    e1_ref[0] = e1
    p0_ref[0] = p0.astype(I32)
    p1_ref[0] = p1.astype(I32)
    grow = lax.broadcasted_iota(I32, (LANES, tr), 0)
    gmat = jnp.where(grow == 0, g0, jnp.where(grow == 1, g1, 0.0))
    gt_ref[0] = gmat.T


def _dispatch_kernel(slot_ref, zrow_ref, nu_ref, h_ref, xs_ref, ring, zbuf, sem, zsem, *, n_tok, n_blocks):
    s = pl.program_id(0)
    ns = pl.num_programs(0)
    tr = h_ref.shape[0] // SUBLANES
    blk = zbuf.shape[0] // SUBLANES
    par = s % 2

    @pl.when(s == 0)
    def _():
        zbuf[...] = jnp.zeros_like(zbuf)
        zsrc = _tile_rows(zbuf, 0, 1)
        for e in range(N_EXPERTS):
            first, npad = zrow_ref[e], zrow_ref[N_EXPERTS + e]

            def zero_row(i, carry, first=first):
                pltpu.make_async_copy(zsrc, _tile_rows(xs_ref, first + i, 1), zsem).start()
                return carry

            def zero_row_wait(i, carry):
                pltpu.make_async_copy(zsrc, _tile_rows(xs_ref, 0, 1), zsem).wait()
                return carry

            lax.fori_loop(0, npad, zero_row, 0)
            lax.fori_loop(0, npad, zero_row_wait, 0)

        def zero_blk(i, carry):
            pltpu.make_async_copy(zbuf, _tile_rows(xs_ref, i * blk, blk), zsem).start()
            return carry

        def zero_blk_wait(i, carry):
            pltpu.make_async_copy(zbuf, _tile_rows(xs_ref, 0, blk), zsem).wait()
            return carry

        lax.fori_loop(nu_ref[0], n_blocks, zero_blk, 0)
        lax.fori_loop(nu_ref[0], n_blocks, zero_blk_wait, 0)

    def wait_ring(p):
        for _ in range(TOP_K):
            pltpu.make_async_copy(ring.at[p], _tile_rows(xs_ref, 0, tr), sem.at[p]).wait()

    @pl.when(s >= 2)
    def _():
        wait_ring(par)

    ring[par] = h_ref[...]
    base = s * tr

    def issue(i, carry):
        for k in range(TOP_K):
            pltpu.make_async_copy(_tile_rows(ring.at[par], i, 1),
                                  _tile_rows(xs_ref, slot_ref[k * n_tok + base + i], 1), sem.at[par]).start()
        return carry

    lax.fori_loop(0, tr, issue, 0, unroll=8)

    @pl.when(s == ns - 1)
    def _():
        wait_ring(par)

        @pl.when(ns >= 2)
        def _():
            wait_ring(1 - par)


def _expert_kernel(be_ref, nu_ref, xs_ref, w1_ref, w3_ref, w2_ref, ys_ref, w1b, w3b, w2b):
    i = pl.program_id(0)
    used = i < nu_ref[0]
    new_expert = jnp.logical_or(i == 0, be_ref[i] != be_ref[jnp.maximum(i - 1, 0)])

    @pl.when(jnp.logical_and(used, new_expert))
    def _():
        w1b[...] = w1_ref[0, 0].astype(BF16)
        w3b[...] = w3_ref[0, 0].astype(BF16)
        w2b[...] = w2_ref[0, 0].astype(BF16)

    @pl.when(used)
    def _():
        xb = _load_token_tiles(xs_ref, xs_ref.shape[0] // SUBLANES).astype(BF16)
        a = jnp.dot(xb, w1b[...], preferred_element_type=F32)
        b = jnp.dot(xb, w3b[...], preferred_element_type=F32)
        hid = (a * jax.nn.sigmoid(a)) * b
        _store_token_tiles(ys_ref, jnp.dot(hid.astype(BF16), w2b[...], preferred_element_type=F32))

    @pl.when(jnp.logical_not(used))
    def _():
        ys_ref[...] = jnp.zeros_like(ys_ref)


def _combine_kernel(slot_ref, x_ref, mod_ref, gt_ref, nf_ref, ys_ref, o_ref, ybuf, sem,
                    *, n_tok, rows_per_batch, tile_off, final):
    bb, t = pl.program_id(0), pl.program_id(1)
    nt = pl.num_programs(1)
    tr = x_ref.shape[1]
    step = bb * nt + t
    nsteps = pl.num_programs(0) * nt

    def tok_base(s):
        return (s // nt) * rows_per_batch + (s % nt + tile_off) * tr

    def issue(s):
        par = s % 2
        base = tok_base(s)

        def one(i, carry):
            for k in range(TOP_K):
                pltpu.make_async_copy(_tile_rows(ys_ref, slot_ref[k * n_tok + base + i], 1),
                                      _tile_rows(ybuf.at[par, k], i, 1), sem.at[par]).start()
            return carry

        lax.fori_loop(0, tr, one, 0, unroll=8)

    @pl.when(step == 0)
    def _():
        issue(step)

    @pl.when(step + 1 < nsteps)
    def _():
        issue(step + 1)

    par = step % 2
    for k in range(TOP_K):
        pltpu.make_async_copy(_tile_rows(ys_ref, 0, tr), ybuf.at[par, k], sem.at[par]).wait()

    gt = gt_ref[0]
    y = (gt[:, 0:1] * _load_token_tiles(ybuf.at[par, 0], tr)
         + gt[:, 1:2] * _load_token_tiles(ybuf.at[par, 1], tr))
    out = x_ref[0] + mod_ref[0, 0][5:6] * y
    if final:
        out = out * lax.rsqrt(jnp.mean(out * out, axis=-1, keepdims=True) + EPS) * nf_ref[...]
    o_ref[0] = out


def _moe(x, mods, g, wr_perm, rb_perm, layer, w1, w3, w2, norm_f, final, ctx_len):
    b, n, d = x.shape
    nt = n // TR
    n_tok = b * n
    de = w1.shape[-1]

    def lane_major():
        return pl.BlockSpec((1, 1, TR), lambda bb, t: (bb * nt + t, 0, 0))

    lane_major_shape = jax.ShapeDtypeStruct((b * nt, 1, TR), I32)
    h, e0, e1, p0, p1, gtab, counts = pl.pallas_call(
        _route_kernel, grid=(b, nt),
        in_specs=[_row_spec(TR, d), _seg_mod_spec(d), _full_spec((1, d)),
                  _full_spec((SUBLANES * EXPERTS_PER_GROUP, d)), _full_spec((SUBLANES * EXPERTS_PER_GROUP, 1))],
        out_specs=[pl.BlockSpec((TR * SUBLANES, LANES), lambda bb, t: (bb * nt + t, 0)),
                   lane_major(), lane_major(), lane_major(), lane_major(),
                   _row_spec(TR, LANES),
                   pl.BlockSpec((N_EXPERTS, 1), lambda bb, t: (0, 0))],
        out_shape=[jax.ShapeDtypeStruct((n_tok * SUBLANES, LANES), F32), lane_major_shape, lane_major_shape,
                   lane_major_shape, lane_major_shape,
                   jax.ShapeDtypeStruct((b, n, LANES), F32), jax.ShapeDtypeStruct((N_EXPERTS, 1), I32)],
        scratch_shapes=[pltpu.VMEM((N_EXPERTS, 1), F32)],
        compiler_params=_params(("arbitrary", "arbitrary")),
        name="moe_route",
    )(x, mods, g.reshape(1, d), wr_perm, rb_perm)

    counts = counts[:, 0]
    padded = (counts + MOE_BLK - 1) // MOE_BLK * MOE_BLK
    pend = jnp.cumsum(padded)
    pstart = (pend - padded).astype(I32)
    n_slots = n_tok * TOP_K + N_EXPERTS * MOE_BLK
    n_blocks = n_slots // MOE_BLK
    n_used = (pend[-1] // MOE_BLK).astype(I32).reshape(1)
    blk_start = jnp.arange(n_blocks, dtype=I32) * MOE_BLK
    blk_e = jnp.minimum(jnp.sum((pend[None, :] <= blk_start[:, None]).astype(I32), axis=1), N_EXPERTS - 1)

    def slot_of(e, p):
        start = jnp.zeros_like(e)
        for j in range(N_EXPERTS):
            start = jnp.where(e == j, pstart[j], start)
        return (start + p).reshape(n_tok)

    slots = jnp.concatenate([slot_of(e0, p0), slot_of(e1, p1)])
    zrow = jnp.concatenate([pstart + counts, padded - counts]).astype(I32)

    xs = pl.pallas_call(
        functools.partial(_dispatch_kernel, n_tok=n_tok, n_blocks=n_blocks),
        grid_spec=pltpu.PrefetchScalarGridSpec(
            num_scalar_prefetch=3, grid=(n_tok // TR,),
            in_specs=[pl.BlockSpec((TR * SUBLANES, LANES), lambda s, sl, zr, nu: (s, 0))],
            out_specs=pl.BlockSpec(memory_space=pl.ANY),
            scratch_shapes=[pltpu.VMEM((2, TR * SUBLANES, LANES), F32), pltpu.VMEM((MOE_BLK * SUBLANES, LANES), F32),
                            pltpu.SemaphoreType.DMA((2,)), pltpu.SemaphoreType.DMA]),
        out_shape=jax.ShapeDtypeStruct((n_slots * SUBLANES, LANES), F32),
        compiler_params=_params(("arbitrary",)),
        name="moe_dispatch",
    )(slots, zrow, n_used, h)

    def wspec(shape):
        return pl.BlockSpec((1, 1) + shape, lambda i, be, nu: (layer, be[i], 0, 0))

    def blk_spec():
        return pl.BlockSpec((MOE_BLK * SUBLANES, LANES), lambda i, be, nu: (i, 0))

    ys = pl.pallas_call(
        _expert_kernel,
        grid_spec=pltpu.PrefetchScalarGridSpec(
            num_scalar_prefetch=2, grid=(n_blocks,),
            in_specs=[blk_spec(), wspec((d, de)), wspec((d, de)), wspec((de, d))],
            out_specs=blk_spec(),
            scratch_shapes=[pltpu.VMEM((d, de), BF16), pltpu.VMEM((d, de), BF16), pltpu.VMEM((de, d), BF16)]),
        out_shape=jax.ShapeDtypeStruct((n_slots * SUBLANES, LANES), F32),
        compiler_params=_params(("arbitrary",)),
        name="moe_experts",
    )(blk_e, n_used, xs, w1, w3, w2)

    tile_off = ctx_len // TR if final else 0
    nt_out = nt - tile_off
    out_rows = nt_out * TR

    def rows(width):
        return pl.BlockSpec((1, TR, width), lambda bb, t, s: (bb, t + tile_off, 0))

    return pl.pallas_call(
        functools.partial(_combine_kernel, n_tok=n_tok, rows_per_batch=n, tile_off=tile_off, final=final),
        grid_spec=pltpu.PrefetchScalarGridSpec(
            num_scalar_prefetch=1, grid=(b, nt_out),
            in_specs=[rows(d),
                      pl.BlockSpec((1, 1, N_MOD, d), lambda bb, t, s: (bb, jnp.minimum(t + tile_off, 1), 0, 0)),
                      rows(LANES), pl.BlockSpec((1, d), lambda bb, t, s: (0, 0)),
                      pl.BlockSpec(memory_space=pl.ANY)],
            out_specs=pl.BlockSpec((1, TR, d), lambda bb, t, s: (bb, t, 0)),
            scratch_shapes=[pltpu.VMEM((2, TOP_K, TR * SUBLANES, LANES), F32), pltpu.SemaphoreType.DMA((2,))]),
        out_shape=jax.ShapeDtypeStruct((b, out_rows, d), F32),
        compiler_params=_params(("arbitrary", "arbitrary")),
        name="moe_combine",
    )(slots, x, mods, gtab, norm_f.reshape(1, d), ys)


def kernel(x, c, ctx, c_ctx, w_mod, b_mod, norm_mix, norm_ffn, norm_f, conv_pw1, conv_pw1_b, conv_dw, conv_dw_b, conv_ln_g, conv_ln_b, conv_pw2, conv_pw2_b, attn_w_qkv, attn_w_o, attn_sink, lru_w_in, lru_conv_w, lru_conv_b, lru_wa, lru_ba, lru_wx, lru_bx, lru_lam, lru_w_out, moe_w_router, moe_router_bias, moe_w1, moe_w3, moe_w2):
    b, s, d = x.shape
    ctx_len = ctx.shape[1]
    depth = w_mod.shape[0]
    assert ctx_len == TR and s % TR == 0 and d == SUBLANES * LANES
    assert moe_w_router.shape[1] == N_EXPERTS

    mods = _modulation(c, c_ctx, w_mod, b_mod)
    xs = jnp.concatenate([ctx, x], axis=1)

    wr = jnp.zeros((EXPERTS_PER_GROUP, SUBLANES, d), F32).at[:, :N_GROUPS].set(
        moe_w_router.T.reshape(N_GROUPS, EXPERTS_PER_GROUP, d).swapaxes(0, 1))
    wr = wr.reshape(EXPERTS_PER_GROUP * SUBLANES, d).astype(BF16)
    rb = jnp.zeros((EXPERTS_PER_GROUP, SUBLANES), F32).at[:, :N_GROUPS].set(
        moe_router_bias.astype(F32).reshape(N_GROUPS, EXPERTS_PER_GROUP).T).reshape(-1, 1)

    for i in range(depth):
        kind, slot = i % N_MIXERS, i // N_MIXERS
        m = mods[i]
        if kind == 0:
            xs = _conv_mixer(xs, m, norm_mix[i], conv_pw1[slot], conv_pw1_b[slot], conv_dw[slot], conv_dw_b[slot],
                             conv_ln_g[slot], conv_ln_b[slot], conv_pw2[slot], conv_pw2_b[slot])
        elif kind == 1:
            xs = _attn_mixer(xs, m, norm_mix[i], attn_w_qkv[slot], attn_w_o[slot], attn_sink[slot], ctx_len)
        else:
            xs = _lru_mixer(xs, m, norm_mix[i], lru_w_in[slot], lru_conv_w[slot], lru_conv_b[slot], lru_wa[slot],
                            lru_ba[slot], lru_wx[slot], lru_bx[slot], lru_lam[slot], lru_w_out[slot])
        xs = _moe(xs, m, norm_ffn[i], wr, rb, i, moe_w1, moe_w3, moe_w2, norm_f, i == depth - 1, ctx_len)
    return xs
```

```python
import functools

import jax
import jax.numpy as jnp
from jax import lax
from jax.experimental import pallas as pl
from jax.experimental.pallas import tpu as pltpu

F32 = jnp.float32
BF16 = jnp.bfloat16
I32 = jnp.int32

EPS = 1e-6
N_MOD = 6
N_MIXERS = 3
GRID_W = 64
CONV_WIDTH = 31
CONV_HALF = (CONV_WIDTH - 1) // 2
HEAD_DIM = 64
N_KV_HEADS = 4
GQA_GROUP = 4
WINDOW = 128
ROPE_BASE = 10000.0
ROPE_PAIRS = HEAD_DIM // 4
LRU_BLOCKS = 8
LRU_CONV_W = 4
LRU_C = 8.0
N_EXPERTS = 16
N_GROUPS = 4
EXPERTS_PER_GROUP = 4
TOP_K = 2

LANES = 128
SUBLANES = 8
TR = 256
TQ = 128
HALO = 16
MOE_BLK = 256
ROUTER_ROWS = SUBLANES * EXPERTS_PER_GROUP
NEG = -1e30
VMEM_LIMIT = 56 * 1024 * 1024


def _params(sem, vmem=VMEM_LIMIT):
    return pltpu.CompilerParams(dimension_semantics=sem, vmem_limit_bytes=vmem)


def _norm_mod(x, g, shift, scale):
    y = x * lax.rsqrt(jnp.mean(x * x, axis=-1, keepdims=True) + EPS)
    return (y * g) * (1.0 + scale) + shift


def _seg_mod_spec(d, tiles_per_seg=1, tile_off=0):
    return pl.BlockSpec((1, 1, N_MOD, d),
                        lambda b, t, *_: (b, jnp.minimum((t + tile_off) // tiles_per_seg, 1), 0, 0))


def _row_spec(tr, d, tile_off=0):
    return pl.BlockSpec((1, tr, d), lambda b, t, *_: (b, t + tile_off, 0))


def _full_spec(shape):
    nd = len(shape)
    return pl.BlockSpec(shape, lambda b, t, *_: (0,) * nd)


def _mod_kernel(c_ref, w_ref, b_ref, o_ref):
    c = c_ref[...]
    sc = c * jax.nn.sigmoid(c)
    o_ref[0] = jnp.dot(sc.astype(BF16), w_ref[0].astype(BF16), preferred_element_type=F32) + b_ref[0]


def _modulation(c, c_ctx, w_mod, b_mod):
    depth, d, nout = w_mod.shape
    b = c.shape[0]
    assert b + 1 <= SUBLANES
    rows = jnp.zeros((SUBLANES, d), F32).at[:b].set(c).at[b].set(c_ctx)
    tn = 1536
    raw = pl.pallas_call(
        _mod_kernel,
        grid=(depth, nout // tn),
        in_specs=[pl.BlockSpec((SUBLANES, d), lambda i, j: (0, 0)),
                  pl.BlockSpec((1, d, tn), lambda i, j: (i, 0, j)),
                  pl.BlockSpec((1, 1, tn), lambda i, j: (i, 0, j))],
        out_specs=pl.BlockSpec((1, SUBLANES, tn), lambda i, j: (i, 0, j)),
        out_shape=jax.ShapeDtypeStruct((depth, SUBLANES, nout), F32),
        compiler_params=_params(("arbitrary", "arbitrary")),
        name="modulation",
    )(rows, w_mod, b_mod.reshape(depth, 1, nout))
    raw = raw.reshape(depth, SUBLANES, N_MOD, d)
    lat = raw[:, :b]
    ctx = jnp.broadcast_to(raw[:, b][:, None], lat.shape)
    return jnp.stack([ctx, lat], axis=2)


def _store_token_tiles(ref, mat):
    rows = mat.shape[0]
    for s in range(SUBLANES):
        ref[pl.ds(s, rows, stride=SUBLANES), :] = mat[:, s * LANES:(s + 1) * LANES]


def _load_token_tiles(ref, rows):
    return jnp.concatenate([ref[pl.ds(s, rows, stride=SUBLANES), :] for s in range(SUBLANES)], axis=1)


def _tile_rows(ref, first_token, n_tokens):
    return ref.at[pl.ds(pl.multiple_of(first_token * SUBLANES, SUBLANES), n_tokens * SUBLANES)]


def _route_tile(x, m, g_ref, wr_ref, rb_ref, carry, first, h_ref, e0_ref, e1_ref, p0_ref, p1_ref, gt_ref, cnt_ref):
    tr = x.shape[0]

    @pl.when(first)
    def _():
        carry[...] = jnp.zeros_like(carry)

    h = _norm_mod(x, g_ref[...], m[3:4], m[4:5])
    _store_token_tiles(h_ref, h)
    logit = lax.dot_general(wr_ref[...], h.astype(BF16), (((1,), (1,)), ((), ())),
                            preferred_element_type=F32)
    s = jax.nn.sigmoid(logit)
    sel = s + rb_ref[...]
    sj = [s[SUBLANES * j:SUBLANES * (j + 1)] for j in range(EXPERTS_PER_GROUP)]
    cj = [sel[SUBLANES * j:SUBLANES * (j + 1)] for j in range(EXPERTS_PER_GROUP)]

    hi1, lo1 = jnp.maximum(cj[0], cj[1]), jnp.minimum(cj[0], cj[1])
    hi2, lo2 = jnp.maximum(cj[2], cj[3]), jnp.minimum(cj[2], cj[3])
    top1 = jnp.maximum(hi1, hi2)
    top2 = jnp.maximum(jnp.minimum(hi1, hi2), jnp.maximum(lo1, lo2))
    row = lax.broadcasted_iota(I32, (SUBLANES, tr), 0)
    gscore = jnp.where(row < N_GROUPS, top1 + top2, -jnp.inf)
    gmax = jnp.max(gscore, axis=0, keepdims=True)
    grp = jnp.min(jnp.where(gscore == gmax, row, SUBLANES), axis=0, keepdims=True)
    pick = row == grp
    c = [jnp.sum(jnp.where(pick, v, 0.0), axis=0, keepdims=True) for v in cj]
    w = [jnp.sum(jnp.where(pick, v, 0.0), axis=0, keepdims=True) for v in sj]

    def argtop(vals):
        best, bi, bw = vals[0], jnp.zeros_like(grp), w[0]
        for j in range(1, EXPERTS_PER_GROUP):
            better = vals[j] > best
            best = jnp.where(better, vals[j], best)
            bi = jnp.where(better, j, bi)
            bw = jnp.where(better, w[j], bw)
        return bi, bw

    i0, w0 = argtop(c)
    i1, w1 = argtop([jnp.where(i0 == j, -jnp.inf, c[j]) for j in range(EXPERTS_PER_GROUP)])
    e0 = grp * EXPERTS_PER_GROUP + i0
    e1 = grp * EXPERTS_PER_GROUP + i1
    wsum = w0 + w1
    g0 = w0 / wsum
    g1 = w1 / wsum

    erow = lax.broadcasted_iota(I32, (N_EXPERTS, tr), 0)
    member = jnp.logical_or(erow == e0, erow == e1)
    upper = (lax.broadcasted_iota(I32, (tr, tr), 0) < lax.broadcasted_iota(I32, (tr, tr), 1))
    before = jnp.dot(member.astype(BF16), upper.astype(BF16), preferred_element_type=F32) + carry[...]
    p0 = jnp.sum(jnp.where(erow == e0, before, 0.0), axis=0, keepdims=True)
    p1 = jnp.sum(jnp.where(erow == e1, before, 0.0), axis=0, keepdims=True)
    total = carry[...] + jnp.sum(member.astype(F32), axis=1, keepdims=True)
    carry[...] = total
    cnt_ref[...] = total.astype(I32)

    e0_ref[0] = e0
    e1_ref[0] = e1
    p0_ref[0] = p0.astype(I32)
    p1_ref[0] = p1.astype(I32)
    grow = lax.broadcasted_iota(I32, (LANES, tr), 0)
    gmat = jnp.where(grow == 0, g0, jnp.where(grow == 1, g1, 0.0))
    gt_ref[0] = gmat.T


def _route_io(b, n, d, tile, tile_of):
    nt = n // tile

    def flat(bb, t):
        return bb * nt + tile_of(t)

    lane_major = pl.BlockSpec((1, 1, tile), lambda bb, t, *_: (flat(bb, t), 0, 0))
    lane_major_shape = jax.ShapeDtypeStruct((b * nt, 1, tile), I32)
    in_specs = [_full_spec((1, d)), _full_spec((ROUTER_ROWS, d)), _full_spec((ROUTER_ROWS, 1))]
    out_specs = [pl.BlockSpec((tile * SUBLANES, LANES), lambda bb, t, *_: (flat(bb, t), 0)),
                 lane_major, lane_major, lane_major, lane_major,
                 pl.BlockSpec((1, tile, LANES), lambda bb, t, *_: (bb, tile_of(t), 0)),
                 pl.BlockSpec((N_EXPERTS, 1), lambda bb, t, *_: (0, 0))]
    out_shape = [jax.ShapeDtypeStruct((b * n * SUBLANES, LANES), F32), lane_major_shape, lane_major_shape,
                 lane_major_shape, lane_major_shape,
                 jax.ShapeDtypeStruct((b, n, LANES), F32), jax.ShapeDtypeStruct((N_EXPERTS, 1), I32)]
    scratch = [pltpu.VMEM((N_EXPERTS, 1), F32)]
    return in_specs, out_specs, out_shape, scratch


def _first_step():
    return jnp.logical_and(pl.program_id(0) == 0, pl.program_id(1) == 0)


def _glu_tail(h, w_ref, b_ref, o_ref):
    d = h.shape[-1]
    u = jnp.dot(h.astype(BF16), w_ref[...], preferred_element_type=F32) + b_ref[...]
    o_ref[0] = u[:, :d] * jax.nn.sigmoid(u[:, d:])


def _qkv_tail(h, w_ref, cos_ref, sin_ref, q_ref, k_ref, v_ref):
    d = h.shape[-1]
    nk = k_ref.shape[-1]
    t = jnp.dot(h.astype(BF16), w_ref[...], preferred_element_type=F32)
    cos = cos_ref[...]
    sin = sin_ref[...]
    lane = lax.broadcasted_iota(I32, cos.shape, 1)
    first_half = (lane % (2 * ROPE_PAIRS)) < ROPE_PAIRS

    def rope(xg):
        partner = jnp.where(first_half, pltpu.roll(xg, LANES - ROPE_PAIRS, 1), pltpu.roll(xg, ROPE_PAIRS, 1))
        return xg * cos + partner * sin

    for j in range(d // LANES):
        cs = slice(j * LANES, (j + 1) * LANES)
        q_ref[0, :, cs] = rope(t[:, cs]).astype(BF16)
    for j in range(nk // LANES):
        cs = slice(j * LANES, (j + 1) * LANES)
        k_ref[0, :, cs] = rope(t[:, d + j * LANES:d + (j + 1) * LANES]).astype(BF16)
    v_ref[0] = t[:, d + nk:].astype(BF16)


def _lru_in_tail(h, w_ref, u_ref, gg_ref):
    d = h.shape[-1]
    t = jnp.dot(h.astype(BF16), w_ref[...], preferred_element_type=F32)
    u_ref[0] = t[:, :d]
    gg_ref[0] = jax.nn.gelu(t[:, d:])


def _rope_tables(n, ctx_len):
    pos = jnp.arange(n - ctx_len, dtype=I32)
    inv = ROPE_BASE ** (-jnp.arange(ROPE_PAIRS, dtype=F32) / ROPE_PAIRS)
    ar = (pos // GRID_W).astype(F32)[:, None] * inv
    ac = (pos % GRID_W).astype(F32)[:, None] * inv
    ang = jnp.concatenate([ar, ar, ac, ac], axis=-1)
    sign = jnp.tile(jnp.concatenate([-jnp.ones((ROPE_PAIRS,), F32), jnp.ones((ROPE_PAIRS,), F32)]), 2)
    cos = jnp.concatenate([jnp.ones((ctx_len, HEAD_DIM), F32), jnp.cos(ang)], axis=0)
    sin = jnp.concatenate([jnp.zeros((ctx_len, HEAD_DIM), F32), jnp.sin(ang) * sign], axis=0)
    return jnp.tile(cos, (1, 2)), jnp.tile(sin, (1, 2))


def _in_proj(kind, p, b, n, d, ctx_len):
    if kind == 0:
        args = (p["conv_pw1"].astype(BF16), p["conv_pw1_b"].reshape(1, 2 * d))
        return dict(tail=_glu_tail, args=args, in_specs=[_full_spec((d, 2 * d)), _full_spec((1, 2 * d))],
                    out_specs=[_row_spec(TR, d)], out_shape=[jax.ShapeDtypeStruct((b, n, d), F32)])
    if kind == 1:
        kvd = N_KV_HEADS * HEAD_DIM
        nk = N_KV_HEADS * LANES
        assert d == 2 * nk
        scale = HEAD_DIM ** -0.5
        w_qkv = p["attn_w_qkv"]

        def dup(w):
            w = w.reshape(d, N_KV_HEADS, HEAD_DIM)
            return jnp.concatenate([w, w], axis=-1).reshape(d, nk)

        w_all = jnp.concatenate([w_qkv[:, :d] * scale, dup(w_qkv[:, d:d + kvd]), dup(w_qkv[:, d + kvd:])],
                                axis=1).astype(BF16)
        cos, sin = _rope_tables(n, ctx_len)
        table = pl.BlockSpec((TR, LANES), lambda bb, t, *_: (t, 0))
        return dict(tail=_qkv_tail, args=(w_all, cos, sin), in_specs=[_full_spec((d, d + 2 * nk)), table, table],
                    out_specs=[_row_spec(TR, d), _row_spec(TR, nk), _row_spec(TR, nk)],
                    out_shape=[jax.ShapeDtypeStruct((b, n, d), BF16), jax.ShapeDtypeStruct((b, n, nk), BF16),
                               jax.ShapeDtypeStruct((b, n, nk), BF16)])
    return dict(tail=_lru_in_tail, args=(p["lru_w_in"].astype(BF16),), in_specs=[_full_spec((d, 2 * d))],
                out_specs=[_row_spec(TR, d), _row_spec(TR, d)],
                out_shape=[jax.ShapeDtypeStruct((b, n, d), F32), jax.ShapeDtypeStruct((b, n, d), F32)])


def _in_proj_kernel(x_ref, mod_ref, g_ref, *refs, tail):
    m = mod_ref[0, 0]
    tail(_norm_mod(x_ref[0], g_ref[...], m[0:1], m[1:2]), *refs)


def _in_proj_standalone(ip, x, mods, g):
    b, n, d = x.shape
    return pl.pallas_call(
        functools.partial(_in_proj_kernel, tail=ip["tail"]), grid=(b, n // TR),
        in_specs=[_row_spec(TR, d), _seg_mod_spec(d), _full_spec((1, d))] + ip["in_specs"],
        out_specs=ip["out_specs"], out_shape=ip["out_shape"],
        compiler_params=_params(("parallel", "parallel")),
        name="in_proj",
    )(x, mods, g.reshape(1, d), *ip["args"])


def _conv_core_kernel(x_ref, mod_ref, gp_ref, gc_ref, gn_ref, dw_ref, dwb_ref, lng_ref, lnb_ref, w_ref, b_ref,
                      gffn_ref, wr_ref, rb_ref, o_ref, h_ref, e0_ref, e1_ref, p0_ref, p1_ref, gt_ref, cnt_ref,
                      gbuf, cbuf, carry):
    t = pl.program_id(1)
    nt = pl.num_programs(1)
    first = _first_step()
    tr, d = gc_ref.shape[1], gc_ref.shape[2]
    seg_first = t <= 1
    seg_last = jnp.logical_or(t == 0, t == nt - 1)
    gbuf[0, 0:HALO, :] = jnp.where(seg_first, 0.0, gp_ref[0])
    gbuf[0, HALO:HALO + tr, :] = gc_ref[0]
    gbuf[0, HALO + tr:HALO + tr + HALO, :] = jnp.where(seg_last, 0.0, gn_ref[0])
    span = tr + 2 * HALO - SUBLANES
    for r in range(1, SUBLANES):
        gbuf[r, 0:span, :] = gbuf[0, r:r + span, :]

    rc = 32
    base = HALO - CONV_HALF

    def chunk(i, carry_):
        r0 = pl.multiple_of(i * rc, rc)
        for c in range(d // LANES):
            cs = slice(c * LANES, (c + 1) * LANES)
            acc = jnp.zeros((rc, LANES), F32)
            for k in range(CONV_WIDTH):
                off = base + k
                rows = pl.ds(r0 + (off // SUBLANES) * SUBLANES, rc)
                acc = acc + dw_ref[k:k + 1, cs] * gbuf[off % SUBLANES, rows, cs]
            cbuf[pl.ds(r0, rc), cs] = acc
        return carry_

    lax.fori_loop(0, tr // rc, chunk, 0)

    u = cbuf[...] + dwb_ref[...]
    mu = jnp.mean(u, axis=-1, keepdims=True)
    uc = u - mu
    var = jnp.mean(uc * uc, axis=-1, keepdims=True)
    v = uc * lax.rsqrt(var + EPS) * lng_ref[...] + lnb_ref[...]
    v = v * jax.nn.sigmoid(v)
    y = jnp.dot(v.astype(BF16), w_ref[...], preferred_element_type=F32) + b_ref[...]
    m = mod_ref[0, 0]
    x_new = x_ref[0] + m[2:3] * y
    o_ref[0] = x_new
    _route_tile(x_new, m, gffn_ref, wr_ref, rb_ref, carry, first,
                h_ref, e0_ref, e1_ref, p0_ref, p1_ref, gt_ref, cnt_ref)


def _conv_core(x, mods, glu, p, route_args):
    b, n, d = x.shape
    nt = n // TR
    hb = TR // HALO
    nh = n // HALO
    dwp = jnp.zeros((32, d), F32).at[:CONV_WIDTH].set(p["conv_dw"])
    r_in, r_out, r_shape, r_scratch = _route_io(b, n, d, TR, lambda t: t)
    return pl.pallas_call(
        _conv_core_kernel, grid=(b, nt),
        in_specs=[_row_spec(TR, d), _seg_mod_spec(d),
                  pl.BlockSpec((1, HALO, d), lambda bb, t: (bb, jnp.maximum(t * hb - 1, 0), 0)),
                  _row_spec(TR, d),
                  pl.BlockSpec((1, HALO, d), lambda bb, t: (bb, jnp.minimum((t + 1) * hb, nh - 1), 0)),
                  _full_spec((32, d)), _full_spec((1, d)), _full_spec((1, d)), _full_spec((1, d)),
                  _full_spec((d, d)), _full_spec((1, d))] + r_in,
        out_specs=[_row_spec(TR, d)] + r_out,
        out_shape=[jax.ShapeDtypeStruct((b, n, d), F32)] + r_shape,
        scratch_shapes=[pltpu.VMEM((SUBLANES, TR + 2 * HALO, d), F32), pltpu.VMEM((TR, d), F32)] + r_scratch,
        compiler_params=_params(("arbitrary", "arbitrary")),
        name="conv_core",
    )(x, mods, glu, glu, glu, dwp, p["conv_dw_b"].reshape(1, d), p["conv_ln_g"].reshape(1, d),
      p["conv_ln_b"].reshape(1, d), p["conv_pw2"].astype(BF16), p["conv_pw2_b"].reshape(1, d), *route_args)


def _attn_core_kernel(sink_ref, x_ref, mod_ref, q_ref, kc_ref, vc_ref, k0_ref, k1_ref, k2_ref,
                      v0_ref, v1_ref, v2_ref, wo_ref, gffn_ref, wr_ref, rb_ref,
                      o_ref, h_ref, e0_ref, e1_ref, p0_ref, p1_ref, gt_ref, cnt_ref, carry, *, ctx_len, n_rows):
    t = pl.program_id(1)
    first = _first_step()
    tq = q_ref.shape[1]
    nctx = kc_ref.shape[1]
    nkeys = nctx + 3 * tq

    rq = t * tq + lax.broadcasted_iota(I32, (tq, nkeys), 0)
    col = lax.broadcasted_iota(I32, (tq, nkeys), 1)
    rk = (t - 1) * tq + (col - nctx)
    win_ok = (rk >= ctx_len) & (rk < n_rows) & (jnp.abs(rq - rk) <= WINDOW) & (t * tq >= ctx_len)
    bias = jnp.where((col < nctx) | win_ok, 0.0, NEG).astype(F32)

    lane = lax.broadcasted_iota(I32, (tq, LANES), 1)
    lo = lane < HEAD_DIM

    out_cols = []
    for g in range(N_KV_HEADS):
        gs = slice(g * LANES, (g + 1) * LANES)
        kg = jnp.concatenate([kc_ref[0, :, gs], k0_ref[0, :, gs], k1_ref[0, :, gs], k2_ref[0, :, gs]], axis=0)
        vg = jnp.concatenate([vc_ref[0, :, gs], v0_ref[0, :, gs], v1_ref[0, :, gs], v2_ref[0, :, gs]], axis=0)
        qa = q_ref[0, :, (2 * g) * LANES:(2 * g + 1) * LANES]
        qb = q_ref[0, :, (2 * g + 1) * LANES:(2 * g + 2) * LANES]
        zero = jnp.zeros_like(qa)
        q4 = jnp.concatenate([jnp.where(lo, qa, zero), jnp.where(lo, zero, qa),
                              jnp.where(lo, qb, zero), jnp.where(lo, zero, qb)], axis=0)
        s = lax.dot_general(q4, kg, (((1,), (1,)), ((), ())), preferred_element_type=F32)
        ps, dens = [], []
        for hh in range(GQA_GROUP):
            sk = sink_ref[g * GQA_GROUP + hh]
            sh = s[hh * tq:(hh + 1) * tq] + bias
            mx = jnp.maximum(jnp.max(sh, axis=-1, keepdims=True), sk)
            pr = jnp.exp(sh - mx)
            dens.append(jnp.sum(pr, axis=-1, keepdims=True) + jnp.exp(sk - mx))
            ps.append(pr.astype(BF16))
        pv = jnp.dot(jnp.concatenate(ps, axis=0), vg, preferred_element_type=F32)
        oh = [pv[hh * tq:(hh + 1) * tq] / dens[hh] for hh in range(GQA_GROUP)]
        out_cols.append(jnp.where(lo, oh[0], oh[1]))
        out_cols.append(jnp.where(lo, oh[2], oh[3]))
    o = jnp.concatenate(out_cols, axis=1).astype(BF16)
    y = jnp.dot(o, wo_ref[...], preferred_element_type=F32)
    m = mod_ref[0, 0]
    x_new = x_ref[0] + m[2:3] * y
    o_ref[0] = x_new
    _route_tile(x_new, m, gffn_ref, wr_ref, rb_ref, carry, first,
                h_ref, e0_ref, e1_ref, p0_ref, p1_ref, gt_ref, cnt_ref)


def _attn_core(x, mods, qkv, p, route_args, ctx_len):
    b, n, d = x.shape
    q, k, v = qkv
    nk = k.shape[-1]
    assert ctx_len == TR
    ntq = n // TQ

    def win(off):
        return pl.BlockSpec((1, TQ, nk), lambda bb, t, s: (bb, jnp.clip(t + off, 0, ntq - 1), 0))

    ctx_spec = pl.BlockSpec((1, ctx_len, nk), lambda bb, t, s: (bb, 0, 0))
    r_in, r_out, r_shape, r_scratch = _route_io(b, n, d, TQ, lambda t: t)
    grid_spec = pltpu.PrefetchScalarGridSpec(
        num_scalar_prefetch=1, grid=(b, ntq),
        in_specs=[_row_spec(TQ, d), _seg_mod_spec(d, tiles_per_seg=TR // TQ), _row_spec(TQ, d),
                  ctx_spec, ctx_spec, win(-1), win(0), win(1), win(-1), win(0), win(1),
                  _full_spec((d, d))] + r_in,
        out_specs=[_row_spec(TQ, d)] + r_out,
        scratch_shapes=r_scratch)
    return pl.pallas_call(
        functools.partial(_attn_core_kernel, ctx_len=ctx_len, n_rows=n),
        grid_spec=grid_spec,
        out_shape=[jax.ShapeDtypeStruct((b, n, d), F32)] + r_shape,
        compiler_params=_params(("arbitrary", "arbitrary")),
        name="attn_core",
    )(p["attn_sink"].astype(F32), x, mods, q, k, v, k, k, k, v, v, v, p["attn_w_o"].astype(BF16), *route_args)


def _lru_gates_scan(back, t, u_ref, cw_ref, cb_ref, wg_ref, ba_ref, bx_ref, lam_ref,
                    ubuf, halo, hcar, abuf, bbuf, hbuf):
    tr, d = u_ref.shape[1], u_ref.shape[2]
    nb = d // LRU_BLOCKS

    @pl.when(t <= 1)
    def _():
        halo[...] = jnp.zeros_like(halo)

    @pl.when(t == 0)
    def _():
        hcar[...] = jnp.zeros_like(hcar)

    u = u_ref[0]
    if not back:
        ubuf[0:SUBLANES, :] = halo[...]
        ubuf[SUBLANES:SUBLANES + tr, :] = u
        halo[...] = u[tr - SUBLANES:, :]
        taps = [ubuf[SUBLANES - (LRU_CONV_W - 1) + k:SUBLANES - (LRU_CONV_W - 1) + k + tr, :]
                for k in range(LRU_CONV_W)]
    else:
        ubuf[0:tr, :] = u
        ubuf[tr:tr + SUBLANES, :] = halo[...]
        halo[...] = u[:SUBLANES, :]
        taps = [ubuf[(LRU_CONV_W - 1) - k:(LRU_CONV_W - 1) - k + tr, :] for k in range(LRU_CONV_W)]
    cc = cb_ref[...] + taps[0] * cw_ref[0:1, :]
    for k in range(1, LRU_CONV_W):
        cc = cc + taps[k] * cw_ref[k:k + 1, :]

    ccb = cc.astype(BF16)
    lam = lam_ref[...]
    neg_c_softplus = -LRU_C * (jnp.maximum(-lam, 0.0) + jnp.log(1.0 + jnp.exp(-jnp.abs(lam))))
    for blk in range(LRU_BLOCKS):
        cs = slice(blk * nb, (blk + 1) * nb)
        z = jnp.dot(ccb[:, cs], wg_ref[blk], preferred_element_type=F32)
        r = jax.nn.sigmoid(z[:, :nb] + ba_ref[:, cs])
        gi = jax.nn.sigmoid(z[:, nb:] + bx_ref[:, cs])
        log_a = neg_c_softplus[:, cs] * r
        a = jnp.exp(log_a)
        abuf[:, cs] = a
        bbuf[:, cs] = jnp.sqrt(1.0 - a * a) * (gi * cc[:, cs])

    row = lax.broadcasted_iota(I32, (SUBLANES, d), 0)
    nchunk = tr // SUBLANES

    def chunk(i, h):
        ci = (nchunk - 1 - i) if back else i
        r0 = pl.multiple_of(ci * SUBLANES, SUBLANES)
        a = abuf[pl.ds(r0, SUBLANES), :]
        bb = bbuf[pl.ds(r0, SUBLANES), :]
        for k in (1, 2, 4):
            if back:
                sh, ok = SUBLANES - k, row < SUBLANES - k
            else:
                sh, ok = k, row >= k
            a_s = pltpu.roll(a, sh, 0)
            b_s = pltpu.roll(bb, sh, 0)
            bb = jnp.where(ok, a * b_s + bb, bb)
            a = jnp.where(ok, a * a_s, a)
        hh = a * h + bb
        hbuf[pl.ds(r0, SUBLANES), :] = hh
        return hh[0:1, :] if back else hh[SUBLANES - 1:SUBLANES, :]

    hcar[0:1, :] = lax.fori_loop(0, nchunk, chunk, hcar[0:1, :])


def _lru_fwd_kernel(u_ref, cw_ref, cb_ref, wg_ref, ba_ref, bx_ref, lam_ref, hf_ref,
                    ubuf, halo, hcar, abuf, bbuf, hbuf):
    _lru_gates_scan(False, pl.program_id(1), u_ref, cw_ref, cb_ref, wg_ref, ba_ref, bx_ref, lam_ref,
                    ubuf, halo, hcar, abuf, bbuf, hbuf)
    hf_ref[0] = hbuf[...]


def _lru_bwd_kernel(u_ref, cw_ref, cb_ref, wg_ref, ba_ref, bx_ref, lam_ref, hf_ref, gg_ref, x_ref, mod_ref,
                    wo_ref, gffn_ref, wr_ref, rb_ref, o_ref, h_ref, e0_ref, e1_ref, p0_ref, p1_ref, gt_ref, cnt_ref,
                    ubuf, halo, hcar, abuf, bbuf, hbuf, carry):
    first = _first_step()
    _lru_gates_scan(True, pl.program_id(1), u_ref, cw_ref, cb_ref, wg_ref, ba_ref, bx_ref, lam_ref,
                    ubuf, halo, hcar, abuf, bbuf, hbuf)
    y = (hf_ref[0] + hbuf[...]) * gg_ref[0]
    out = jnp.dot(y.astype(BF16), wo_ref[...], preferred_element_type=F32)
    m = mod_ref[0, 0]
    x_new = x_ref[0] + m[2:3] * out
    o_ref[0] = x_new
    _route_tile(x_new, m, gffn_ref, wr_ref, rb_ref, carry, first,
                h_ref, e0_ref, e1_ref, p0_ref, p1_ref, gt_ref, cnt_ref)


def _lru_core(x, mods, ugg, p, route_args):
    b, n, d = x.shape
    u, gg = ugg
    nt = n // TR
    nb = d // LRU_BLOCKS
    wg = jnp.concatenate([p["lru_wa"], p["lru_wx"]], axis=-1).astype(BF16)
    scratch = [pltpu.VMEM((TR + SUBLANES, d), F32), pltpu.VMEM((SUBLANES, d), F32),
               pltpu.VMEM((SUBLANES, d), F32), pltpu.VMEM((TR, d), F32), pltpu.VMEM((TR, d), F32),
               pltpu.VMEM((TR, d), F32)]

    def gate_specs(order):
        return [pl.BlockSpec((1, TR, d), lambda bb, t: (bb, order(t), 0)),
                _full_spec((LRU_CONV_W, d)), _full_spec((1, d)), _full_spec((LRU_BLOCKS, nb, 2 * nb)),
                _full_spec((1, d)), _full_spec((1, d)), _full_spec((1, d))]

    def gate_args(dd):
        return (u, p["lru_conv_w"][dd], p["lru_conv_b"][dd].reshape(1, d), wg[dd], p["lru_ba"][dd].reshape(1, d),
                p["lru_bx"][dd].reshape(1, d), p["lru_lam"][dd].reshape(1, d))

    hf = pl.pallas_call(
        _lru_fwd_kernel, grid=(b, nt),
        in_specs=gate_specs(lambda t: t),
        out_specs=_row_spec(TR, d),
        out_shape=jax.ShapeDtypeStruct((b, n, d), F32),
        scratch_shapes=scratch,
        compiler_params=_params(("arbitrary", "arbitrary")),
        name="lru_fwd",
    )(*gate_args(0))

    def rev(t):
        return jnp.where(t == 0, 0, nt - t)

    def rev_spec():
        return pl.BlockSpec((1, TR, d), lambda bb, t: (bb, rev(t), 0))

    r_in, r_out, r_shape, r_scratch = _route_io(b, n, d, TR, rev)
    return pl.pallas_call(
        _lru_bwd_kernel, grid=(b, nt),
        in_specs=gate_specs(rev) + [rev_spec(), rev_spec(), rev_spec(), _seg_mod_spec(d), _full_spec((d, d))] + r_in,
        out_specs=[rev_spec()] + r_out,
        out_shape=[jax.ShapeDtypeStruct((b, n, d), F32)] + r_shape,
        scratch_shapes=scratch + r_scratch,
        compiler_params=_params(("arbitrary", "arbitrary")),
        name="lru_bwd",
    )(*gate_args(1), hf, gg, x, mods, p["lru_w_out"].astype(BF16), *route_args)


def _dispatch_kernel(slot_ref, zrow_ref, nu_ref, h_ref, xs_ref, ring, zbuf, sem, zsem, *, n_tok, n_blocks):
    s = pl.program_id(0)
    ns = pl.num_programs(0)
    tr = h_ref.shape[0] // SUBLANES
    blk = zbuf.shape[0] // SUBLANES
    par = s % 2

    @pl.when(s == 0)
    def _():
        zbuf[...] = jnp.zeros_like(zbuf)
        zsrc = _tile_rows(zbuf, 0, 1)
        for e in range(N_EXPERTS):
            first, npad = zrow_ref[e], zrow_ref[N_EXPERTS + e]

            def zero_row(i, carry, first=first):
                pltpu.make_async_copy(zsrc, _tile_rows(xs_ref, first + i, 1), zsem).start()
                return carry

            def zero_row_wait(i, carry):
                pltpu.make_async_copy(zsrc, _tile_rows(xs_ref, 0, 1), zsem).wait()
                return carry

            lax.fori_loop(0, npad, zero_row, 0)
            lax.fori_loop(0, npad, zero_row_wait, 0)

        def zero_blk(i, carry):
            pltpu.make_async_copy(zbuf, _tile_rows(xs_ref, i * blk, blk), zsem).start()
            return carry

        def zero_blk_wait(i, carry):
            pltpu.make_async_copy(zbuf, _tile_rows(xs_ref, 0, blk), zsem).wait()
            return carry

        lax.fori_loop(nu_ref[0], n_blocks, zero_blk, 0)
        lax.fori_loop(nu_ref[0], n_blocks, zero_blk_wait, 0)

    def wait_ring(p):
        for _ in range(TOP_K):
            pltpu.make_async_copy(ring.at[p], _tile_rows(xs_ref, 0, tr), sem.at[p]).wait()

    @pl.when(s >= 2)
    def _():
        wait_ring(par)

    ring[par] = h_ref[...]
    base = s * tr

    def issue(i, carry):
        for k in range(TOP_K):
            pltpu.make_async_copy(_tile_rows(ring.at[par], i, 1),
                                  _tile_rows(xs_ref, slot_ref[k * n_tok + base + i], 1),
                                  sem.at[par]).start(priority=k)
        return carry

    lax.fori_loop(0, tr, issue, 0, unroll=8)

    @pl.when(s == ns - 1)
    def _():
        wait_ring(par)

        @pl.when(ns >= 2)
        def _():
            wait_ring(1 - par)


def _expert_kernel(be_ref, nu_ref, xs_ref, w1_ref, w3_ref, w2_ref, ys_ref, w1b, w3b, w2b):
    i = pl.program_id(0)
    used = i < nu_ref[0]
    new_expert = jnp.logical_or(i == 0, be_ref[i] != be_ref[jnp.maximum(i - 1, 0)])

    @pl.when(jnp.logical_and(used, new_expert))
    def _():
        w1b[...] = w1_ref[0, 0].astype(BF16)
        w3b[...] = w3_ref[0, 0].astype(BF16)
        w2b[...] = w2_ref[0, 0].astype(BF16)

    @pl.when(used)
    def _():
        xb = _load_token_tiles(xs_ref, xs_ref.shape[0] // SUBLANES).astype(BF16)
        a = jnp.dot(xb, w1b[...], preferred_element_type=F32)
        b = jnp.dot(xb, w3b[...], preferred_element_type=F32)
        hid = (a * jax.nn.sigmoid(a)) * b
        _store_token_tiles(ys_ref, jnp.dot(hid.astype(BF16), w2b[...], preferred_element_type=F32))

    @pl.when(jnp.logical_not(used))
    def _():
        ys_ref[...] = jnp.zeros_like(ys_ref)


def _combine_kernel(slot_ref, x_ref, mod_ref, gt_ref, ys_ref, *refs, n_tok, rows_per_batch, tile_off, tail, n_tail_in):
    if tail is None:
        nf_ref, o_ref, ybuf, sem = refs
    else:
        modn_ref, gn_ref = refs[0], refs[1]
        tail_in = refs[2:2 + n_tail_in]
        o_ref = refs[2 + n_tail_in]
        tail_out = refs[3 + n_tail_in:-2]
        ybuf, sem = refs[-2], refs[-1]
    bb, t = pl.program_id(0), pl.program_id(1)
    nt = pl.num_programs(1)
    tr = x_ref.shape[1]
    step = bb * nt + t
    nsteps = pl.num_programs(0) * nt

    def tok_base(s):
        return (s // nt) * rows_per_batch + (s % nt + tile_off) * tr

    def issue(s):
        par = s % 2
        base = tok_base(s)

        def one(i, carry):
            for k in range(TOP_K):
                pltpu.make_async_copy(_tile_rows(ys_ref, slot_ref[k * n_tok + base + i], 1),
                                      _tile_rows(ybuf.at[par, k], i, 1), sem.at[par]).start(priority=k)
            return carry

        lax.fori_loop(0, tr, one, 0, unroll=8)

    @pl.when(step == 0)
    def _():
        issue(step)

    @pl.when(step + 1 < nsteps)
    def _():
        issue(step + 1)

    par = step % 2
    for k in range(TOP_K):
        pltpu.make_async_copy(_tile_rows(ys_ref, 0, tr), ybuf.at[par, k], sem.at[par]).wait()

    gt = gt_ref[0]
    y = (gt[:, 0:1] * _load_token_tiles(ybuf.at[par, 0], tr)
         + gt[:, 1:2] * _load_token_tiles(ybuf.at[par, 1], tr))
    out = x_ref[0] + mod_ref[0, 0][5:6] * y
    if tail is None:
        o_ref[0] = out * lax.rsqrt(jnp.mean(out * out, axis=-1, keepdims=True) + EPS) * nf_ref[...]
    else:
        o_ref[0] = out
        mn = modn_ref[0, 0]
        tail(_norm_mod(out, gn_ref[...], mn[0:1], mn[1:2]), *tail_in, *tail_out)


def _moe(x, mods, routed, layer, w1, w3, w2, ctx_len, norm_f=None, next_ip=None, next_mods=None, next_g=None):
    b, n, d = x.shape
    nt = n // TR
    n_tok = b * n
    de = w1.shape[-1]
    h, e0, e1, p0, p1, gtab, counts = routed

    counts = counts[:, 0]
    padded = (counts + MOE_BLK - 1) // MOE_BLK * MOE_BLK
    pend = jnp.cumsum(padded)
    pstart = (pend - padded).astype(I32)
    n_slots = n_tok * TOP_K + N_EXPERTS * MOE_BLK
    n_blocks = n_slots // MOE_BLK
    n_used = (pend[-1] // MOE_BLK).astype(I32).reshape(1)
    blk_start = jnp.arange(n_blocks, dtype=I32) * MOE_BLK
    blk_e = jnp.minimum(jnp.sum((pend[None, :] <= blk_start[:, None]).astype(I32), axis=1), N_EXPERTS - 1)

    def slot_of(e, p):
        start = jnp.zeros_like(e)
        for j in range(N_EXPERTS):
            start = jnp.where(e == j, pstart[j], start)
        return (start + p).reshape(n_tok)

    slots = jnp.concatenate([slot_of(e0, p0), slot_of(e1, p1)])
    zrow = jnp.concatenate([pstart + counts, padded - counts]).astype(I32)

    xs = pl.pallas_call(
        functools.partial(_dispatch_kernel, n_tok=n_tok, n_blocks=n_blocks),
        grid_spec=pltpu.PrefetchScalarGridSpec(
            num_scalar_prefetch=3, grid=(n_tok // TR,),
            in_specs=[pl.BlockSpec((TR * SUBLANES, LANES), lambda s, sl, zr, nu: (s, 0))],
            out_specs=pl.BlockSpec(memory_space=pl.ANY),
            scratch_shapes=[pltpu.VMEM((2, TR * SUBLANES, LANES), F32), pltpu.VMEM((MOE_BLK * SUBLANES, LANES), F32),
                            pltpu.SemaphoreType.DMA((2,)), pltpu.SemaphoreType.DMA]),
        out_shape=jax.ShapeDtypeStruct((n_slots * SUBLANES, LANES), F32),
        compiler_params=_params(("arbitrary",)),
        name="moe_dispatch",
    )(slots, zrow, n_used, h)

    def wspec(shape):
        return pl.BlockSpec((1, 1) + shape, lambda i, be, nu: (layer, be[i], 0, 0))

    def blk_spec():
        return pl.BlockSpec((MOE_BLK * SUBLANES, LANES), lambda i, be, nu: (i, 0))

    ys = pl.pallas_call(
        _expert_kernel,
        grid_spec=pltpu.PrefetchScalarGridSpec(
            num_scalar_prefetch=2, grid=(n_blocks,),
            in_specs=[blk_spec(), wspec((d, de)), wspec((d, de)), wspec((de, d))],
            out_specs=blk_spec(),
            scratch_shapes=[pltpu.VMEM((d, de), BF16), pltpu.VMEM((d, de), BF16), pltpu.VMEM((de, d), BF16)]),
        out_shape=jax.ShapeDtypeStruct((n_slots * SUBLANES, LANES), F32),
        compiler_params=_params(("arbitrary",)),
        name="moe_experts",
    )(blk_e, n_used, xs, w1, w3, w2)

    final = next_ip is None
    tile_off = ctx_len // TR if final else 0
    nt_out = nt - tile_off
    common_specs = [_row_spec(TR, d, tile_off), _seg_mod_spec(d, tile_off=tile_off), _row_spec(TR, LANES, tile_off),
                    pl.BlockSpec(memory_space=pl.ANY)]
    scratch = [pltpu.VMEM((2, TOP_K, TR * SUBLANES, LANES), F32), pltpu.SemaphoreType.DMA((2,))]
    stream_spec = pl.BlockSpec((1, TR, d), lambda bb, t, s: (bb, t, 0))
    stream_shape = jax.ShapeDtypeStruct((b, nt_out * TR, d), F32)
    if final:
        return pl.pallas_call(
            functools.partial(_combine_kernel, n_tok=n_tok, rows_per_batch=n, tile_off=tile_off, tail=None,
                              n_tail_in=0),
            grid_spec=pltpu.PrefetchScalarGridSpec(
                num_scalar_prefetch=1, grid=(b, nt_out),
                in_specs=common_specs + [_full_spec((1, d))],
                out_specs=stream_spec, scratch_shapes=scratch),
            out_shape=stream_shape,
            compiler_params=_params(("arbitrary", "arbitrary")),
            name="moe_combine_final",
        )(slots, x, mods, gtab, ys, norm_f.reshape(1, d))
    outs = pl.pallas_call(
        functools.partial(_combine_kernel, n_tok=n_tok, rows_per_batch=n, tile_off=tile_off, tail=next_ip["tail"],
                          n_tail_in=len(next_ip["args"])),
        grid_spec=pltpu.PrefetchScalarGridSpec(
            num_scalar_prefetch=1, grid=(b, nt_out),
            in_specs=common_specs + [_seg_mod_spec(d), _full_spec((1, d))] + next_ip["in_specs"],
            out_specs=[stream_spec] + next_ip["out_specs"], scratch_shapes=scratch),
        out_shape=[stream_shape] + next_ip["out_shape"],
        compiler_params=_params(("arbitrary", "arbitrary")),
        name="moe_combine_in_proj",
    )(slots, x, mods, gtab, ys, next_mods, next_g.reshape(1, d), *next_ip["args"])
    return outs[0], outs[1:]


def kernel(x, c, ctx, c_ctx, w_mod, b_mod, norm_mix, norm_ffn, norm_f, conv_pw1, conv_pw1_b, conv_dw, conv_dw_b, conv_ln_g, conv_ln_b, conv_pw2, conv_pw2_b, attn_w_qkv, attn_w_o, attn_sink, lru_w_in, lru_conv_w, lru_conv_b, lru_wa, lru_ba, lru_wx, lru_bx, lru_lam, lru_w_out, moe_w_router, moe_router_bias, moe_w1, moe_w3, moe_w2):
    b, s, d = x.shape
    ctx_len = ctx.shape[1]
    depth = w_mod.shape[0]
    n = ctx_len + s
    assert ctx_len == TR and s % TR == 0 and d == SUBLANES * LANES
    assert moe_w_router.shape[1] == N_EXPERTS

    mods = _modulation(c, c_ctx, w_mod, b_mod)
    xs = jnp.concatenate([ctx, x], axis=1)

    wr = jnp.zeros((EXPERTS_PER_GROUP, SUBLANES, d), F32).at[:, :N_GROUPS].set(
        moe_w_router.T.reshape(N_GROUPS, EXPERTS_PER_GROUP, d).swapaxes(0, 1))
    wr = wr.reshape(ROUTER_ROWS, d).astype(BF16)
    rb = jnp.zeros((EXPERTS_PER_GROUP, SUBLANES), F32).at[:, :N_GROUPS].set(
        moe_router_bias.astype(F32).reshape(N_GROUPS, EXPERTS_PER_GROUP).T).reshape(-1, 1)

    def layer_params(i):
        slot = i // N_MIXERS
        return [dict(conv_pw1=conv_pw1, conv_pw1_b=conv_pw1_b, conv_dw=conv_dw, conv_dw_b=conv_dw_b,
                     conv_ln_g=conv_ln_g, conv_ln_b=conv_ln_b, conv_pw2=conv_pw2, conv_pw2_b=conv_pw2_b),
                dict(attn_w_qkv=attn_w_qkv, attn_w_o=attn_w_o, attn_sink=attn_sink),
                dict(lru_w_in=lru_w_in, lru_conv_w=lru_conv_w, lru_conv_b=lru_conv_b, lru_wa=lru_wa, lru_ba=lru_ba,
                     lru_wx=lru_wx, lru_bx=lru_bx, lru_lam=lru_lam, lru_w_out=lru_w_out)][i % N_MIXERS], slot

    def params_of(i):
        group, slot = layer_params(i)
        return {k: v[slot] for k, v in group.items()}

    p = params_of(0)
    ip = _in_proj(0, p, b, n, d, ctx_len)
    a_out = _in_proj_standalone(ip, xs, mods[0], norm_mix[0])
    if not isinstance(a_out, (list, tuple)):
        a_out = [a_out]
    for i in range(depth):
        kind = i % N_MIXERS
        route_args = (norm_ffn[i].reshape(1, d), wr, rb)
        if kind == 0:
            res = _conv_core(xs, mods[i], a_out[0], p, route_args)
        elif kind == 1:
            res = _attn_core(xs, mods[i], a_out, p, route_args, ctx_len)
        else:
            res = _lru_core(xs, mods[i], a_out, p, route_args)
        x_new, routed = res[0], res[1:]
        if i == depth - 1:
            return _moe(x_new, mods[i], routed, i, moe_w1, moe_w3, moe_w2, ctx_len, norm_f=norm_f)
        p = params_of(i + 1)
        ip = _in_proj((i + 1) % N_MIXERS, p, b, n, d, ctx_len)
        xs, a_out = _moe(x_new, mods[i], routed, i, moe_w1, moe_w3, moe_w2, ctx_len,
                         next_ip=ip, next_mods=mods[i + 1], next_g=norm_mix[i + 1])
```

```python
import functools

import jax
import jax.numpy as jnp
from jax import lax
from jax.experimental import pallas as pl
from jax.experimental.pallas import tpu as pltpu

F32 = jnp.float32
BF16 = jnp.bfloat16
I32 = jnp.int32

EPS = 1e-6
N_MOD = 6
N_MIXERS = 3
GRID_W = 64
CONV_WIDTH = 31
CONV_HALF = (CONV_WIDTH - 1) // 2
HEAD_DIM = 64
N_KV_HEADS = 4
GQA_GROUP = 4
WINDOW = 128
ROPE_BASE = 10000.0
ROPE_PAIRS = HEAD_DIM // 4
LRU_BLOCKS = 8
LRU_CONV_W = 4
LRU_C = 8.0
N_EXPERTS = 16
N_GROUPS = 4
EXPERTS_PER_GROUP = 4
TOP_K = 2

LANES = 128
SUBLANES = 8
TR = 256
TQ = 128
HALO = 16
MOE_BLK = 512
ROUTER_ROWS = SUBLANES * EXPERTS_PER_GROUP
NEG = -1e30
VMEM_LIMIT = 56 * 1024 * 1024


def _params(sem, vmem=VMEM_LIMIT):
    return pltpu.CompilerParams(dimension_semantics=sem, vmem_limit_bytes=vmem)


def _norm_mod(x, g, shift, scale):
    y = x * lax.rsqrt(jnp.mean(x * x, axis=-1, keepdims=True) + EPS)
    return (y * g) * (1.0 + scale) + shift


def _seg_mod_spec(d, tiles_per_seg=1, tile_off=0):
    return pl.BlockSpec((1, 1, N_MOD, d),
                        lambda b, t, *_: (b, jnp.minimum((t + tile_off) // tiles_per_seg, 1), 0, 0))


def _row_spec(tr, d, tile_off=0):
    return pl.BlockSpec((1, tr, d), lambda b, t, *_: (b, t + tile_off, 0))


def _full_spec(shape):
    nd = len(shape)
    return pl.BlockSpec(shape, lambda b, t, *_: (0,) * nd)


def _mod_kernel(c_ref, w_ref, b_ref, o_ref):
    c = c_ref[...]
    sc = c * jax.nn.sigmoid(c)
    o_ref[0] = jnp.dot(sc.astype(BF16), w_ref[0].astype(BF16), preferred_element_type=F32) + b_ref[0]


def _modulation(c, c_ctx, w_mod, b_mod):
    depth, d, nout = w_mod.shape
    b = c.shape[0]
    assert b + 1 <= SUBLANES
    rows = jnp.zeros((SUBLANES, d), F32).at[:b].set(c).at[b].set(c_ctx)
    tn = 1536
    raw = pl.pallas_call(
        _mod_kernel,
        grid=(depth, nout // tn),
        in_specs=[pl.BlockSpec((SUBLANES, d), lambda i, j: (0, 0)),
                  pl.BlockSpec((1, d, tn), lambda i, j: (i, 0, j)),
                  pl.BlockSpec((1, 1, tn), lambda i, j: (i, 0, j))],
        out_specs=pl.BlockSpec((1, SUBLANES, tn), lambda i, j: (i, 0, j)),
        out_shape=jax.ShapeDtypeStruct((depth, SUBLANES, nout), F32),
        compiler_params=_params(("arbitrary", "arbitrary")),
        name="modulation",
    )(rows, w_mod, b_mod.reshape(depth, 1, nout))
    raw = raw.reshape(depth, SUBLANES, N_MOD, d)
    lat = raw[:, :b]
    ctx = jnp.broadcast_to(raw[:, b][:, None], lat.shape)
    return jnp.stack([ctx, lat], axis=2)


def _store_token_tiles(ref, mat):
    rows = mat.shape[0]
    for s in range(SUBLANES):
        ref[pl.ds(s, rows, stride=SUBLANES), :] = mat[:, s * LANES:(s + 1) * LANES]


def _load_token_tiles(ref, rows):
    return jnp.concatenate([ref[pl.ds(s, rows, stride=SUBLANES), :] for s in range(SUBLANES)], axis=1)


def _tile_rows(ref, first_token, n_tokens):
    return ref.at[pl.ds(pl.multiple_of(first_token * SUBLANES, SUBLANES), n_tokens * SUBLANES)]


def _route_tile(x, m, g_ref, wr_ref, rb_ref, carry, first, h_ref, e0_ref, e1_ref, p0_ref, p1_ref, gt_ref, cnt_ref):
    tr = x.shape[0]

    @pl.when(first)
    def _():
        carry[...] = jnp.zeros_like(carry)

    h = _norm_mod(x, g_ref[...], m[3:4], m[4:5])
    _store_token_tiles(h_ref, h)
    logit = lax.dot_general(wr_ref[...], h.astype(BF16), (((1,), (1,)), ((), ())),
                            preferred_element_type=F32)
    s = jax.nn.sigmoid(logit)
    sel = s + rb_ref[...]
    sj = [s[SUBLANES * j:SUBLANES * (j + 1)] for j in range(EXPERTS_PER_GROUP)]
    cj = [sel[SUBLANES * j:SUBLANES * (j + 1)] for j in range(EXPERTS_PER_GROUP)]

    hi1, lo1 = jnp.maximum(cj[0], cj[1]), jnp.minimum(cj[0], cj[1])
    hi2, lo2 = jnp.maximum(cj[2], cj[3]), jnp.minimum(cj[2], cj[3])
    top1 = jnp.maximum(hi1, hi2)
    top2 = jnp.maximum(jnp.minimum(hi1, hi2), jnp.maximum(lo1, lo2))
    row = lax.broadcasted_iota(I32, (SUBLANES, tr), 0)
    gscore = jnp.where(row < N_GROUPS, top1 + top2, -jnp.inf)
    gmax = jnp.max(gscore, axis=0, keepdims=True)
    grp = jnp.min(jnp.where(gscore == gmax, row, SUBLANES), axis=0, keepdims=True)
    pick = row == grp
    c = [jnp.sum(jnp.where(pick, v, 0.0), axis=0, keepdims=True) for v in cj]
    w = [jnp.sum(jnp.where(pick, v, 0.0), axis=0, keepdims=True) for v in sj]

    def argtop(vals):
        best, bi, bw = vals[0], jnp.zeros_like(grp), w[0]
        for j in range(1, EXPERTS_PER_GROUP):
            better = vals[j] > best
            best = jnp.where(better, vals[j], best)
            bi = jnp.where(better, j, bi)
            bw = jnp.where(better, w[j], bw)
        return bi, bw

    i0, w0 = argtop(c)
    i1, w1 = argtop([jnp.where(i0 == j, -jnp.inf, c[j]) for j in range(EXPERTS_PER_GROUP)])
    e0 = grp * EXPERTS_PER_GROUP + i0
    e1 = grp * EXPERTS_PER_GROUP + i1
    wsum = w0 + w1
    g0 = w0 / wsum
    g1 = w1 / wsum

    erow = lax.broadcasted_iota(I32, (N_EXPERTS, tr), 0)
    member = jnp.logical_or(erow == e0, erow == e1)
    upper = (lax.broadcasted_iota(I32, (tr, tr), 0) < lax.broadcasted_iota(I32, (tr, tr), 1))
    before = jnp.dot(member.astype(BF16), upper.astype(BF16), preferred_element_type=F32) + carry[...]
    p0 = jnp.sum(jnp.where(erow == e0, before, 0.0), axis=0, keepdims=True)
    p1 = jnp.sum(jnp.where(erow == e1, before, 0.0), axis=0, keepdims=True)
    total = carry[...] + jnp.sum(member.astype(F32), axis=1, keepdims=True)
    carry[...] = total
    cnt_ref[...] = total.astype(I32)

    e0_ref[0] = e0
    e1_ref[0] = e1
    p0_ref[0] = p0.astype(I32)
    p1_ref[0] = p1.astype(I32)
    grow = lax.broadcasted_iota(I32, (LANES, tr), 0)
    gmat = jnp.where(grow == 0, g0, jnp.where(grow == 1, g1, 0.0))
    gt_ref[0] = gmat.T


def _route_io(b, n, d, tile, tile_of):
    nt = n // tile

    def flat(bb, t):
        return bb * nt + tile_of(t)

    lane_major = pl.BlockSpec((1, 1, tile), lambda bb, t, *_: (flat(bb, t), 0, 0))
    lane_major_shape = jax.ShapeDtypeStruct((b * nt, 1, tile), I32)
    in_specs = [_full_spec((1, d)), _full_spec((ROUTER_ROWS, d)), _full_spec((ROUTER_ROWS, 1))]
    out_specs = [pl.BlockSpec((tile * SUBLANES, LANES), lambda bb, t, *_: (flat(bb, t), 0)),
                 lane_major, lane_major, lane_major, lane_major,
                 pl.BlockSpec((1, tile, LANES), lambda bb, t, *_: (bb, tile_of(t), 0)),
                 pl.BlockSpec((N_EXPERTS, 1), lambda bb, t, *_: (0, 0))]
    out_shape = [jax.ShapeDtypeStruct((b * n * SUBLANES, LANES), F32), lane_major_shape, lane_major_shape,
                 lane_major_shape, lane_major_shape,
                 jax.ShapeDtypeStruct((b, n, LANES), F32), jax.ShapeDtypeStruct((N_EXPERTS, 1), I32)]
    scratch = [pltpu.VMEM((N_EXPERTS, 1), F32)]
    return in_specs, out_specs, out_shape, scratch


def _first_step():
    return jnp.logical_and(pl.program_id(0) == 0, pl.program_id(1) == 0)


def _glu_tail(h, w_ref, b_ref, o_ref):
    d = h.shape[-1]
    u = jnp.dot(h.astype(BF16), w_ref[...], preferred_element_type=F32) + b_ref[...]
    o_ref[0] = u[:, :d] * jax.nn.sigmoid(u[:, d:])


def _qkv_tail(h, w_ref, cos_ref, sin_ref, q_ref, k_ref, v_ref):
    d = h.shape[-1]
    nk = k_ref.shape[-1]
    t = jnp.dot(h.astype(BF16), w_ref[...], preferred_element_type=F32)
    cos = cos_ref[...]
    sin = sin_ref[...]
    lane = lax.broadcasted_iota(I32, cos.shape, 1)
    first_half = (lane % (2 * ROPE_PAIRS)) < ROPE_PAIRS

    def rope(xg):
        partner = jnp.where(first_half, pltpu.roll(xg, LANES - ROPE_PAIRS, 1), pltpu.roll(xg, ROPE_PAIRS, 1))
        return xg * cos + partner * sin

    for j in range(d // LANES):
        cs = slice(j * LANES, (j + 1) * LANES)
        q_ref[0, :, cs] = rope(t[:, cs]).astype(BF16)
    for j in range(nk // LANES):
        cs = slice(j * LANES, (j + 1) * LANES)
        k_ref[0, :, cs] = rope(t[:, d + j * LANES:d + (j + 1) * LANES]).astype(BF16)
    v_ref[0] = t[:, d + nk:].astype(BF16)


def _lru_in_tail(h, w_ref, u_ref, gg_ref):
    d = h.shape[-1]
    t = jnp.dot(h.astype(BF16), w_ref[...], preferred_element_type=F32)
    u_ref[0] = t[:, :d]
    gg_ref[0] = jax.nn.gelu(t[:, d:])


def _rope_tables(n, ctx_len):
    pos = jnp.arange(n - ctx_len, dtype=I32)
    inv = ROPE_BASE ** (-jnp.arange(ROPE_PAIRS, dtype=F32) / ROPE_PAIRS)
    ar = (pos // GRID_W).astype(F32)[:, None] * inv
    ac = (pos % GRID_W).astype(F32)[:, None] * inv
    ang = jnp.concatenate([ar, ar, ac, ac], axis=-1)
    sign = jnp.tile(jnp.concatenate([-jnp.ones((ROPE_PAIRS,), F32), jnp.ones((ROPE_PAIRS,), F32)]), 2)
    cos = jnp.concatenate([jnp.ones((ctx_len, HEAD_DIM), F32), jnp.cos(ang)], axis=0)
    sin = jnp.concatenate([jnp.zeros((ctx_len, HEAD_DIM), F32), jnp.sin(ang) * sign], axis=0)
    return jnp.tile(cos, (1, 2)), jnp.tile(sin, (1, 2))


def _in_proj(kind, p, b, n, d, ctx_len):
    if kind == 0:
        args = (p["conv_pw1"].astype(BF16), p["conv_pw1_b"].reshape(1, 2 * d))
        return dict(tail=_glu_tail, args=args, in_specs=[_full_spec((d, 2 * d)), _full_spec((1, 2 * d))],
                    out_specs=[_row_spec(TR, d)], out_shape=[jax.ShapeDtypeStruct((b, n, d), F32)])
    if kind == 1:
        kvd = N_KV_HEADS * HEAD_DIM
        nk = N_KV_HEADS * LANES
        assert d == 2 * nk
        scale = HEAD_DIM ** -0.5
        w_qkv = p["attn_w_qkv"]

        def dup(w):
            w = w.reshape(d, N_KV_HEADS, HEAD_DIM)
            return jnp.concatenate([w, w], axis=-1).reshape(d, nk)

        w_all = jnp.concatenate([w_qkv[:, :d] * scale, dup(w_qkv[:, d:d + kvd]), dup(w_qkv[:, d + kvd:])],
                                axis=1).astype(BF16)
        cos, sin = _rope_tables(n, ctx_len)
        table = pl.BlockSpec((TR, LANES), lambda bb, t, *_: (t, 0))
        return dict(tail=_qkv_tail, args=(w_all, cos, sin), in_specs=[_full_spec((d, d + 2 * nk)), table, table],
                    out_specs=[_row_spec(TR, d), _row_spec(TR, nk), _row_spec(TR, nk)],
                    out_shape=[jax.ShapeDtypeStruct((b, n, d), BF16), jax.ShapeDtypeStruct((b, n, nk), BF16),
                               jax.ShapeDtypeStruct((b, n, nk), BF16)])
    return dict(tail=_lru_in_tail, args=(p["lru_w_in"].astype(BF16),), in_specs=[_full_spec((d, 2 * d))],
                out_specs=[_row_spec(TR, d), _row_spec(TR, d)],
                out_shape=[jax.ShapeDtypeStruct((b, n, d), F32), jax.ShapeDtypeStruct((b, n, d), F32)])


def _in_proj_kernel(x_ref, mod_ref, g_ref, *refs, tail):
    m = mod_ref[0, 0]
    tail(_norm_mod(x_ref[0], g_ref[...], m[0:1], m[1:2]), *refs)


def _in_proj_standalone(ip, x, mods, g):
    b, n, d = x.shape
    return pl.pallas_call(
        functools.partial(_in_proj_kernel, tail=ip["tail"]), grid=(b, n // TR),
        in_specs=[_row_spec(TR, d), _seg_mod_spec(d), _full_spec((1, d))] + ip["in_specs"],
        out_specs=ip["out_specs"], out_shape=ip["out_shape"],
        compiler_params=_params(("parallel", "parallel")),
        name="in_proj",
    )(x, mods, g.reshape(1, d), *ip["args"])


def _conv_core_kernel(x_ref, mod_ref, gp_ref, gc_ref, gn_ref, dw_ref, dwb_ref, lng_ref, lnb_ref, w_ref, b_ref,
                      gffn_ref, wr_ref, rb_ref, o_ref, h_ref, e0_ref, e1_ref, p0_ref, p1_ref, gt_ref, cnt_ref,
                      gbuf, cbuf, carry):
    t = pl.program_id(1)
    nt = pl.num_programs(1)
    first = _first_step()
    tr, d = gc_ref.shape[1], gc_ref.shape[2]
    seg_first = t <= 1
    seg_last = jnp.logical_or(t == 0, t == nt - 1)
    gbuf[0, 0:HALO, :] = jnp.where(seg_first, 0.0, gp_ref[0])
    gbuf[0, HALO:HALO + tr, :] = gc_ref[0]
    gbuf[0, HALO + tr:HALO + tr + HALO, :] = jnp.where(seg_last, 0.0, gn_ref[0])
    span = tr + 2 * HALO - SUBLANES
    for r in range(1, SUBLANES):
        gbuf[r, 0:span, :] = gbuf[0, r:r + span, :]

    rc = 32
    base = HALO - CONV_HALF

    def chunk(i, carry_):
        r0 = pl.multiple_of(i * rc, rc)
        for c in range(d // LANES):
            cs = slice(c * LANES, (c + 1) * LANES)
            acc = jnp.zeros((rc, LANES), F32)
            for k in range(CONV_WIDTH):
                off = base + k
                rows = pl.ds(r0 + (off // SUBLANES) * SUBLANES, rc)
                acc = acc + dw_ref[k:k + 1, cs] * gbuf[off % SUBLANES, rows, cs]
            cbuf[pl.ds(r0, rc), cs] = acc
        return carry_

    lax.fori_loop(0, tr // rc, chunk, 0)

    u = cbuf[...] + dwb_ref[...]
    mu = jnp.mean(u, axis=-1, keepdims=True)
    uc = u - mu
    var = jnp.mean(uc * uc, axis=-1, keepdims=True)
    v = uc * lax.rsqrt(var + EPS) * lng_ref[...] + lnb_ref[...]
    v = v * jax.nn.sigmoid(v)
    y = jnp.dot(v.astype(BF16), w_ref[...], preferred_element_type=F32) + b_ref[...]
    m = mod_ref[0, 0]
    x_new = x_ref[0] + m[2:3] * y
    o_ref[0] = x_new
    _route_tile(x_new, m, gffn_ref, wr_ref, rb_ref, carry, first,
                h_ref, e0_ref, e1_ref, p0_ref, p1_ref, gt_ref, cnt_ref)


def _conv_core(x, mods, glu, p, route_args):
    b, n, d = x.shape
    nt = n // TR
    hb = TR // HALO
    nh = n // HALO
    dwp = jnp.zeros((32, d), F32).at[:CONV_WIDTH].set(p["conv_dw"])
    r_in, r_out, r_shape, r_scratch = _route_io(b, n, d, TR, lambda t: t)
    return pl.pallas_call(
        _conv_core_kernel, grid=(b, nt),
        in_specs=[_row_spec(TR, d), _seg_mod_spec(d),
                  pl.BlockSpec((1, HALO, d), lambda bb, t: (bb, jnp.maximum(t * hb - 1, 0), 0)),
                  _row_spec(TR, d),
                  pl.BlockSpec((1, HALO, d), lambda bb, t: (bb, jnp.minimum((t + 1) * hb, nh - 1), 0)),
                  _full_spec((32, d)), _full_spec((1, d)), _full_spec((1, d)), _full_spec((1, d)),
                  _full_spec((d, d)), _full_spec((1, d))] + r_in,
        out_specs=[_row_spec(TR, d)] + r_out,
        out_shape=[jax.ShapeDtypeStruct((b, n, d), F32)] + r_shape,
        scratch_shapes=[pltpu.VMEM((SUBLANES, TR + 2 * HALO, d), F32), pltpu.VMEM((TR, d), F32)] + r_scratch,
        compiler_params=_params(("arbitrary", "arbitrary")),
        name="conv_core",
    )(x, mods, glu, glu, glu, dwp, p["conv_dw_b"].reshape(1, d), p["conv_ln_g"].reshape(1, d),
      p["conv_ln_b"].reshape(1, d), p["conv_pw2"].astype(BF16), p["conv_pw2_b"].reshape(1, d), *route_args)


def _attn_core_kernel(sink_ref, x_ref, mod_ref, q_ref, kc_ref, vc_ref, k0_ref, k1_ref, k2_ref,
                      v0_ref, v1_ref, v2_ref, wo_ref, gffn_ref, wr_ref, rb_ref,
                      o_ref, h_ref, e0_ref, e1_ref, p0_ref, p1_ref, gt_ref, cnt_ref, carry, *, ctx_len, n_rows):
    t = pl.program_id(1)
    first = _first_step()
    tq = q_ref.shape[1]
    nctx = kc_ref.shape[1]
    nkeys = nctx + 3 * tq

    rq = t * tq + lax.broadcasted_iota(I32, (tq, nkeys), 0)
    col = lax.broadcasted_iota(I32, (tq, nkeys), 1)
    rk = (t - 1) * tq + (col - nctx)
    win_ok = (rk >= ctx_len) & (rk < n_rows) & (jnp.abs(rq - rk) <= WINDOW) & (t * tq >= ctx_len)
    bias = jnp.where((col < nctx) | win_ok, 0.0, NEG).astype(F32)

    lane = lax.broadcasted_iota(I32, (tq, LANES), 1)
    lo = lane < HEAD_DIM

    out_cols = []
    for g in range(N_KV_HEADS):
        gs = slice(g * LANES, (g + 1) * LANES)
        kg = jnp.concatenate([kc_ref[0, :, gs], k0_ref[0, :, gs], k1_ref[0, :, gs], k2_ref[0, :, gs]], axis=0)
        vg = jnp.concatenate([vc_ref[0, :, gs], v0_ref[0, :, gs], v1_ref[0, :, gs], v2_ref[0, :, gs]], axis=0)
        qa = q_ref[0, :, (2 * g) * LANES:(2 * g + 1) * LANES]
        qb = q_ref[0, :, (2 * g + 1) * LANES:(2 * g + 2) * LANES]
        zero = jnp.zeros_like(qa)
        q4 = jnp.concatenate([jnp.where(lo, qa, zero), jnp.where(lo, zero, qa),
                              jnp.where(lo, qb, zero), jnp.where(lo, zero, qb)], axis=0)
        s = lax.dot_general(q4, kg, (((1,), (1,)), ((), ())), preferred_element_type=F32)
        ps, dens = [], []
        for hh in range(GQA_GROUP):
            sk = sink_ref[g * GQA_GROUP + hh]
            sh = s[hh * tq:(hh + 1) * tq] + bias
            mx = jnp.maximum(jnp.max(sh, axis=-1, keepdims=True), sk)
            pr = jnp.exp(sh - mx)
            dens.append(jnp.sum(pr, axis=-1, keepdims=True) + jnp.exp(sk - mx))
            ps.append(pr.astype(BF16))
        pv = jnp.dot(jnp.concatenate(ps, axis=0), vg, preferred_element_type=F32)
        oh = [pv[hh * tq:(hh + 1) * tq] / dens[hh] for hh in range(GQA_GROUP)]
        out_cols.append(jnp.where(lo, oh[0], oh[1]))
        out_cols.append(jnp.where(lo, oh[2], oh[3]))
    o = jnp.concatenate(out_cols, axis=1).astype(BF16)
    y = jnp.dot(o, wo_ref[...], preferred_element_type=F32)
    m = mod_ref[0, 0]
    x_new = x_ref[0] + m[2:3] * y
    o_ref[0] = x_new
    _route_tile(x_new, m, gffn_ref, wr_ref, rb_ref, carry, first,
                h_ref, e0_ref, e1_ref, p0_ref, p1_ref, gt_ref, cnt_ref)


def _attn_core(x, mods, qkv, p, route_args, ctx_len):
    b, n, d = x.shape
    q, k, v = qkv
    nk = k.shape[-1]
    assert ctx_len == TR
    ntq = n // TQ

    def win(off):
        return pl.BlockSpec((1, TQ, nk), lambda bb, t, s: (bb, jnp.clip(t + off, 0, ntq - 1), 0))

    ctx_spec = pl.BlockSpec((1, ctx_len, nk), lambda bb, t, s: (bb, 0, 0))
    r_in, r_out, r_shape, r_scratch = _route_io(b, n, d, TQ, lambda t: t)
    grid_spec = pltpu.PrefetchScalarGridSpec(
        num_scalar_prefetch=1, grid=(b, ntq),
        in_specs=[_row_spec(TQ, d), _seg_mod_spec(d, tiles_per_seg=TR // TQ), _row_spec(TQ, d),
                  ctx_spec, ctx_spec, win(-1), win(0), win(1), win(-1), win(0), win(1),
                  _full_spec((d, d))] + r_in,
        out_specs=[_row_spec(TQ, d)] + r_out,
        scratch_shapes=r_scratch)
    return pl.pallas_call(
        functools.partial(_attn_core_kernel, ctx_len=ctx_len, n_rows=n),
        grid_spec=grid_spec,
        out_shape=[jax.ShapeDtypeStruct((b, n, d), F32)] + r_shape,
        compiler_params=_params(("arbitrary", "arbitrary")),
        name="attn_core",
    )(p["attn_sink"].astype(F32), x, mods, q, k, v, k, k, k, v, v, v, p["attn_w_o"].astype(BF16), *route_args)


def _lru_gates_scan(back, t, u_ref, cw_ref, cb_ref, wg_ref, ba_ref, bx_ref, lam_ref,
                    ubuf, halo, hcar, abuf, bbuf, hbuf):
    tr, d = u_ref.shape[1], u_ref.shape[2]
    nb = d // LRU_BLOCKS

    @pl.when(t <= 1)
    def _():
        halo[...] = jnp.zeros_like(halo)

    @pl.when(t == 0)
    def _():
        hcar[...] = jnp.zeros_like(hcar)

    u = u_ref[0]
    if not back:
        ubuf[0:SUBLANES, :] = halo[...]
        ubuf[SUBLANES:SUBLANES + tr, :] = u
        halo[...] = u[tr - SUBLANES:, :]
        taps = [ubuf[SUBLANES - (LRU_CONV_W - 1) + k:SUBLANES - (LRU_CONV_W - 1) + k + tr, :]
                for k in range(LRU_CONV_W)]
    else:
        ubuf[0:tr, :] = u
        ubuf[tr:tr + SUBLANES, :] = halo[...]
        halo[...] = u[:SUBLANES, :]
        taps = [ubuf[(LRU_CONV_W - 1) - k:(LRU_CONV_W - 1) - k + tr, :] for k in range(LRU_CONV_W)]
    cc = cb_ref[...] + taps[0] * cw_ref[0:1, :]
    for k in range(1, LRU_CONV_W):
        cc = cc + taps[k] * cw_ref[k:k + 1, :]

    ccb = cc.astype(BF16)
    lam = lam_ref[...]
    neg_c_softplus = -LRU_C * (jnp.maximum(-lam, 0.0) + jnp.log(1.0 + jnp.exp(-jnp.abs(lam))))
    for blk in range(LRU_BLOCKS):
        cs = slice(blk * nb, (blk + 1) * nb)
        z = jnp.dot(ccb[:, cs], wg_ref[blk], preferred_element_type=F32)
        r = jax.nn.sigmoid(z[:, :nb] + ba_ref[:, cs])
        gi = jax.nn.sigmoid(z[:, nb:] + bx_ref[:, cs])
        log_a = neg_c_softplus[:, cs] * r
        a = jnp.exp(log_a)
        abuf[:, cs] = a
        bbuf[:, cs] = jnp.sqrt(1.0 - a * a) * (gi * cc[:, cs])

    row = lax.broadcasted_iota(I32, (SUBLANES, d), 0)
    nchunk = tr // SUBLANES

    def chunk(i, h):
        ci = (nchunk - 1 - i) if back else i
        r0 = pl.multiple_of(ci * SUBLANES, SUBLANES)
        a = abuf[pl.ds(r0, SUBLANES), :]
        bb = bbuf[pl.ds(r0, SUBLANES), :]
        for k in (1, 2, 4):
            if back:
                sh, ok = SUBLANES - k, row < SUBLANES - k
            else:
                sh, ok = k, row >= k
            a_s = pltpu.roll(a, sh, 0)
            b_s = pltpu.roll(bb, sh, 0)
            bb = jnp.where(ok, a * b_s + bb, bb)
            a = jnp.where(ok, a * a_s, a)
        hh = a * h + bb
        hbuf[pl.ds(r0, SUBLANES), :] = hh
        return hh[0:1, :] if back else hh[SUBLANES - 1:SUBLANES, :]

    hcar[0:1, :] = lax.fori_loop(0, nchunk, chunk, hcar[0:1, :])


def _lru_fwd_kernel(u_ref, cw_ref, cb_ref, wg_ref, ba_ref, bx_ref, lam_ref, hf_ref,
                    ubuf, halo, hcar, abuf, bbuf, hbuf):
    _lru_gates_scan(False, pl.program_id(1), u_ref, cw_ref, cb_ref, wg_ref, ba_ref, bx_ref, lam_ref,
                    ubuf, halo, hcar, abuf, bbuf, hbuf)
    hf_ref[0] = hbuf[...]


def _lru_bwd_kernel(u_ref, cw_ref, cb_ref, wg_ref, ba_ref, bx_ref, lam_ref, hf_ref, gg_ref, x_ref, mod_ref,
                    wo_ref, gffn_ref, wr_ref, rb_ref, o_ref, h_ref, e0_ref, e1_ref, p0_ref, p1_ref, gt_ref, cnt_ref,
                    ubuf, halo, hcar, abuf, bbuf, hbuf, carry):
    first = _first_step()
    _lru_gates_scan(True, pl.program_id(1), u_ref, cw_ref, cb_ref, wg_ref, ba_ref, bx_ref, lam_ref,
                    ubuf, halo, hcar, abuf, bbuf, hbuf)
    y = (hf_ref[0] + hbuf[...]) * gg_ref[0]
    out = jnp.dot(y.astype(BF16), wo_ref[...], preferred_element_type=F32)
    m = mod_ref[0, 0]
    x_new = x_ref[0] + m[2:3] * out
    o_ref[0] = x_new
    _route_tile(x_new, m, gffn_ref, wr_ref, rb_ref, carry, first,
                h_ref, e0_ref, e1_ref, p0_ref, p1_ref, gt_ref, cnt_ref)


def _lru_core(x, mods, ugg, p, route_args):
    b, n, d = x.shape
    u, gg = ugg
    nt = n // TR
    nb = d // LRU_BLOCKS
    wg = jnp.concatenate([p["lru_wa"], p["lru_wx"]], axis=-1).astype(BF16)
    scratch = [pltpu.VMEM((TR + SUBLANES, d), F32), pltpu.VMEM((SUBLANES, d), F32),
               pltpu.VMEM((SUBLANES, d), F32), pltpu.VMEM((TR, d), F32), pltpu.VMEM((TR, d), F32),
               pltpu.VMEM((TR, d), F32)]

    def gate_specs(order):
        return [pl.BlockSpec((1, TR, d), lambda bb, t: (bb, order(t), 0)),
                _full_spec((LRU_CONV_W, d)), _full_spec((1, d)), _full_spec((LRU_BLOCKS, nb, 2 * nb)),
                _full_spec((1, d)), _full_spec((1, d)), _full_spec((1, d))]

    def gate_args(dd):
        return (u, p["lru_conv_w"][dd], p["lru_conv_b"][dd].reshape(1, d), wg[dd], p["lru_ba"][dd].reshape(1, d),
                p["lru_bx"][dd].reshape(1, d), p["lru_lam"][dd].reshape(1, d))

    hf = pl.pallas_call(
        _lru_fwd_kernel, grid=(b, nt),
        in_specs=gate_specs(lambda t: t),
        out_specs=_row_spec(TR, d),
        out_shape=jax.ShapeDtypeStruct((b, n, d), F32),
        scratch_shapes=scratch,
        compiler_params=_params(("arbitrary", "arbitrary")),
        name="lru_fwd",
    )(*gate_args(0))

    def rev(t):
        return jnp.where(t == 0, 0, nt - t)

    def rev_spec():
        return pl.BlockSpec((1, TR, d), lambda bb, t: (bb, rev(t), 0))

    r_in, r_out, r_shape, r_scratch = _route_io(b, n, d, TR, rev)
    return pl.pallas_call(
        _lru_bwd_kernel, grid=(b, nt),
        in_specs=gate_specs(rev) + [rev_spec(), rev_spec(), rev_spec(), _seg_mod_spec(d), _full_spec((d, d))] + r_in,
        out_specs=[rev_spec()] + r_out,
        out_shape=[jax.ShapeDtypeStruct((b, n, d), F32)] + r_shape,
        scratch_shapes=scratch + r_scratch,
        compiler_params=_params(("arbitrary", "arbitrary")),
        name="lru_bwd",
    )(*gate_args(1), hf, gg, x, mods, p["lru_w_out"].astype(BF16), *route_args)


def _slot(k, tok, e0_ref, e1_ref, p0_ref, p1_ref, ps_ref):
    e_ref, p_ref = ((e0_ref, p0_ref), (e1_ref, p1_ref))[k]
    return ps_ref[e_ref[tok]] + p_ref[tok]


def _dispatch_kernel(e0_ref, e1_ref, p0_ref, p1_ref, ps_ref, zrow_ref, nu_ref, h_ref, xs_ref, ring, zbuf, sem, zsem,
                     *, n_blocks):
    s = pl.program_id(0)
    ns = pl.num_programs(0)
    tr = h_ref.shape[0] // SUBLANES
    blk = zbuf.shape[0] // SUBLANES
    par = s % 2

    @pl.when(s == 0)
    def _():
        zbuf[...] = jnp.zeros_like(zbuf)
        zsrc = _tile_rows(zbuf, 0, 1)
        for e in range(N_EXPERTS):
            first, npad = zrow_ref[e], zrow_ref[N_EXPERTS + e]

            def zero_row(i, carry, first=first):
                pltpu.make_async_copy(zsrc, _tile_rows(xs_ref, first + i, 1), zsem).start()
                return carry

            def zero_row_wait(i, carry):
                pltpu.make_async_copy(zsrc, _tile_rows(xs_ref, 0, 1), zsem).wait()
                return carry

            lax.fori_loop(0, npad, zero_row, 0)
            lax.fori_loop(0, npad, zero_row_wait, 0)

        def zero_blk(i, carry):
            pltpu.make_async_copy(zbuf, _tile_rows(xs_ref, i * blk, blk), zsem).start()
            return carry

        def zero_blk_wait(i, carry):
            pltpu.make_async_copy(zbuf, _tile_rows(xs_ref, 0, blk), zsem).wait()
            return carry

        lax.fori_loop(nu_ref[0], n_blocks, zero_blk, 0)
        lax.fori_loop(nu_ref[0], n_blocks, zero_blk_wait, 0)

    def wait_ring(p):
        for _ in range(TOP_K):
            pltpu.make_async_copy(ring.at[p], _tile_rows(xs_ref, 0, tr), sem.at[p]).wait()

    @pl.when(s >= 2)
    def _():
        wait_ring(par)

    ring[par] = h_ref[...]
    base = s * tr

    def issue(i, carry):
        for k in range(TOP_K):
            slot = _slot(k, base + i, e0_ref, e1_ref, p0_ref, p1_ref, ps_ref)
            pltpu.make_async_copy(_tile_rows(ring.at[par], i, 1), _tile_rows(xs_ref, slot, 1),
                                  sem.at[par]).start(priority=k)
        return carry

    lax.fori_loop(0, tr, issue, 0, unroll=8)

    @pl.when(s == ns - 1)
    def _():
        wait_ring(par)

        @pl.when(ns >= 2)
        def _():
            wait_ring(1 - par)


def _expert_kernel(be_ref, nu_ref, xs_ref, w1_ref, w3_ref, w2_ref, ys_ref, w1b, w3b, w2b):
    i = pl.program_id(0)
    used = i < nu_ref[0]
    new_expert = jnp.logical_or(i == 0, be_ref[i] != be_ref[jnp.maximum(i - 1, 0)])

    @pl.when(jnp.logical_and(used, new_expert))
    def _():
        w1b[...] = w1_ref[0, 0].astype(BF16)
        w3b[...] = w3_ref[0, 0].astype(BF16)
        w2b[...] = w2_ref[0, 0].astype(BF16)

    @pl.when(used)
    def _():
        xb = _load_token_tiles(xs_ref, xs_ref.shape[0] // SUBLANES).astype(BF16)
        a = jnp.dot(xb, w1b[...], preferred_element_type=F32)
        b = jnp.dot(xb, w3b[...], preferred_element_type=F32)
        hid = (a * jax.nn.sigmoid(a)) * b
        _store_token_tiles(ys_ref, jnp.dot(hid.astype(BF16), w2b[...], preferred_element_type=F32))

    @pl.when(jnp.logical_not(used))
    def _():
        ys_ref[...] = jnp.zeros_like(ys_ref)


def _combine_kernel(e0_ref, e1_ref, p0_ref, p1_ref, ps_ref, x_ref, mod_ref, gt_ref, ys_ref, *refs,
                    rows_per_batch, tile_off, tail, n_tail_in):
    if tail is None:
        nf_ref, o_ref, ybuf, sem = refs
    else:
        modn_ref, gn_ref = refs[0], refs[1]
        tail_in = refs[2:2 + n_tail_in]
        o_ref = refs[2 + n_tail_in]
        tail_out = refs[3 + n_tail_in:-2]
        ybuf, sem = refs[-2], refs[-1]
    bb, t = pl.program_id(0), pl.program_id(1)
    nt = pl.num_programs(1)
    tr = x_ref.shape[1]
    step = bb * nt + t
    nsteps = pl.num_programs(0) * nt

    def tok_base(s):
        return (s // nt) * rows_per_batch + (s % nt + tile_off) * tr

    def issue(s):
        par = s % 2
        base = tok_base(s)

        def one(i, carry):
            for k in range(TOP_K):
                slot = _slot(k, base + i, e0_ref, e1_ref, p0_ref, p1_ref, ps_ref)
                pltpu.make_async_copy(_tile_rows(ys_ref, slot, 1),
                                      _tile_rows(ybuf.at[par, k], i, 1), sem.at[par]).start(priority=k)
            return carry

        lax.fori_loop(0, tr, one, 0, unroll=8)

    @pl.when(step == 0)
    def _():
        issue(step)

    @pl.when(step + 1 < nsteps)
    def _():
        issue(step + 1)

    par = step % 2
    for k in range(TOP_K):
        pltpu.make_async_copy(_tile_rows(ys_ref, 0, tr), ybuf.at[par, k], sem.at[par]).wait()

    gt = gt_ref[0]
    y = (gt[:, 0:1] * _load_token_tiles(ybuf.at[par, 0], tr)
         + gt[:, 1:2] * _load_token_tiles(ybuf.at[par, 1], tr))
    out = x_ref[0] + mod_ref[0, 0][5:6] * y
    if tail is None:
        o_ref[0] = out * lax.rsqrt(jnp.mean(out * out, axis=-1, keepdims=True) + EPS) * nf_ref[...]
    else:
        o_ref[0] = out
        mn = modn_ref[0, 0]
        tail(_norm_mod(out, gn_ref[...], mn[0:1], mn[1:2]), *tail_in, *tail_out)


def _moe(x, mods, routed, layer, w1, w3, w2, ctx_len, norm_f=None, next_ip=None, next_mods=None, next_g=None):
    b, n, d = x.shape
    nt = n // TR
    n_tok = b * n
    de = w1.shape[-1]
    h, e0, e1, p0, p1, gtab, counts = routed

    counts = counts[:, 0]
    padded = (counts + MOE_BLK - 1) // MOE_BLK * MOE_BLK
    pend = jnp.cumsum(padded)
    pstart = (pend - padded).astype(I32)
    n_slots = n_tok * TOP_K + N_EXPERTS * MOE_BLK
    n_blocks = n_slots // MOE_BLK
    n_used = (pend[-1] // MOE_BLK).astype(I32).reshape(1)
    blk_start = jnp.arange(n_blocks, dtype=I32) * MOE_BLK
    blk_e = jnp.minimum(jnp.sum((pend[None, :] <= blk_start[:, None]).astype(I32), axis=1), N_EXPERTS - 1)

    assign = tuple(a.reshape(n_tok) for a in (e0, e1, p0, p1)) + (pstart,)
    zrow = jnp.concatenate([pstart + counts, padded - counts]).astype(I32)

    xs = pl.pallas_call(
        functools.partial(_dispatch_kernel, n_blocks=n_blocks),
        grid_spec=pltpu.PrefetchScalarGridSpec(
            num_scalar_prefetch=7, grid=(n_tok // TR,),
            in_specs=[pl.BlockSpec((TR * SUBLANES, LANES), lambda s, *_: (s, 0))],
            out_specs=pl.BlockSpec(memory_space=pl.ANY),
            scratch_shapes=[pltpu.VMEM((2, TR * SUBLANES, LANES), F32), pltpu.VMEM((MOE_BLK * SUBLANES, LANES), F32),
                            pltpu.SemaphoreType.DMA((2,)), pltpu.SemaphoreType.DMA]),
        out_shape=jax.ShapeDtypeStruct((n_slots * SUBLANES, LANES), F32),
        compiler_params=_params(("arbitrary",)),
        name="moe_dispatch",
    )(*assign, zrow, n_used, h)

    def wspec(shape):
        return pl.BlockSpec((1, 1) + shape, lambda i, be, nu: (layer, be[i], 0, 0))

    def blk_spec():
        return pl.BlockSpec((MOE_BLK * SUBLANES, LANES), lambda i, be, nu: (i, 0))

    ys = pl.pallas_call(
        _expert_kernel,
        grid_spec=pltpu.PrefetchScalarGridSpec(
            num_scalar_prefetch=2, grid=(n_blocks,),
            in_specs=[blk_spec(), wspec((d, de)), wspec((d, de)), wspec((de, d))],
            out_specs=blk_spec(),
            scratch_shapes=[pltpu.VMEM((d, de), BF16), pltpu.VMEM((d, de), BF16), pltpu.VMEM((de, d), BF16)]),
        out_shape=jax.ShapeDtypeStruct((n_slots * SUBLANES, LANES), F32),
        compiler_params=_params(("arbitrary",)),
        name="moe_experts",
    )(blk_e, n_used, xs, w1, w3, w2)

    final = next_ip is None
    tile_off = ctx_len // TR if final else 0
    nt_out = nt - tile_off
    common_specs = [_row_spec(TR, d, tile_off), _seg_mod_spec(d, tile_off=tile_off), _row_spec(TR, LANES, tile_off),
                    pl.BlockSpec(memory_space=pl.ANY)]
    scratch = [pltpu.VMEM((2, TOP_K, TR * SUBLANES, LANES), F32), pltpu.SemaphoreType.DMA((2,))]
    stream_spec = _row_spec(TR, d)
    stream_shape = jax.ShapeDtypeStruct((b, nt_out * TR, d), F32)
    if final:
        return pl.pallas_call(
            functools.partial(_combine_kernel, rows_per_batch=n, tile_off=tile_off, tail=None, n_tail_in=0),
            grid_spec=pltpu.PrefetchScalarGridSpec(
                num_scalar_prefetch=len(assign), grid=(b, nt_out),
                in_specs=common_specs + [_full_spec((1, d))],
                out_specs=stream_spec, scratch_shapes=scratch),
            out_shape=stream_shape,
            compiler_params=_params(("arbitrary", "arbitrary")),
            name="moe_combine_final",
        )(*assign, x, mods, gtab, ys, norm_f.reshape(1, d))
    outs = pl.pallas_call(
        functools.partial(_combine_kernel, rows_per_batch=n, tile_off=tile_off, tail=next_ip["tail"],
                          n_tail_in=len(next_ip["args"])),
        grid_spec=pltpu.PrefetchScalarGridSpec(
            num_scalar_prefetch=len(assign), grid=(b, nt_out),
            in_specs=common_specs + [_seg_mod_spec(d), _full_spec((1, d))] + next_ip["in_specs"],
            out_specs=[stream_spec] + next_ip["out_specs"], scratch_shapes=scratch),
        out_shape=[stream_shape] + next_ip["out_shape"],
        compiler_params=_params(("arbitrary", "arbitrary")),
        name="moe_combine_in_proj",
    )(*assign, x, mods, gtab, ys, next_mods, next_g.reshape(1, d), *next_ip["args"])
    return outs[0], outs[1:]


def kernel(x, c, ctx, c_ctx, w_mod, b_mod, norm_mix, norm_ffn, norm_f, conv_pw1, conv_pw1_b, conv_dw, conv_dw_b, conv_ln_g, conv_ln_b, conv_pw2, conv_pw2_b, attn_w_qkv, attn_w_o, attn_sink, lru_w_in, lru_conv_w, lru_conv_b, lru_wa, lru_ba, lru_wx, lru_bx, lru_lam, lru_w_out, moe_w_router, moe_router_bias, moe_w1, moe_w3, moe_w2):
    b, s, d = x.shape
    ctx_len = ctx.shape[1]
    depth = w_mod.shape[0]
    n = ctx_len + s
    assert ctx_len == TR and s % TR == 0 and d == SUBLANES * LANES
    assert moe_w_router.shape[1] == N_EXPERTS

    mods = _modulation(c, c_ctx, w_mod, b_mod)
    xs = jnp.concatenate([ctx, x], axis=1)

    wr = jnp.zeros((EXPERTS_PER_GROUP, SUBLANES, d), F32).at[:, :N_GROUPS].set(
        moe_w_router.T.reshape(N_GROUPS, EXPERTS_PER_GROUP, d).swapaxes(0, 1))
    wr = wr.reshape(ROUTER_ROWS, d).astype(BF16)
    rb = jnp.zeros((EXPERTS_PER_GROUP, SUBLANES), F32).at[:, :N_GROUPS].set(
        moe_router_bias.astype(F32).reshape(N_GROUPS, EXPERTS_PER_GROUP).T).reshape(-1, 1)

    def layer_params(i):
        slot = i // N_MIXERS
        return [dict(conv_pw1=conv_pw1, conv_pw1_b=conv_pw1_b, conv_dw=conv_dw, conv_dw_b=conv_dw_b,
                     conv_ln_g=conv_ln_g, conv_ln_b=conv_ln_b, conv_pw2=conv_pw2, conv_pw2_b=conv_pw2_b),
                dict(attn_w_qkv=attn_w_qkv, attn_w_o=attn_w_o, attn_sink=attn_sink),
                dict(lru_w_in=lru_w_in, lru_conv_w=lru_conv_w, lru_conv_b=lru_conv_b, lru_wa=lru_wa, lru_ba=lru_ba,
                     lru_wx=lru_wx, lru_bx=lru_bx, lru_lam=lru_lam, lru_w_out=lru_w_out)][i % N_MIXERS], slot

    def params_of(i):
        group, slot = layer_params(i)
        return {k: v[slot] for k, v in group.items()}

    p = params_of(0)
    ip = _in_proj(0, p, b, n, d, ctx_len)
    a_out = _in_proj_standalone(ip, xs, mods[0], norm_mix[0])
    if not isinstance(a_out, (list, tuple)):
        a_out = [a_out]
    for i in range(depth):
        kind = i % N_MIXERS
        route_args = (norm_ffn[i].reshape(1, d), wr, rb)
        if kind == 0:
            res = _conv_core(xs, mods[i], a_out[0], p, route_args)
        elif kind == 1:
            res = _attn_core(xs, mods[i], a_out, p, route_args, ctx_len)
        else:
            res = _lru_core(xs, mods[i], a_out, p, route_args)
        x_new, routed = res[0], res[1:]
        if i == depth - 1:
            return _moe(x_new, mods[i], routed, i, moe_w1, moe_w3, moe_w2, ctx_len, norm_f=norm_f)
        p = params_of(i + 1)
        ip = _in_proj((i + 1) % N_MIXERS, p, b, n, d, ctx_len)
        xs, a_out = _moe(x_new, mods[i], routed, i, moe_w1, moe_w3, moe_w2, ctx_len,
                         next_ip=ip, next_mods=mods[i + 1], next_g=norm_mix[i + 1])
```

```python
import functools

import jax
import jax.numpy as jnp
from jax import lax
from jax.experimental import pallas as pl
from jax.experimental.pallas import tpu as pltpu

F32 = jnp.float32
BF16 = jnp.bfloat16
I32 = jnp.int32

EPS = 1e-6
N_MOD = 6
N_MIXERS = 3
GRID_W = 64
CONV_WIDTH = 31
CONV_HALF = (CONV_WIDTH - 1) // 2
HEAD_DIM = 64
N_KV_HEADS = 4
GQA_GROUP = 4
WINDOW = 128
ROPE_BASE = 10000.0
ROPE_PAIRS = HEAD_DIM // 4
LRU_BLOCKS = 8
LRU_CONV_W = 4
LRU_C = 8.0
N_EXPERTS = 16
N_GROUPS = 4
EXPERTS_PER_GROUP = 4
TOP_K = 2

LANES = 128
SUBLANES = 8
TR = 256
TQ = 128
HALO = 16
MOE_BLK = 512
ROUTER_ROWS = SUBLANES * EXPERTS_PER_GROUP
NEG = -1e30
VMEM_LIMIT = 56 * 1024 * 1024


def _params(sem, vmem=VMEM_LIMIT):
    return pltpu.CompilerParams(dimension_semantics=sem, vmem_limit_bytes=vmem)


def _norm_mod(x, g, shift, scale):
    y = x * lax.rsqrt(jnp.mean(x * x, axis=-1, keepdims=True) + EPS)
    return (y * g) * (1.0 + scale) + shift


def _seg_mod_spec(d, tiles_per_seg=1, tile_off=0):
    return pl.BlockSpec((1, 1, N_MOD, d),
                        lambda b, t, *_: (b, jnp.minimum((t + tile_off) // tiles_per_seg, 1), 0, 0))


def _row_spec(tr, d, tile_off=0):
    return pl.BlockSpec((1, tr, d), lambda b, t, *_: (b, t + tile_off, 0))


def _full_spec(shape):
    nd = len(shape)
    return pl.BlockSpec(shape, lambda b, t, *_: (0,) * nd)


def _mod_kernel(c_ref, w_ref, b_ref, o_ref):
    c = c_ref[...]
    sc = c * jax.nn.sigmoid(c)
    o_ref[0] = jnp.dot(sc.astype(BF16), w_ref[0].astype(BF16), preferred_element_type=F32) + b_ref[0]


def _modulation(c, c_ctx, w_mod, b_mod):
    depth, d, nout = w_mod.shape
    b = c.shape[0]
    assert b + 1 <= SUBLANES
    rows = jnp.zeros((SUBLANES, d), F32).at[:b].set(c).at[b].set(c_ctx)
    tn = 1536
    raw = pl.pallas_call(
        _mod_kernel,
        grid=(depth, nout // tn),
        in_specs=[pl.BlockSpec((SUBLANES, d), lambda i, j: (0, 0)),
                  pl.BlockSpec((1, d, tn), lambda i, j: (i, 0, j)),
                  pl.BlockSpec((1, 1, tn), lambda i, j: (i, 0, j))],
        out_specs=pl.BlockSpec((1, SUBLANES, tn), lambda i, j: (i, 0, j)),
        out_shape=jax.ShapeDtypeStruct((depth, SUBLANES, nout), F32),
        compiler_params=_params(("arbitrary", "arbitrary")),
        name="modulation",
    )(rows, w_mod, b_mod.reshape(depth, 1, nout))
    raw = raw.reshape(depth, SUBLANES, N_MOD, d)
    lat = raw[:, :b]
    ctx = jnp.broadcast_to(raw[:, b][:, None], lat.shape)
    return jnp.stack([ctx, lat], axis=2)


def _store_token_tiles(ref, mat):
    rows = mat.shape[0]
    for s in range(SUBLANES):
        ref[pl.ds(s, rows, stride=SUBLANES), :] = mat[:, s * LANES:(s + 1) * LANES]


def _load_token_tiles(ref, rows):
    return jnp.concatenate([ref[pl.ds(s, rows, stride=SUBLANES), :] for s in range(SUBLANES)], axis=1)


def _tile_rows(ref, first_token, n_tokens):
    return ref.at[pl.ds(pl.multiple_of(first_token * SUBLANES, SUBLANES), n_tokens * SUBLANES)]


def _route_tile(x, m, g_ref, wr_ref, rb_ref, carry, first, h_ref, e0_ref, e1_ref, p0_ref, p1_ref, gt_ref, cnt_ref):
    tr = x.shape[0]

    @pl.when(first)
    def _():
        carry[...] = jnp.zeros_like(carry)

    h = _norm_mod(x, g_ref[...], m[3:4], m[4:5])
    _store_token_tiles(h_ref, h)
    logit = lax.dot_general(wr_ref[...], h.astype(BF16), (((1,), (1,)), ((), ())),
                            preferred_element_type=F32)
    s = jax.nn.sigmoid(logit)
    sel = s + rb_ref[...]
    sj = [s[SUBLANES * j:SUBLANES * (j + 1)] for j in range(EXPERTS_PER_GROUP)]
    cj = [sel[SUBLANES * j:SUBLANES * (j + 1)] for j in range(EXPERTS_PER_GROUP)]

    hi1, lo1 = jnp.maximum(cj[0], cj[1]), jnp.minimum(cj[0], cj[1])
    hi2, lo2 = jnp.maximum(cj[2], cj[3]), jnp.minimum(cj[2], cj[3])
    top1 = jnp.maximum(hi1, hi2)
    top2 = jnp.maximum(jnp.minimum(hi1, hi2), jnp.maximum(lo1, lo2))
    row = lax.broadcasted_iota(I32, (SUBLANES, tr), 0)
    gscore = jnp.where(row < N_GROUPS, top1 + top2, -jnp.inf)
    gmax = jnp.max(gscore, axis=0, keepdims=True)
    grp = jnp.min(jnp.where(gscore == gmax, row, SUBLANES), axis=0, keepdims=True)
    pick = row == grp
    c = [jnp.sum(jnp.where(pick, v, 0.0), axis=0, keepdims=True) for v in cj]
    w = [jnp.sum(jnp.where(pick, v, 0.0), axis=0, keepdims=True) for v in sj]

    def argtop(vals):
        best, bi, bw = vals[0], jnp.zeros_like(grp), w[0]
        for j in range(1, EXPERTS_PER_GROUP):
            better = vals[j] > best
            best = jnp.where(better, vals[j], best)
            bi = jnp.where(better, j, bi)
            bw = jnp.where(better, w[j], bw)
        return bi, bw

    i0, w0 = argtop(c)
    i1, w1 = argtop([jnp.where(i0 == j, -jnp.inf, c[j]) for j in range(EXPERTS_PER_GROUP)])
    e0 = grp * EXPERTS_PER_GROUP + i0
    e1 = grp * EXPERTS_PER_GROUP + i1
    wsum = w0 + w1
    g0 = w0 / wsum
    g1 = w1 / wsum

    erow = lax.broadcasted_iota(I32, (N_EXPERTS, tr), 0)
    member = jnp.logical_or(erow == e0, erow == e1)
    upper = (lax.broadcasted_iota(I32, (tr, tr), 0) < lax.broadcasted_iota(I32, (tr, tr), 1))
    before = jnp.dot(member.astype(BF16), upper.astype(BF16), preferred_element_type=F32) + carry[...]
    p0 = jnp.sum(jnp.where(erow == e0, before, 0.0), axis=0, keepdims=True)
    p1 = jnp.sum(jnp.where(erow == e1, before, 0.0), axis=0, keepdims=True)
    total = carry[...] + jnp.sum(member.astype(F32), axis=1, keepdims=True)
    carry[...] = total
    cnt_ref[...] = total.astype(I32)

    e0_ref[0] = e0
    e1_ref[0] = e1
    p0_ref[0] = p0.astype(I32)
    p1_ref[0] = p1.astype(I32)
    grow = lax.broadcasted_iota(I32, (LANES, tr), 0)
    gmat = jnp.where(grow == 0, g0, jnp.where(grow == 1, g1, 0.0))
    gt_ref[0] = gmat.T


def _route_io(b, n, d, tile, tile_of):
    nt = n // tile

    def flat(bb, t):
        return bb * nt + tile_of(t)

    lane_major = pl.BlockSpec((1, 1, tile), lambda bb, t, *_: (flat(bb, t), 0, 0))
    lane_major_shape = jax.ShapeDtypeStruct((b * nt, 1, tile), I32)
    in_specs = [_full_spec((1, d)), _full_spec((ROUTER_ROWS, d)), _full_spec((ROUTER_ROWS, 1))]
    out_specs = [pl.BlockSpec((tile * SUBLANES, LANES), lambda bb, t, *_: (flat(bb, t), 0)),
                 lane_major, lane_major, lane_major, lane_major,
                 pl.BlockSpec((1, tile, LANES), lambda bb, t, *_: (bb, tile_of(t), 0)),
                 pl.BlockSpec((N_EXPERTS, 1), lambda bb, t, *_: (0, 0))]
    out_shape = [jax.ShapeDtypeStruct((b * n * SUBLANES, LANES), F32), lane_major_shape, lane_major_shape,
                 lane_major_shape, lane_major_shape,
                 jax.ShapeDtypeStruct((b, n, LANES), F32), jax.ShapeDtypeStruct((N_EXPERTS, 1), I32)]
    scratch = [pltpu.VMEM((N_EXPERTS, 1), F32)]
    return in_specs, out_specs, out_shape, scratch


def _first_step():
    return jnp.logical_and(pl.program_id(0) == 0, pl.program_id(1) == 0)


def _glu_tail(h, w_ref, b_ref, o_ref):
    d = h.shape[-1]
    u = jnp.dot(h.astype(BF16), w_ref[...], preferred_element_type=F32) + b_ref[...]
    o_ref[0] = u[:, :d] * jax.nn.sigmoid(u[:, d:])


def _qkv_tail(h, w_ref, cos_ref, sin_ref, q_ref, k_ref, v_ref):
    d = h.shape[-1]
    nk = k_ref.shape[-1]
    t = jnp.dot(h.astype(BF16), w_ref[...], preferred_element_type=F32)
    cos = cos_ref[...]
    sin = sin_ref[...]
    lane = lax.broadcasted_iota(I32, cos.shape, 1)
    first_half = (lane % (2 * ROPE_PAIRS)) < ROPE_PAIRS

    def rope(xg):
        partner = jnp.where(first_half, pltpu.roll(xg, LANES - ROPE_PAIRS, 1), pltpu.roll(xg, ROPE_PAIRS, 1))
        return xg * cos + partner * sin

    for j in range(d // LANES):
        cs = slice(j * LANES, (j + 1) * LANES)
        q_ref[0, :, cs] = rope(t[:, cs]).astype(BF16)
    for j in range(nk // LANES):
        cs = slice(j * LANES, (j + 1) * LANES)
        k_ref[0, :, cs] = rope(t[:, d + j * LANES:d + (j + 1) * LANES]).astype(BF16)
    v_ref[0] = t[:, d + nk:].astype(BF16)


def _lru_in_tail(h, w_ref, u_ref, gg_ref):
    d = h.shape[-1]
    t = jnp.dot(h.astype(BF16), w_ref[...], preferred_element_type=F32)
    u_ref[0] = t[:, :d]
    gg_ref[0] = jax.nn.gelu(t[:, d:])


def _rope_tables(n, ctx_len):
    pos = jnp.arange(n - ctx_len, dtype=I32)
    inv = ROPE_BASE ** (-jnp.arange(ROPE_PAIRS, dtype=F32) / ROPE_PAIRS)
    ar = (pos // GRID_W).astype(F32)[:, None] * inv
    ac = (pos % GRID_W).astype(F32)[:, None] * inv
    ang = jnp.concatenate([ar, ar, ac, ac], axis=-1)
    sign = jnp.tile(jnp.concatenate([-jnp.ones((ROPE_PAIRS,), F32), jnp.ones((ROPE_PAIRS,), F32)]), 2)
    cos = jnp.concatenate([jnp.ones((ctx_len, HEAD_DIM), F32), jnp.cos(ang)], axis=0)
    sin = jnp.concatenate([jnp.zeros((ctx_len, HEAD_DIM), F32), jnp.sin(ang) * sign], axis=0)
    return jnp.tile(cos, (1, 2)), jnp.tile(sin, (1, 2))


def _in_proj(kind, p, b, n, d, ctx_len):
    if kind == 0:
        args = (p["conv_pw1"].astype(BF16), p["conv_pw1_b"].reshape(1, 2 * d))
        return dict(tail=_glu_tail, args=args, in_specs=[_full_spec((d, 2 * d)), _full_spec((1, 2 * d))],
                    out_specs=[_row_spec(TR, d)], out_shape=[jax.ShapeDtypeStruct((b, n, d), F32)])
    if kind == 1:
        kvd = N_KV_HEADS * HEAD_DIM
        nk = N_KV_HEADS * LANES
        assert d == 2 * nk
        scale = HEAD_DIM ** -0.5
        w_qkv = p["attn_w_qkv"]

        def dup(w):
            w = w.reshape(d, N_KV_HEADS, HEAD_DIM)
            return jnp.concatenate([w, w], axis=-1).reshape(d, nk)

        w_all = jnp.concatenate([w_qkv[:, :d] * scale, dup(w_qkv[:, d:d + kvd]), dup(w_qkv[:, d + kvd:])],
                                axis=1).astype(BF16)
        cos, sin = _rope_tables(n, ctx_len)
        table = pl.BlockSpec((TR, LANES), lambda bb, t, *_: (t, 0))
        return dict(tail=_qkv_tail, args=(w_all, cos, sin), in_specs=[_full_spec((d, d + 2 * nk)), table, table],
                    out_specs=[_row_spec(TR, d), _row_spec(TR, nk), _row_spec(TR, nk)],
                    out_shape=[jax.ShapeDtypeStruct((b, n, d), BF16), jax.ShapeDtypeStruct((b, n, nk), BF16),
                               jax.ShapeDtypeStruct((b, n, nk), BF16)])
    return dict(tail=_lru_in_tail, args=(p["lru_w_in"].astype(BF16),), in_specs=[_full_spec((d, 2 * d))],
                out_specs=[_row_spec(TR, d), _row_spec(TR, d)],
                out_shape=[jax.ShapeDtypeStruct((b, n, d), F32), jax.ShapeDtypeStruct((b, n, d), F32)])


def _in_proj_kernel(x_ref, mod_ref, g_ref, *refs, tail):
    m = mod_ref[0, 0]
    tail(_norm_mod(x_ref[0], g_ref[...], m[0:1], m[1:2]), *refs)


def _in_proj_standalone(ip, x, mods, g):
    b, n, d = x.shape
    return pl.pallas_call(
        functools.partial(_in_proj_kernel, tail=ip["tail"]), grid=(b, n // TR),
        in_specs=[_row_spec(TR, d), _seg_mod_spec(d), _full_spec((1, d))] + ip["in_specs"],
        out_specs=ip["out_specs"], out_shape=ip["out_shape"],
        compiler_params=_params(("parallel", "parallel")),
        name="in_proj",
    )(x, mods, g.reshape(1, d), *ip["args"])


def _conv_core_kernel(x_ref, mod_ref, gp_ref, gc_ref, gn_ref, dw_ref, dwb_ref, lng_ref, lnb_ref, w_ref, b_ref,
                      gffn_ref, wr_ref, rb_ref, o_ref, h_ref, e0_ref, e1_ref, p0_ref, p1_ref, gt_ref, cnt_ref,
                      gbuf, cbuf, carry):
    t = pl.program_id(1)
    nt = pl.num_programs(1)
    first = _first_step()
    tr, d = gc_ref.shape[1], gc_ref.shape[2]
    seg_first = t <= 1
    seg_last = jnp.logical_or(t == 0, t == nt - 1)
    gbuf[0, 0:HALO, :] = jnp.where(seg_first, 0.0, gp_ref[0])
    gbuf[0, HALO:HALO + tr, :] = gc_ref[0]
    gbuf[0, HALO + tr:HALO + tr + HALO, :] = jnp.where(seg_last, 0.0, gn_ref[0])
    span = tr + 2 * HALO - SUBLANES
    for r in range(1, SUBLANES):
        gbuf[r, 0:span, :] = gbuf[0, r:r + span, :]

    rc = 32
    base = HALO - CONV_HALF

    def chunk(i, carry_):
        r0 = pl.multiple_of(i * rc, rc)
        for c in range(d // LANES):
            cs = slice(c * LANES, (c + 1) * LANES)
            acc = jnp.zeros((rc, LANES), F32)
            for k in range(CONV_WIDTH):
                off = base + k
                rows = pl.ds(r0 + (off // SUBLANES) * SUBLANES, rc)
                acc = acc + dw_ref[k:k + 1, cs] * gbuf[off % SUBLANES, rows, cs]
            cbuf[pl.ds(r0, rc), cs] = acc
        return carry_

    lax.fori_loop(0, tr // rc, chunk, 0)

    u = cbuf[...] + dwb_ref[...]
    mu = jnp.mean(u, axis=-1, keepdims=True)
    uc = u - mu
    var = jnp.mean(uc * uc, axis=-1, keepdims=True)
    v = uc * lax.rsqrt(var + EPS) * lng_ref[...] + lnb_ref[...]
    v = v * jax.nn.sigmoid(v)
    y = jnp.dot(v.astype(BF16), w_ref[...], preferred_element_type=F32) + b_ref[...]
    m = mod_ref[0, 0]
    x_new = x_ref[0] + m[2:3] * y
    o_ref[0] = x_new
    _route_tile(x_new, m, gffn_ref, wr_ref, rb_ref, carry, first,
                h_ref, e0_ref, e1_ref, p0_ref, p1_ref, gt_ref, cnt_ref)


def _conv_core(x, mods, glu, p, route_args):
    b, n, d = x.shape
    nt = n // TR
    hb = TR // HALO
    nh = n // HALO
    dwp = jnp.zeros((32, d), F32).at[:CONV_WIDTH].set(p["conv_dw"])
    r_in, r_out, r_shape, r_scratch = _route_io(b, n, d, TR, lambda t: t)
    return pl.pallas_call(
        _conv_core_kernel, grid=(b, nt),
        in_specs=[_row_spec(TR, d), _seg_mod_spec(d),
                  pl.BlockSpec((1, HALO, d), lambda bb, t: (bb, jnp.maximum(t * hb - 1, 0), 0)),
                  _row_spec(TR, d),
                  pl.BlockSpec((1, HALO, d), lambda bb, t: (bb, jnp.minimum((t + 1) * hb, nh - 1), 0)),
                  _full_spec((32, d)), _full_spec((1, d)), _full_spec((1, d)), _full_spec((1, d)),
                  _full_spec((d, d)), _full_spec((1, d))] + r_in,
        out_specs=[_row_spec(TR, d)] + r_out,
        out_shape=[jax.ShapeDtypeStruct((b, n, d), F32)] + r_shape,
        scratch_shapes=[pltpu.VMEM((SUBLANES, TR + 2 * HALO, d), F32), pltpu.VMEM((TR, d), F32)] + r_scratch,
        compiler_params=_params(("arbitrary", "arbitrary")),
        name="conv_core",
    )(x, mods, glu, glu, glu, dwp, p["conv_dw_b"].reshape(1, d), p["conv_ln_g"].reshape(1, d),
      p["conv_ln_b"].reshape(1, d), p["conv_pw2"].astype(BF16), p["conv_pw2_b"].reshape(1, d), *route_args)


def _attn_core_kernel(sink_ref, x_ref, mod_ref, q_ref, kc_ref, vc_ref, kp_ref, km_ref, kn_ref,
                      vp_ref, vm_ref, vn_ref, wo_ref, gffn_ref, wr_ref, rb_ref,
                      o_ref, h_ref, e0_ref, e1_ref, p0_ref, p1_ref, gt_ref, cnt_ref, carry, *, ctx_len, n_rows):
    first = _first_step()
    tq = kp_ref.shape[1]
    nctx = kc_ref.shape[1]
    nkeys = nctx + 3 * tq
    lane = lax.broadcasted_iota(I32, (tq, LANES), 1)
    lo = lane < HEAD_DIM
    col = lax.broadcasted_iota(I32, (tq, nkeys), 1)
    row = lax.broadcasted_iota(I32, (tq, nkeys), 0)

    def attend(t, q_rows, k_parts, v_parts):
        rq = t * tq + row
        rk = (t - 1) * tq + (col - nctx)
        win_ok = (rk >= ctx_len) & (rk < n_rows) & (jnp.abs(rq - rk) <= WINDOW) & (t * tq >= ctx_len)
        bias = jnp.where((col < nctx) | win_ok, 0.0, NEG).astype(F32)
        out_cols = []
        for g in range(N_KV_HEADS):
            gs = slice(g * LANES, (g + 1) * LANES)
            kg = jnp.concatenate([kc_ref[0, :, gs]] + [part(gs) for part in k_parts], axis=0)
            vg = jnp.concatenate([vc_ref[0, :, gs]] + [part(gs) for part in v_parts], axis=0)
            qa = q_ref[0, q_rows, (2 * g) * LANES:(2 * g + 1) * LANES]
            qb = q_ref[0, q_rows, (2 * g + 1) * LANES:(2 * g + 2) * LANES]
            zero = jnp.zeros_like(qa)
            q4 = jnp.concatenate([jnp.where(lo, qa, zero), jnp.where(lo, zero, qa),
                                  jnp.where(lo, qb, zero), jnp.where(lo, zero, qb)], axis=0)
            s = lax.dot_general(q4, kg, (((1,), (1,)), ((), ())), preferred_element_type=F32)
            ps, dens = [], []
            for hh in range(GQA_GROUP):
                sk = sink_ref[g * GQA_GROUP + hh]
                sh = s[hh * tq:(hh + 1) * tq] + bias
                mx = jnp.maximum(jnp.max(sh, axis=-1, keepdims=True), sk)
                pr = jnp.exp(sh - mx)
                dens.append(jnp.sum(pr, axis=-1, keepdims=True) + jnp.exp(sk - mx))
                ps.append(pr.astype(BF16))
            pv = jnp.dot(jnp.concatenate(ps, axis=0), vg, preferred_element_type=F32)
            oh = [pv[hh * tq:(hh + 1) * tq] / dens[hh] for hh in range(GQA_GROUP)]
            out_cols.append(jnp.where(lo, oh[0], oh[1]))
            out_cols.append(jnp.where(lo, oh[2], oh[3]))
        return jnp.concatenate(out_cols, axis=1).astype(BF16)

    def rows_of(ref, r0):
        return lambda gs: ref[0, r0:r0 + tq, gs]

    t2 = 2 * pl.program_id(1)
    o = jnp.concatenate([
        attend(t2, slice(0, tq), [rows_of(kp_ref, 0), rows_of(km_ref, 0), rows_of(km_ref, tq)],
               [rows_of(vp_ref, 0), rows_of(vm_ref, 0), rows_of(vm_ref, tq)]),
        attend(t2 + 1, slice(tq, 2 * tq), [rows_of(km_ref, 0), rows_of(km_ref, tq), rows_of(kn_ref, 0)],
               [rows_of(vm_ref, 0), rows_of(vm_ref, tq), rows_of(vn_ref, 0)])], axis=0)
    y = jnp.dot(o, wo_ref[...], preferred_element_type=F32)
    m = mod_ref[0, 0]
    x_new = x_ref[0] + m[2:3] * y
    o_ref[0] = x_new
    _route_tile(x_new, m, gffn_ref, wr_ref, rb_ref, carry, first,
                h_ref, e0_ref, e1_ref, p0_ref, p1_ref, gt_ref, cnt_ref)


def _attn_core(x, mods, qkv, p, route_args, ctx_len):
    b, n, d = x.shape
    q, k, v = qkv
    nk = k.shape[-1]
    assert ctx_len == TR and TR == 2 * TQ
    ntq = n // TQ

    prev_spec = pl.BlockSpec((1, TQ, nk), lambda bb, t, s: (bb, jnp.maximum(2 * t - 1, 0), 0))
    next_spec = pl.BlockSpec((1, TQ, nk), lambda bb, t, s: (bb, jnp.minimum(2 * t + 2, ntq - 1), 0))
    ctx_spec = pl.BlockSpec((1, ctx_len, nk), lambda bb, t, s: (bb, 0, 0))
    r_in, r_out, r_shape, r_scratch = _route_io(b, n, d, TR, lambda t: t)
    grid_spec = pltpu.PrefetchScalarGridSpec(
        num_scalar_prefetch=1, grid=(b, n // TR),
        in_specs=[_row_spec(TR, d), _seg_mod_spec(d), _row_spec(TR, d),
                  ctx_spec, ctx_spec, prev_spec, _row_spec(TR, nk), next_spec, prev_spec, _row_spec(TR, nk), next_spec,
                  _full_spec((d, d))] + r_in,
        out_specs=[_row_spec(TR, d)] + r_out,
        scratch_shapes=r_scratch)
    return pl.pallas_call(
        functools.partial(_attn_core_kernel, ctx_len=ctx_len, n_rows=n),
        grid_spec=grid_spec,
        out_shape=[jax.ShapeDtypeStruct((b, n, d), F32)] + r_shape,
        compiler_params=_params(("arbitrary", "arbitrary")),
        name="attn_core",
    )(p["attn_sink"].astype(F32), x, mods, q, k, v, k, k, k, v, v, v, p["attn_w_o"].astype(BF16), *route_args)


def _lru_gates_scan(back, t, u_ref, cw_ref, cb_ref, wg_ref, ba_ref, bx_ref, lam_ref,
                    ubuf, halo, hcar, abuf, bbuf, hbuf):
    tr, d = u_ref.shape[1], u_ref.shape[2]
    nb = d // LRU_BLOCKS

    @pl.when(t <= 1)
    def _():
        halo[...] = jnp.zeros_like(halo)

    @pl.when(t == 0)
    def _():
        hcar[...] = jnp.zeros_like(hcar)

    u = u_ref[0]
    if not back:
        ubuf[0:SUBLANES, :] = halo[...]
        ubuf[SUBLANES:SUBLANES + tr, :] = u
        halo[...] = u[tr - SUBLANES:, :]
        taps = [ubuf[SUBLANES - (LRU_CONV_W - 1) + k:SUBLANES - (LRU_CONV_W - 1) + k + tr, :]
                for k in range(LRU_CONV_W)]
    else:
        ubuf[0:tr, :] = u
        ubuf[tr:tr + SUBLANES, :] = halo[...]
        halo[...] = u[:SUBLANES, :]
        taps = [ubuf[(LRU_CONV_W - 1) - k:(LRU_CONV_W - 1) - k + tr, :] for k in range(LRU_CONV_W)]
    cc = cb_ref[...] + taps[0] * cw_ref[0:1, :]
    for k in range(1, LRU_CONV_W):
        cc = cc + taps[k] * cw_ref[k:k + 1, :]

    ccb = cc.astype(BF16)
    lam = lam_ref[...]
    neg_c_softplus = -LRU_C * (jnp.maximum(-lam, 0.0) + jnp.log(1.0 + jnp.exp(-jnp.abs(lam))))
    for blk in range(LRU_BLOCKS):
        cs = slice(blk * nb, (blk + 1) * nb)
        z = jnp.dot(ccb[:, cs], wg_ref[blk], preferred_element_type=F32)
        r = jax.nn.sigmoid(z[:, :nb] + ba_ref[:, cs])
        gi = jax.nn.sigmoid(z[:, nb:] + bx_ref[:, cs])
        log_a = neg_c_softplus[:, cs] * r
        a = jnp.exp(log_a)
        abuf[:, cs] = a
        bbuf[:, cs] = jnp.sqrt(1.0 - a * a) * (gi * cc[:, cs])

    row = lax.broadcasted_iota(I32, (SUBLANES, d), 0)
    nchunk = tr // SUBLANES

    def chunk(i, h):
        ci = (nchunk - 1 - i) if back else i
        r0 = pl.multiple_of(ci * SUBLANES, SUBLANES)
        a = abuf[pl.ds(r0, SUBLANES), :]
        bb = bbuf[pl.ds(r0, SUBLANES), :]
        for k in (1, 2, 4):
            if back:
                sh, ok = SUBLANES - k, row < SUBLANES - k
            else:
                sh, ok = k, row >= k
            a_s = pltpu.roll(a, sh, 0)
            b_s = pltpu.roll(bb, sh, 0)
            bb = jnp.where(ok, a * b_s + bb, bb)
            a = jnp.where(ok, a * a_s, a)
        hh = a * h + bb
        hbuf[pl.ds(r0, SUBLANES), :] = hh
        return hh[0:1, :] if back else hh[SUBLANES - 1:SUBLANES, :]

    hcar[0:1, :] = lax.fori_loop(0, nchunk, chunk, hcar[0:1, :])


def _lru_fwd_kernel(u_ref, cw_ref, cb_ref, wg_ref, ba_ref, bx_ref, lam_ref, hf_ref,
                    ubuf, halo, hcar, abuf, bbuf, hbuf):
    _lru_gates_scan(False, pl.program_id(1), u_ref, cw_ref, cb_ref, wg_ref, ba_ref, bx_ref, lam_ref,
                    ubuf, halo, hcar, abuf, bbuf, hbuf)
    hf_ref[0] = hbuf[...]


def _lru_bwd_kernel(u_ref, cw_ref, cb_ref, wg_ref, ba_ref, bx_ref, lam_ref, hf_ref, gg_ref, x_ref, mod_ref,
                    wo_ref, gffn_ref, wr_ref, rb_ref, o_ref, h_ref, e0_ref, e1_ref, p0_ref, p1_ref, gt_ref, cnt_ref,
                    ubuf, halo, hcar, abuf, bbuf, hbuf, carry):
    first = _first_step()
    _lru_gates_scan(True, pl.program_id(1), u_ref, cw_ref, cb_ref, wg_ref, ba_ref, bx_ref, lam_ref,
                    ubuf, halo, hcar, abuf, bbuf, hbuf)
    y = (hf_ref[0] + hbuf[...]) * gg_ref[0]
    out = jnp.dot(y.astype(BF16), wo_ref[...], preferred_element_type=F32)
    m = mod_ref[0, 0]
    x_new = x_ref[0] + m[2:3] * out
    o_ref[0] = x_new
    _route_tile(x_new, m, gffn_ref, wr_ref, rb_ref, carry, first,
                h_ref, e0_ref, e1_ref, p0_ref, p1_ref, gt_ref, cnt_ref)


def _lru_core(x, mods, ugg, p, route_args):
    b, n, d = x.shape
    u, gg = ugg
    nt = n // TR
    nb = d // LRU_BLOCKS
    wg = jnp.concatenate([p["lru_wa"], p["lru_wx"]], axis=-1).astype(BF16)
    scratch = [pltpu.VMEM((TR + SUBLANES, d), F32), pltpu.VMEM((SUBLANES, d), F32),
               pltpu.VMEM((SUBLANES, d), F32), pltpu.VMEM((TR, d), F32), pltpu.VMEM((TR, d), F32),
               pltpu.VMEM((TR, d), F32)]

    def gate_specs(order):
        return [pl.BlockSpec((1, TR, d), lambda bb, t: (bb, order(t), 0)),
                _full_spec((LRU_CONV_W, d)), _full_spec((1, d)), _full_spec((LRU_BLOCKS, nb, 2 * nb)),
                _full_spec((1, d)), _full_spec((1, d)), _full_spec((1, d))]

    def gate_args(dd):
        return (u, p["lru_conv_w"][dd], p["lru_conv_b"][dd].reshape(1, d), wg[dd], p["lru_ba"][dd].reshape(1, d),
                p["lru_bx"][dd].reshape(1, d), p["lru_lam"][dd].reshape(1, d))

    hf = pl.pallas_call(
        _lru_fwd_kernel, grid=(b, nt),
        in_specs=gate_specs(lambda t: t),
        out_specs=_row_spec(TR, d),
        out_shape=jax.ShapeDtypeStruct((b, n, d), F32),
        scratch_shapes=scratch,
        compiler_params=_params(("arbitrary", "arbitrary")),
        name="lru_fwd",
    )(*gate_args(0))

    def rev(t):
        return jnp.where(t == 0, 0, nt - t)

    def rev_spec():
        return pl.BlockSpec((1, TR, d), lambda bb, t: (bb, rev(t), 0))

    r_in, r_out, r_shape, r_scratch = _route_io(b, n, d, TR, rev)
    return pl.pallas_call(
        _lru_bwd_kernel, grid=(b, nt),
        in_specs=gate_specs(rev) + [rev_spec(), rev_spec(), rev_spec(), _seg_mod_spec(d), _full_spec((d, d))] + r_in,
        out_specs=[rev_spec()] + r_out,
        out_shape=[jax.ShapeDtypeStruct((b, n, d), F32)] + r_shape,
        scratch_shapes=scratch + r_scratch,
        compiler_params=_params(("arbitrary", "arbitrary")),
        name="lru_bwd",
    )(*gate_args(1), hf, gg, x, mods, p["lru_w_out"].astype(BF16), *route_args)


def _dispatch_kernel(slot_ref, zrow_ref, nu_ref, h_ref, xs_ref, ring, zbuf, sem, zsem, *, n_tok, n_blocks):
    s = pl.program_id(0)
    ns = pl.num_programs(0)
    tr = h_ref.shape[0] // SUBLANES
    blk = zbuf.shape[0] // SUBLANES
    par = s % 2

    @pl.when(s == 0)
    def _():
        zbuf[...] = jnp.zeros_like(zbuf)
        for e in range(N_EXPERTS):
            pltpu.make_async_copy(zbuf, _tile_rows(xs_ref, zrow_ref[e], blk), zsem).start()
        for e in range(N_EXPERTS):
            pltpu.make_async_copy(zbuf, _tile_rows(xs_ref, 0, blk), zsem).wait()

        def zero_blk(i, carry):
            pltpu.make_async_copy(zbuf, _tile_rows(xs_ref, i * blk, blk), zsem).start()
            return carry

        def zero_blk_wait(i, carry):
            pltpu.make_async_copy(zbuf, _tile_rows(xs_ref, 0, blk), zsem).wait()
            return carry

        lax.fori_loop(nu_ref[0], n_blocks, zero_blk, 0)
        lax.fori_loop(nu_ref[0], n_blocks, zero_blk_wait, 0)

    def wait_ring(p):
        for _ in range(TOP_K):
            pltpu.make_async_copy(ring.at[p], _tile_rows(xs_ref, 0, tr), sem.at[p]).wait()

    @pl.when(s >= 2)
    def _():
        wait_ring(par)

    ring[par] = h_ref[...]
    base = s * tr

    def issue(i, carry):
        for k in range(TOP_K):
            pltpu.make_async_copy(_tile_rows(ring.at[par], i, 1),
                                  _tile_rows(xs_ref, slot_ref[k * n_tok + base + i], 1),
                                  sem.at[par]).start(priority=k)
        return carry

    lax.fori_loop(0, tr, issue, 0, unroll=8)

    @pl.when(s == ns - 1)
    def _():
        wait_ring(par)

        @pl.when(ns >= 2)
        def _():
            wait_ring(1 - par)


def _expert_kernel(be_ref, nu_ref, xs_ref, w1_ref, w3_ref, w2_ref, ys_ref, w1b, w3b, w2b):
    i = pl.program_id(0)
    used = i < nu_ref[0]
    new_expert = jnp.logical_or(i == 0, be_ref[i] != be_ref[jnp.maximum(i - 1, 0)])

    @pl.when(jnp.logical_and(used, new_expert))
    def _():
        w1b[...] = w1_ref[0, 0].astype(BF16)
        w3b[...] = w3_ref[0, 0].astype(BF16)
        w2b[...] = w2_ref[0, 0].astype(BF16)

    @pl.when(used)
    def _():
        xb = _load_token_tiles(xs_ref, xs_ref.shape[0] // SUBLANES).astype(BF16)
        a = jnp.dot(xb, w1b[...], preferred_element_type=F32)
        b = jnp.dot(xb, w3b[...], preferred_element_type=F32)
        hid = (a * jax.nn.sigmoid(a)) * b
        _store_token_tiles(ys_ref, jnp.dot(hid.astype(BF16), w2b[...], preferred_element_type=F32))

    @pl.when(jnp.logical_not(used))
    def _():
        ys_ref[...] = jnp.zeros_like(ys_ref)


def _combine_kernel(slot_ref, x_ref, mod_ref, gt_ref, ys_ref, *refs, n_tok, rows_per_batch, tile_off, tail, n_tail_in):
    if tail is None:
        nf_ref, o_ref, ybuf, sem = refs
    else:
        modn_ref, gn_ref = refs[0], refs[1]
        tail_in = refs[2:2 + n_tail_in]
        o_ref = refs[2 + n_tail_in]
        tail_out = refs[3 + n_tail_in:-2]
        ybuf, sem = refs[-2], refs[-1]
    bb, t = pl.program_id(0), pl.program_id(1)
    nt = pl.num_programs(1)
    tr = x_ref.shape[1]
    step = bb * nt + t
    nsteps = pl.num_programs(0) * nt

    def tok_base(s):
        return (s // nt) * rows_per_batch + (s % nt + tile_off) * tr

    def issue(s):
        par = s % 2
        base = tok_base(s)

        def one(i, carry):
            for k in range(TOP_K):
                pltpu.make_async_copy(_tile_rows(ys_ref, slot_ref[k * n_tok + base + i], 1),
                                      _tile_rows(ybuf.at[par, k], i, 1), sem.at[par]).start(priority=k)
            return carry

        lax.fori_loop(0, tr, one, 0, unroll=8)

    @pl.when(step == 0)
    def _():
        issue(step)

    @pl.when(step + 1 < nsteps)
    def _():
        issue(step + 1)

    par = step % 2
    for k in range(TOP_K):
        pltpu.make_async_copy(_tile_rows(ys_ref, 0, tr), ybuf.at[par, k], sem.at[par]).wait()

    gt = gt_ref[0]
    y = (gt[:, 0:1] * _load_token_tiles(ybuf.at[par, 0], tr)
         + gt[:, 1:2] * _load_token_tiles(ybuf.at[par, 1], tr))
    out = x_ref[0] + mod_ref[0, 0][5:6] * y
    if tail is None:
        o_ref[0] = out * lax.rsqrt(jnp.mean(out * out, axis=-1, keepdims=True) + EPS) * nf_ref[...]
    else:
        o_ref[0] = out
        mn = modn_ref[0, 0]
        tail(_norm_mod(out, gn_ref[...], mn[0:1], mn[1:2]), *tail_in, *tail_out)


def _moe(x, mods, routed, layer, w1, w3, w2, ctx_len, norm_f=None, next_ip=None, next_mods=None, next_g=None):
    b, n, d = x.shape
    nt = n // TR
    n_tok = b * n
    de = w1.shape[-1]
    h, e0, e1, p0, p1, gtab, counts = routed

    counts = counts[:, 0]
    padded = (counts + MOE_BLK - 1) // MOE_BLK * MOE_BLK
    pend = jnp.cumsum(padded)
    pstart = (pend - padded).astype(I32)
    n_slots = n_tok * TOP_K + N_EXPERTS * MOE_BLK
    n_blocks = n_slots // MOE_BLK
    n_used = (pend[-1] // MOE_BLK).astype(I32).reshape(1)
    blk_start = jnp.arange(n_blocks, dtype=I32) * MOE_BLK
    blk_e = jnp.minimum(jnp.sum((pend[None, :] <= blk_start[:, None]).astype(I32), axis=1), N_EXPERTS - 1)

    def slot_of(e, p):
        e, p = e.reshape(n_tok), p.reshape(n_tok)
        start = jnp.zeros_like(e)
        for j in range(N_EXPERTS):
            start = jnp.where(e == j, pstart[j], start)
        return start + p

    slots = jnp.concatenate([slot_of(e0, p0), slot_of(e1, p1)])
    zrow = (pstart + counts).astype(I32)

    xs = pl.pallas_call(
        functools.partial(_dispatch_kernel, n_tok=n_tok, n_blocks=n_blocks),
        grid_spec=pltpu.PrefetchScalarGridSpec(
            num_scalar_prefetch=3, grid=(n_tok // TR,),
            in_specs=[pl.BlockSpec((TR * SUBLANES, LANES), lambda s, *_: (s, 0))],
            out_specs=pl.BlockSpec(memory_space=pl.ANY),
            scratch_shapes=[pltpu.VMEM((2, TR * SUBLANES, LANES), F32), pltpu.VMEM((MOE_BLK * SUBLANES, LANES), F32),
                            pltpu.SemaphoreType.DMA((2,)), pltpu.SemaphoreType.DMA]),
        out_shape=jax.ShapeDtypeStruct((n_slots * SUBLANES, LANES), F32),
        compiler_params=_params(("arbitrary",)),
        name="moe_dispatch",
    )(slots, zrow, n_used, h)

    def wspec(shape):
        return pl.BlockSpec((1, 1) + shape, lambda i, be, nu: (layer, be[i], 0, 0))

    def blk_spec():
        return pl.BlockSpec((MOE_BLK * SUBLANES, LANES), lambda i, be, nu: (i, 0))

    ys = pl.pallas_call(
        _expert_kernel,
        grid_spec=pltpu.PrefetchScalarGridSpec(
            num_scalar_prefetch=2, grid=(n_blocks,),
            in_specs=[blk_spec(), wspec((d, de)), wspec((d, de)), wspec((de, d))],
            out_specs=blk_spec(),
            scratch_shapes=[pltpu.VMEM((d, de), BF16), pltpu.VMEM((d, de), BF16), pltpu.VMEM((de, d), BF16)]),
        out_shape=jax.ShapeDtypeStruct((n_slots * SUBLANES, LANES), F32),
        compiler_params=_params(("arbitrary",)),
        name="moe_experts",
    )(blk_e, n_used, xs, w1, w3, w2)

    final = next_ip is None
    tile_off = ctx_len // TR if final else 0
    nt_out = nt - tile_off
    common_specs = [_row_spec(TR, d, tile_off), _seg_mod_spec(d, tile_off=tile_off), _row_spec(TR, LANES, tile_off),
                    pl.BlockSpec(memory_space=pl.ANY)]
    scratch = [pltpu.VMEM((2, TOP_K, TR * SUBLANES, LANES), F32), pltpu.SemaphoreType.DMA((2,))]
    stream_spec = _row_spec(TR, d)
    stream_shape = jax.ShapeDtypeStruct((b, nt_out * TR, d), F32)
    if final:
        return pl.pallas_call(
            functools.partial(_combine_kernel, n_tok=n_tok, rows_per_batch=n, tile_off=tile_off, tail=None,
                              n_tail_in=0),
            grid_spec=pltpu.PrefetchScalarGridSpec(
                num_scalar_prefetch=1, grid=(b, nt_out),
                in_specs=common_specs + [_full_spec((1, d))],
                out_specs=stream_spec, scratch_shapes=scratch),
            out_shape=stream_shape,
            compiler_params=_params(("arbitrary", "arbitrary")),
            name="moe_combine_final",
        )(slots, x, mods, gtab, ys, norm_f.reshape(1, d))
    outs = pl.pallas_call(
        functools.partial(_combine_kernel, n_tok=n_tok, rows_per_batch=n, tile_off=tile_off, tail=next_ip["tail"],
                          n_tail_in=len(next_ip["args"])),
        grid_spec=pltpu.PrefetchScalarGridSpec(
            num_scalar_prefetch=1, grid=(b, nt_out),
            in_specs=common_specs + [_seg_mod_spec(d), _full_spec((1, d))] + next_ip["in_specs"],
            out_specs=[stream_spec] + next_ip["out_specs"], scratch_shapes=scratch),
        out_shape=[stream_shape] + next_ip["out_shape"],
        compiler_params=_params(("arbitrary", "arbitrary")),
        name="moe_combine_in_proj",
    )(slots, x, mods, gtab, ys, next_mods, next_g.reshape(1, d), *next_ip["args"])
    return outs[0], outs[1:]


def kernel(x, c, ctx, c_ctx, w_mod, b_mod, norm_mix, norm_ffn, norm_f, conv_pw1, conv_pw1_b, conv_dw, conv_dw_b, conv_ln_g, conv_ln_b, conv_pw2, conv_pw2_b, attn_w_qkv, attn_w_o, attn_sink, lru_w_in, lru_conv_w, lru_conv_b, lru_wa, lru_ba, lru_wx, lru_bx, lru_lam, lru_w_out, moe_w_router, moe_router_bias, moe_w1, moe_w3, moe_w2):
    b, s, d = x.shape
    ctx_len = ctx.shape[1]
    depth = w_mod.shape[0]
    n = ctx_len + s
    assert ctx_len == TR and s % TR == 0 and d == SUBLANES * LANES
    assert moe_w_router.shape[1] == N_EXPERTS

    mods = _modulation(c, c_ctx, w_mod, b_mod)
    xs = jnp.concatenate([ctx, x], axis=1)

    wr = jnp.zeros((EXPERTS_PER_GROUP, SUBLANES, d), F32).at[:, :N_GROUPS].set(
        moe_w_router.T.reshape(N_GROUPS, EXPERTS_PER_GROUP, d).swapaxes(0, 1))
    wr = wr.reshape(ROUTER_ROWS, d).astype(BF16)
    rb = jnp.zeros((EXPERTS_PER_GROUP, SUBLANES), F32).at[:, :N_GROUPS].set(
        moe_router_bias.astype(F32).reshape(N_GROUPS, EXPERTS_PER_GROUP).T).reshape(-1, 1)

    def layer_params(i):
        slot = i // N_MIXERS
        return [dict(conv_pw1=conv_pw1, conv_pw1_b=conv_pw1_b, conv_dw=conv_dw, conv_dw_b=conv_dw_b,
                     conv_ln_g=conv_ln_g, conv_ln_b=conv_ln_b, conv_pw2=conv_pw2, conv_pw2_b=conv_pw2_b),
                dict(attn_w_qkv=attn_w_qkv, attn_w_o=attn_w_o, attn_sink=attn_sink),
                dict(lru_w_in=lru_w_in, lru_conv_w=lru_conv_w, lru_conv_b=lru_conv_b, lru_wa=lru_wa, lru_ba=lru_ba,
                     lru_wx=lru_wx, lru_bx=lru_bx, lru_lam=lru_lam, lru_w_out=lru_w_out)][i % N_MIXERS], slot

    def params_of(i):
        group, slot = layer_params(i)
        return {k: v[slot] for k, v in group.items()}

    p = params_of(0)
    ip = _in_proj(0, p, b, n, d, ctx_len)
    a_out = _in_proj_standalone(ip, xs, mods[0], norm_mix[0])
    if not isinstance(a_out, (list, tuple)):
        a_out = [a_out]
    for i in range(depth):
        kind = i % N_MIXERS
        route_args = (norm_ffn[i].reshape(1, d), wr, rb)
        if kind == 0:
            res = _conv_core(xs, mods[i], a_out[0], p, route_args)
        elif kind == 1:
            res = _attn_core(xs, mods[i], a_out, p, route_args, ctx_len)
        else:
            res = _lru_core(xs, mods[i], a_out, p, route_args)
        x_new, routed = res[0], res[1:]
        if i == depth - 1:
            return _moe(x_new, mods[i], routed, i, moe_w1, moe_w3, moe_w2, ctx_len, norm_f=norm_f)
        p = params_of(i + 1)
        ip = _in_proj((i + 1) % N_MIXERS, p, b, n, d, ctx_len)
        xs, a_out = _moe(x_new, mods[i], routed, i, moe_w1, moe_w3, moe_w2, ctx_len,
                         next_ip=ip, next_mods=mods[i + 1], next_g=norm_mix[i + 1])
```

```python
import functools

import jax
import jax.numpy as jnp
from jax import lax
from jax.experimental import pallas as pl
from jax.experimental.pallas import tpu as pltpu

F32 = jnp.float32
BF16 = jnp.bfloat16
I32 = jnp.int32

EPS = 1e-6
N_MOD = 6
N_MIXERS = 3
GRID_W = 64
CONV_WIDTH = 31
CONV_HALF = (CONV_WIDTH - 1) // 2
HEAD_DIM = 64
N_KV_HEADS = 4
GQA_GROUP = 4
WINDOW = 128
ROPE_BASE = 10000.0
ROPE_PAIRS = HEAD_DIM // 4
LRU_BLOCKS = 8
LRU_CONV_W = 4
LRU_C = 8.0
N_EXPERTS = 16
N_GROUPS = 4
EXPERTS_PER_GROUP = 4
TOP_K = 2

LANES = 128
SUBLANES = 8
TR = 256
TQ = 128
HALO = 16
MOE_BLK = 512
COMBINE_RING = 3
ROUTER_ROWS = SUBLANES * EXPERTS_PER_GROUP
NEG = -1e30
VMEM_LIMIT = 56 * 1024 * 1024


def _params(sem, vmem=VMEM_LIMIT):
    return pltpu.CompilerParams(dimension_semantics=sem, vmem_limit_bytes=vmem)


def _norm_mod(x, g, shift, scale):
    y = x * lax.rsqrt(jnp.mean(x * x, axis=-1, keepdims=True) + EPS)
    return (y * g) * (1.0 + scale) + shift


def _seg_mod_spec(d, tiles_per_seg=1, tile_off=0):
    return pl.BlockSpec((1, 1, N_MOD, d),
                        lambda b, t, *_: (b, jnp.minimum((t + tile_off) // tiles_per_seg, 1), 0, 0))


def _row_spec(tr, d, tile_off=0):
    return pl.BlockSpec((1, tr, d), lambda b, t, *_: (b, t + tile_off, 0))


def _full_spec(shape):
    nd = len(shape)
    return pl.BlockSpec(shape, lambda b, t, *_: (0,) * nd)


def _mod_kernel(c_ref, w_ref, b_ref, o_ref):
    c = c_ref[...]
    sc = c * jax.nn.sigmoid(c)
    o_ref[0] = jnp.dot(sc.astype(BF16), w_ref[0].astype(BF16), preferred_element_type=F32) + b_ref[0]


def _modulation(c, c_ctx, w_mod, b_mod):
    depth, d, nout = w_mod.shape
    b = c.shape[0]
    assert b + 1 <= SUBLANES
    rows = jnp.zeros((SUBLANES, d), F32).at[:b].set(c).at[b].set(c_ctx)
    tn = 1536
    raw = pl.pallas_call(
        _mod_kernel,
        grid=(depth, nout // tn),
        in_specs=[pl.BlockSpec((SUBLANES, d), lambda i, j: (0, 0)),
                  pl.BlockSpec((1, d, tn), lambda i, j: (i, 0, j)),
                  pl.BlockSpec((1, 1, tn), lambda i, j: (i, 0, j))],
        out_specs=pl.BlockSpec((1, SUBLANES, tn), lambda i, j: (i, 0, j)),
        out_shape=jax.ShapeDtypeStruct((depth, SUBLANES, nout), F32),
        compiler_params=_params(("arbitrary", "arbitrary")),
        name="modulation",
    )(rows, w_mod, b_mod.reshape(depth, 1, nout))
    raw = raw.reshape(depth, SUBLANES, N_MOD, d)
    lat = raw[:, :b]
    ctx = jnp.broadcast_to(raw[:, b][:, None], lat.shape)
    return jnp.stack([ctx, lat], axis=2)


def _store_token_tiles(ref, mat):
    rows = mat.shape[0]
    for s in range(SUBLANES):
        ref[pl.ds(s, rows, stride=SUBLANES), :] = mat[:, s * LANES:(s + 1) * LANES]


def _load_token_tiles(ref, rows):
    return jnp.concatenate([ref[pl.ds(s, rows, stride=SUBLANES), :] for s in range(SUBLANES)], axis=1)


def _tile_rows(ref, first_token, n_tokens):
    first_row = first_token * SUBLANES
    if not isinstance(first_row, int):
        first_row = pl.multiple_of(first_row, SUBLANES)
    return ref.at[pl.ds(first_row, n_tokens * SUBLANES)]


def _route_tile(x, m, g_ref, wr_ref, rb_ref, carry, first, h_ref, e0_ref, e1_ref, p0_ref, p1_ref, gt_ref, cnt_ref):
    tr = x.shape[0]

    @pl.when(first)
    def _():
        carry[...] = jnp.zeros_like(carry)

    h = _norm_mod(x, g_ref[...], m[3:4], m[4:5])
    _store_token_tiles(h_ref, h)
    logit = lax.dot_general(wr_ref[...], h.astype(BF16), (((1,), (1,)), ((), ())),
                            preferred_element_type=F32)
    s = jax.nn.sigmoid(logit)
    sel = s + rb_ref[...]
    sj = [s[SUBLANES * j:SUBLANES * (j + 1)] for j in range(EXPERTS_PER_GROUP)]
    cj = [sel[SUBLANES * j:SUBLANES * (j + 1)] for j in range(EXPERTS_PER_GROUP)]

    hi1, lo1 = jnp.maximum(cj[0], cj[1]), jnp.minimum(cj[0], cj[1])
    hi2, lo2 = jnp.maximum(cj[2], cj[3]), jnp.minimum(cj[2], cj[3])
    top1 = jnp.maximum(hi1, hi2)
    top2 = jnp.maximum(jnp.minimum(hi1, hi2), jnp.maximum(lo1, lo2))
    row = lax.broadcasted_iota(I32, (SUBLANES, tr), 0)
    gscore = jnp.where(row < N_GROUPS, top1 + top2, -jnp.inf)
    gmax = jnp.max(gscore, axis=0, keepdims=True)
    grp = jnp.min(jnp.where(gscore == gmax, row, SUBLANES), axis=0, keepdims=True)
    pick = row == grp
    c = [jnp.sum(jnp.where(pick, v, 0.0), axis=0, keepdims=True) for v in cj]
    w = [jnp.sum(jnp.where(pick, v, 0.0), axis=0, keepdims=True) for v in sj]

    def argtop(vals):
        best, bi, bw = vals[0], jnp.zeros_like(grp), w[0]
        for j in range(1, EXPERTS_PER_GROUP):
            better = vals[j] > best
            best = jnp.where(better, vals[j], best)
            bi = jnp.where(better, j, bi)
            bw = jnp.where(better, w[j], bw)
        return bi, bw

    i0, w0 = argtop(c)
    i1, w1 = argtop([jnp.where(i0 == j, -jnp.inf, c[j]) for j in range(EXPERTS_PER_GROUP)])
    e0 = grp * EXPERTS_PER_GROUP + i0
    e1 = grp * EXPERTS_PER_GROUP + i1
    wsum = w0 + w1
    g0 = w0 / wsum
    g1 = w1 / wsum

    erow = lax.broadcasted_iota(I32, (N_EXPERTS, tr), 0)
    member = jnp.logical_or(erow == e0, erow == e1)
    upper = (lax.broadcasted_iota(I32, (tr, tr), 0) < lax.broadcasted_iota(I32, (tr, tr), 1))
    before = jnp.dot(member.astype(BF16), upper.astype(BF16), preferred_element_type=F32) + carry[...]
    p0 = jnp.sum(jnp.where(erow == e0, before, 0.0), axis=0, keepdims=True)
    p1 = jnp.sum(jnp.where(erow == e1, before, 0.0), axis=0, keepdims=True)
    total = carry[...] + jnp.sum(member.astype(F32), axis=1, keepdims=True)
    carry[...] = total
    cnt_ref[...] = total.astype(I32)

    e0_ref[0] = e0
    e1_ref[0] = e1
    p0_ref[0] = p0.astype(I32)
    p1_ref[0] = p1.astype(I32)
    grow = lax.broadcasted_iota(I32, (LANES, tr), 0)
    gmat = jnp.where(grow == 0, g0, jnp.where(grow == 1, g1, 0.0))
    gt_ref[0] = gmat.T


def _route_io(b, n, d, tile, tile_of):
    nt = n // tile

    def flat(bb, t):
        return bb * nt + tile_of(t)

    lane_major = pl.BlockSpec((1, 1, tile), lambda bb, t, *_: (flat(bb, t), 0, 0))
    lane_major_shape = jax.ShapeDtypeStruct((b * nt, 1, tile), I32)
    in_specs = [_full_spec((1, d)), _full_spec((ROUTER_ROWS, d)), _full_spec((ROUTER_ROWS, 1))]
    out_specs = [pl.BlockSpec((tile * SUBLANES, LANES), lambda bb, t, *_: (flat(bb, t), 0)),
                 lane_major, lane_major, lane_major, lane_major,
                 pl.BlockSpec((1, tile, LANES), lambda bb, t, *_: (bb, tile_of(t), 0)),
                 pl.BlockSpec((N_EXPERTS, 1), lambda bb, t, *_: (0, 0))]
    out_shape = [jax.ShapeDtypeStruct((b * n * SUBLANES, LANES), F32), lane_major_shape, lane_major_shape,
                 lane_major_shape, lane_major_shape,
                 jax.ShapeDtypeStruct((b, n, LANES), F32), jax.ShapeDtypeStruct((N_EXPERTS, 1), I32)]
    scratch = [pltpu.VMEM((N_EXPERTS, 1), F32)]
    return in_specs, out_specs, out_shape, scratch


def _first_step():
    return jnp.logical_and(pl.program_id(0) == 0, pl.program_id(1) == 0)


def _glu_tail(h, w_ref, b_ref, o_ref):
    d = h.shape[-1]
    u = jnp.dot(h.astype(BF16), w_ref[...], preferred_element_type=F32) + b_ref[...]
    o_ref[0] = u[:, :d] * jax.nn.sigmoid(u[:, d:])


def _qkv_tail(h, w_ref, cos_ref, sin_ref, q_ref, k_ref, v_ref):
    d = h.shape[-1]
    nk = k_ref.shape[-1]
    t = jnp.dot(h.astype(BF16), w_ref[...], preferred_element_type=F32)
    cos = cos_ref[...]
    sin = sin_ref[...]
    lane = lax.broadcasted_iota(I32, cos.shape, 1)
    first_half = (lane % (2 * ROPE_PAIRS)) < ROPE_PAIRS

    def rope(xg):
        partner = jnp.where(first_half, pltpu.roll(xg, LANES - ROPE_PAIRS, 1), pltpu.roll(xg, ROPE_PAIRS, 1))
        return xg * cos + partner * sin

    for j in range(d // LANES):
        cs = slice(j * LANES, (j + 1) * LANES)
        q_ref[0, :, cs] = rope(t[:, cs]).astype(BF16)
    for j in range(nk // LANES):
        cs = slice(j * LANES, (j + 1) * LANES)
        k_ref[0, :, cs] = rope(t[:, d + j * LANES:d + (j + 1) * LANES]).astype(BF16)
    v_ref[0] = t[:, d + nk:].astype(BF16)


def _lru_in_tail(h, w_ref, u_ref, gg_ref):
    d = h.shape[-1]
    t = jnp.dot(h.astype(BF16), w_ref[...], preferred_element_type=F32)
    u_ref[0] = t[:, :d]
    gg_ref[0] = jax.nn.gelu(t[:, d:])


def _rope_tables(n, ctx_len):
    pos = jnp.arange(n - ctx_len, dtype=I32)
    inv = ROPE_BASE ** (-jnp.arange(ROPE_PAIRS, dtype=F32) / ROPE_PAIRS)
    ar = (pos // GRID_W).astype(F32)[:, None] * inv
    ac = (pos % GRID_W).astype(F32)[:, None] * inv
    ang = jnp.concatenate([ar, ar, ac, ac], axis=-1)
    sign = jnp.tile(jnp.concatenate([-jnp.ones((ROPE_PAIRS,), F32), jnp.ones((ROPE_PAIRS,), F32)]), 2)
    cos = jnp.concatenate([jnp.ones((ctx_len, HEAD_DIM), F32), jnp.cos(ang)], axis=0)
    sin = jnp.concatenate([jnp.zeros((ctx_len, HEAD_DIM), F32), jnp.sin(ang) * sign], axis=0)
    return jnp.tile(cos, (1, 2)), jnp.tile(sin, (1, 2))


def _in_proj(kind, p, b, n, d, ctx_len):
    if kind == 0:
        args = (p["conv_pw1"].astype(BF16), p["conv_pw1_b"].reshape(1, 2 * d))
        return dict(tail=_glu_tail, args=args, in_specs=[_full_spec((d, 2 * d)), _full_spec((1, 2 * d))],
                    out_specs=[_row_spec(TR, d)], out_shape=[jax.ShapeDtypeStruct((b, n, d), F32)])
    if kind == 1:
        kvd = N_KV_HEADS * HEAD_DIM
        nk = N_KV_HEADS * LANES
        assert d == 2 * nk
        scale = HEAD_DIM ** -0.5
        w_qkv = p["attn_w_qkv"]

        def dup(w):
            w = w.reshape(d, N_KV_HEADS, HEAD_DIM)
            return jnp.concatenate([w, w], axis=-1).reshape(d, nk)

        w_all = jnp.concatenate([w_qkv[:, :d] * scale, dup(w_qkv[:, d:d + kvd]), dup(w_qkv[:, d + kvd:])],
                                axis=1).astype(BF16)
        cos, sin = _rope_tables(n, ctx_len)
        table = pl.BlockSpec((TR, LANES), lambda bb, t, *_: (t, 0))
        return dict(tail=_qkv_tail, args=(w_all, cos, sin), in_specs=[_full_spec((d, d + 2 * nk)), table, table],
                    out_specs=[_row_spec(TR, d), _row_spec(TR, nk), _row_spec(TR, nk)],
                    out_shape=[jax.ShapeDtypeStruct((b, n, d), BF16), jax.ShapeDtypeStruct((b, n, nk), BF16),
                               jax.ShapeDtypeStruct((b, n, nk), BF16)])
    return dict(tail=_lru_in_tail, args=(p["lru_w_in"].astype(BF16),), in_specs=[_full_spec((d, 2 * d))],
                out_specs=[_row_spec(TR, d), _row_spec(TR, d)],
                out_shape=[jax.ShapeDtypeStruct((b, n, d), F32), jax.ShapeDtypeStruct((b, n, d), F32)])


def _in_proj_kernel(x_ref, mod_ref, g_ref, *refs, tail):
    m = mod_ref[0, 0]
    tail(_norm_mod(x_ref[0], g_ref[...], m[0:1], m[1:2]), *refs)


def _in_proj_standalone(ip, x, mods, g):
    b, n, d = x.shape
    return pl.pallas_call(
        functools.partial(_in_proj_kernel, tail=ip["tail"]), grid=(b, n // TR),
        in_specs=[_row_spec(TR, d), _seg_mod_spec(d), _full_spec((1, d))] + ip["in_specs"],
        out_specs=ip["out_specs"], out_shape=ip["out_shape"],
        compiler_params=_params(("parallel", "parallel")),
        name="in_proj",
    )(x, mods, g.reshape(1, d), *ip["args"])


def _conv_core_kernel(x_ref, mod_ref, gp_ref, gc_ref, gn_ref, dw_ref, dwb_ref, lng_ref, lnb_ref, w_ref, b_ref,
                      gffn_ref, wr_ref, rb_ref, o_ref, h_ref, e0_ref, e1_ref, p0_ref, p1_ref, gt_ref, cnt_ref,
                      gbuf, cbuf, carry):
    t = pl.program_id(1)
    nt = pl.num_programs(1)
    first = _first_step()
    tr, d = gc_ref.shape[1], gc_ref.shape[2]
    seg_first = t <= 1
    seg_last = jnp.logical_or(t == 0, t == nt - 1)
    gbuf[0, 0:HALO, :] = jnp.where(seg_first, 0.0, gp_ref[0])
    gbuf[0, HALO:HALO + tr, :] = gc_ref[0]
    gbuf[0, HALO + tr:HALO + tr + HALO, :] = jnp.where(seg_last, 0.0, gn_ref[0])
    span = tr + 2 * HALO - SUBLANES
    for r in range(1, SUBLANES):
        gbuf[r, 0:span, :] = gbuf[0, r:r + span, :]

    rc = 32
    base = HALO - CONV_HALF

    def chunk(i, carry_):
        r0 = pl.multiple_of(i * rc, rc)
        for c in range(d // LANES):
            cs = slice(c * LANES, (c + 1) * LANES)
            acc = jnp.zeros((rc, LANES), F32)
            for k in range(CONV_WIDTH):
                off = base + k
                rows = pl.ds(r0 + (off // SUBLANES) * SUBLANES, rc)
                acc = acc + dw_ref[k:k + 1, cs] * gbuf[off % SUBLANES, rows, cs]
            cbuf[pl.ds(r0, rc), cs] = acc
        return carry_

    lax.fori_loop(0, tr // rc, chunk, 0)

    u = cbuf[...] + dwb_ref[...]
    mu = jnp.mean(u, axis=-1, keepdims=True)
    uc = u - mu
    var = jnp.mean(uc * uc, axis=-1, keepdims=True)
    v = uc * lax.rsqrt(var + EPS) * lng_ref[...] + lnb_ref[...]
    v = v * jax.nn.sigmoid(v)
    y = jnp.dot(v.astype(BF16), w_ref[...], preferred_element_type=F32) + b_ref[...]
    m = mod_ref[0, 0]
    x_new = x_ref[0] + m[2:3] * y
    o_ref[0] = x_new
    _route_tile(x_new, m, gffn_ref, wr_ref, rb_ref, carry, first,
                h_ref, e0_ref, e1_ref, p0_ref, p1_ref, gt_ref, cnt_ref)


def _conv_core(x, mods, glu, p, route_args):
    b, n, d = x.shape
    nt = n // TR
    hb = TR // HALO
    nh = n // HALO
    dwp = jnp.zeros((32, d), F32).at[:CONV_WIDTH].set(p["conv_dw"])
    r_in, r_out, r_shape, r_scratch = _route_io(b, n, d, TR, lambda t: t)
    return pl.pallas_call(
        _conv_core_kernel, grid=(b, nt),
        in_specs=[_row_spec(TR, d), _seg_mod_spec(d),
                  pl.BlockSpec((1, HALO, d), lambda bb, t: (bb, jnp.maximum(t * hb - 1, 0), 0)),
                  _row_spec(TR, d),
                  pl.BlockSpec((1, HALO, d), lambda bb, t: (bb, jnp.minimum((t + 1) * hb, nh - 1), 0)),
                  _full_spec((32, d)), _full_spec((1, d)), _full_spec((1, d)), _full_spec((1, d)),
                  _full_spec((d, d)), _full_spec((1, d))] + r_in,
        out_specs=[_row_spec(TR, d)] + r_out,
        out_shape=[jax.ShapeDtypeStruct((b, n, d), F32)] + r_shape,
        scratch_shapes=[pltpu.VMEM((SUBLANES, TR + 2 * HALO, d), F32), pltpu.VMEM((TR, d), F32)] + r_scratch,
        compiler_params=_params(("arbitrary", "arbitrary")),
        name="conv_core",
    )(x, mods, glu, glu, glu, dwp, p["conv_dw_b"].reshape(1, d), p["conv_ln_g"].reshape(1, d),
      p["conv_ln_b"].reshape(1, d), p["conv_pw2"].astype(BF16), p["conv_pw2_b"].reshape(1, d), *route_args)


def _attn_core_kernel(sink_ref, x_ref, mod_ref, q_ref, kc_ref, vc_ref, kp_ref, km_ref, kn_ref,
                      vp_ref, vm_ref, vn_ref, wo_ref, gffn_ref, wr_ref, rb_ref,
                      o_ref, h_ref, e0_ref, e1_ref, p0_ref, p1_ref, gt_ref, cnt_ref, carry, *, ctx_len, n_rows):
    first = _first_step()
    tq = kp_ref.shape[1]
    nctx = kc_ref.shape[1]
    nkeys = nctx + 3 * tq
    lane = lax.broadcasted_iota(I32, (tq, LANES), 1)
    lo = lane < HEAD_DIM
    col = lax.broadcasted_iota(I32, (tq, nkeys), 1)
    row = lax.broadcasted_iota(I32, (tq, nkeys), 0)

    def attend(t, q_rows, k_parts, v_parts):
        rq = t * tq + row
        rk = (t - 1) * tq + (col - nctx)
        win_ok = (rk >= ctx_len) & (rk < n_rows) & (jnp.abs(rq - rk) <= WINDOW) & (t * tq >= ctx_len)
        bias = jnp.where((col < nctx) | win_ok, 0.0, NEG).astype(F32)
        out_cols = []
        for g in range(N_KV_HEADS):
            gs = slice(g * LANES, (g + 1) * LANES)
            kg = jnp.concatenate([kc_ref[0, :, gs]] + [part(gs) for part in k_parts], axis=0)
            vg = jnp.concatenate([vc_ref[0, :, gs]] + [part(gs) for part in v_parts], axis=0)
            qa = q_ref[0, q_rows, (2 * g) * LANES:(2 * g + 1) * LANES]
            qb = q_ref[0, q_rows, (2 * g + 1) * LANES:(2 * g + 2) * LANES]
            zero = jnp.zeros_like(qa)
            q4 = jnp.concatenate([jnp.where(lo, qa, zero), jnp.where(lo, zero, qa),
                                  jnp.where(lo, qb, zero), jnp.where(lo, zero, qb)], axis=0)
            s = lax.dot_general(q4, kg, (((1,), (1,)), ((), ())), preferred_element_type=F32)
            ps, dens = [], []
            for hh in range(GQA_GROUP):
                sk = sink_ref[g * GQA_GROUP + hh]
                sh = s[hh * tq:(hh + 1) * tq] + bias
                mx = jnp.maximum(jnp.max(sh, axis=-1, keepdims=True), sk)
                pr = jnp.exp(sh - mx)
                dens.append(jnp.sum(pr, axis=-1, keepdims=True) + jnp.exp(sk - mx))
                ps.append(pr.astype(BF16))
            pv = jnp.dot(jnp.concatenate(ps, axis=0), vg, preferred_element_type=F32)
            oh = [pv[hh * tq:(hh + 1) * tq] / dens[hh] for hh in range(GQA_GROUP)]
            out_cols.append(jnp.where(lo, oh[0], oh[1]))
            out_cols.append(jnp.where(lo, oh[2], oh[3]))
        return jnp.concatenate(out_cols, axis=1).astype(BF16)

    def rows_of(ref, r0):
        return lambda gs: ref[0, r0:r0 + tq, gs]

    t2 = 2 * pl.program_id(1)
    o = jnp.concatenate([
        attend(t2, slice(0, tq), [rows_of(kp_ref, 0), rows_of(km_ref, 0), rows_of(km_ref, tq)],
               [rows_of(vp_ref, 0), rows_of(vm_ref, 0), rows_of(vm_ref, tq)]),
        attend(t2 + 1, slice(tq, 2 * tq), [rows_of(km_ref, 0), rows_of(km_ref, tq), rows_of(kn_ref, 0)],
               [rows_of(vm_ref, 0), rows_of(vm_ref, tq), rows_of(vn_ref, 0)])], axis=0)
    y = jnp.dot(o, wo_ref[...], preferred_element_type=F32)
    m = mod_ref[0, 0]
    x_new = x_ref[0] + m[2:3] * y
    o_ref[0] = x_new
    _route_tile(x_new, m, gffn_ref, wr_ref, rb_ref, carry, first,
                h_ref, e0_ref, e1_ref, p0_ref, p1_ref, gt_ref, cnt_ref)


def _attn_core(x, mods, qkv, p, route_args, ctx_len):
    b, n, d = x.shape
    q, k, v = qkv
    nk = k.shape[-1]
    assert ctx_len == TR and TR == 2 * TQ
    ntq = n // TQ

    prev_spec = pl.BlockSpec((1, TQ, nk), lambda bb, t, s: (bb, jnp.maximum(2 * t - 1, 0), 0))
    next_spec = pl.BlockSpec((1, TQ, nk), lambda bb, t, s: (bb, jnp.minimum(2 * t + 2, ntq - 1), 0))
    ctx_spec = pl.BlockSpec((1, ctx_len, nk), lambda bb, t, s: (bb, 0, 0))
    r_in, r_out, r_shape, r_scratch = _route_io(b, n, d, TR, lambda t: t)
    grid_spec = pltpu.PrefetchScalarGridSpec(
        num_scalar_prefetch=1, grid=(b, n // TR),
        in_specs=[_row_spec(TR, d), _seg_mod_spec(d), _row_spec(TR, d),
                  ctx_spec, ctx_spec, prev_spec, _row_spec(TR, nk), next_spec, prev_spec, _row_spec(TR, nk), next_spec,
                  _full_spec((d, d))] + r_in,
        out_specs=[_row_spec(TR, d)] + r_out,
        scratch_shapes=r_scratch)
    return pl.pallas_call(
        functools.partial(_attn_core_kernel, ctx_len=ctx_len, n_rows=n),
        grid_spec=grid_spec,
        out_shape=[jax.ShapeDtypeStruct((b, n, d), F32)] + r_shape,
        compiler_params=_params(("arbitrary", "arbitrary")),
        name="attn_core",
    )(p["attn_sink"].astype(F32), x, mods, q, k, v, k, k, k, v, v, v, p["attn_w_o"].astype(BF16), *route_args)


def _lru_gates_scan(back, t, u_ref, cw_ref, cb_ref, wg_ref, ba_ref, bx_ref, lam_ref,
                    ubuf, halo, hcar, abuf, bbuf, hbuf):
    tr, d = u_ref.shape[1], u_ref.shape[2]
    nb = d // LRU_BLOCKS

    @pl.when(t <= 1)
    def _():
        halo[...] = jnp.zeros_like(halo)

    @pl.when(t == 0)
    def _():
        hcar[...] = jnp.zeros_like(hcar)

    u = u_ref[0]
    if not back:
        ubuf[0:SUBLANES, :] = halo[...]
        ubuf[SUBLANES:SUBLANES + tr, :] = u
        halo[...] = u[tr - SUBLANES:, :]
        taps = [ubuf[SUBLANES - (LRU_CONV_W - 1) + k:SUBLANES - (LRU_CONV_W - 1) + k + tr, :]
                for k in range(LRU_CONV_W)]
    else:
        ubuf[0:tr, :] = u
        ubuf[tr:tr + SUBLANES, :] = halo[...]
        halo[...] = u[:SUBLANES, :]
        taps = [ubuf[(LRU_CONV_W - 1) - k:(LRU_CONV_W - 1) - k + tr, :] for k in range(LRU_CONV_W)]
    cc = cb_ref[...] + taps[0] * cw_ref[0:1, :]
    for k in range(1, LRU_CONV_W):
        cc = cc + taps[k] * cw_ref[k:k + 1, :]

    ccb = cc.astype(BF16)
    lam = lam_ref[...]
    neg_c_softplus = -LRU_C * (jnp.maximum(-lam, 0.0) + jnp.log(1.0 + jnp.exp(-jnp.abs(lam))))
    for blk in range(LRU_BLOCKS):
        cs = slice(blk * nb, (blk + 1) * nb)
        z = jnp.dot(ccb[:, cs], wg_ref[blk], preferred_element_type=F32)
        r = jax.nn.sigmoid(z[:, :nb] + ba_ref[:, cs])
        gi = jax.nn.sigmoid(z[:, nb:] + bx_ref[:, cs])
        log_a = neg_c_softplus[:, cs] * r
        a = jnp.exp(log_a)
        abuf[:, cs] = a
        bbuf[:, cs] = jnp.sqrt(1.0 - a * a) * (gi * cc[:, cs])

    row = lax.broadcasted_iota(I32, (SUBLANES, d), 0)
    nchunk = tr // SUBLANES

    def chunk(i, h):
        ci = (nchunk - 1 - i) if back else i
        r0 = pl.multiple_of(ci * SUBLANES, SUBLANES)
        a = abuf[pl.ds(r0, SUBLANES), :]
        bb = bbuf[pl.ds(r0, SUBLANES), :]
        for k in (1, 2, 4):
            if back:
                sh, ok = SUBLANES - k, row < SUBLANES - k
            else:
                sh, ok = k, row >= k
            a_s = pltpu.roll(a, sh, 0)
            b_s = pltpu.roll(bb, sh, 0)
            bb = jnp.where(ok, a * b_s + bb, bb)
            a = jnp.where(ok, a * a_s, a)
        hh = a * h + bb
        hbuf[pl.ds(r0, SUBLANES), :] = hh
        return hh[0:1, :] if back else hh[SUBLANES - 1:SUBLANES, :]

    hcar[0:1, :] = lax.fori_loop(0, nchunk, chunk, hcar[0:1, :])


def _lru_fwd_kernel(u_ref, cw_ref, cb_ref, wg_ref, ba_ref, bx_ref, lam_ref, hf_ref,
                    ubuf, halo, hcar, abuf, bbuf, hbuf):
    _lru_gates_scan(False, pl.program_id(1), u_ref, cw_ref, cb_ref, wg_ref, ba_ref, bx_ref, lam_ref,
                    ubuf, halo, hcar, abuf, bbuf, hbuf)
    hf_ref[0] = hbuf[...]


def _lru_bwd_kernel(u_ref, cw_ref, cb_ref, wg_ref, ba_ref, bx_ref, lam_ref, hf_ref, gg_ref, x_ref, mod_ref,
                    wo_ref, gffn_ref, wr_ref, rb_ref, o_ref, h_ref, e0_ref, e1_ref, p0_ref, p1_ref, gt_ref, cnt_ref,
                    ubuf, halo, hcar, abuf, bbuf, hbuf, carry):
    first = _first_step()
    _lru_gates_scan(True, pl.program_id(1), u_ref, cw_ref, cb_ref, wg_ref, ba_ref, bx_ref, lam_ref,
                    ubuf, halo, hcar, abuf, bbuf, hbuf)
    y = (hf_ref[0] + hbuf[...]) * gg_ref[0]
    out = jnp.dot(y.astype(BF16), wo_ref[...], preferred_element_type=F32)
    m = mod_ref[0, 0]
    x_new = x_ref[0] + m[2:3] * out
    o_ref[0] = x_new
    _route_tile(x_new, m, gffn_ref, wr_ref, rb_ref, carry, first,
                h_ref, e0_ref, e1_ref, p0_ref, p1_ref, gt_ref, cnt_ref)


def _lru_core(x, mods, ugg, p, route_args):
    b, n, d = x.shape
    u, gg = ugg
    nt = n // TR
    nb = d // LRU_BLOCKS
    wg = jnp.concatenate([p["lru_wa"], p["lru_wx"]], axis=-1).astype(BF16)
    scratch = [pltpu.VMEM((TR + SUBLANES, d), F32), pltpu.VMEM((SUBLANES, d), F32),
               pltpu.VMEM((SUBLANES, d), F32), pltpu.VMEM((TR, d), F32), pltpu.VMEM((TR, d), F32),
               pltpu.VMEM((TR, d), F32)]

    def gate_specs(order):
        return [pl.BlockSpec((1, TR, d), lambda bb, t: (bb, order(t), 0)),
                _full_spec((LRU_CONV_W, d)), _full_spec((1, d)), _full_spec((LRU_BLOCKS, nb, 2 * nb)),
                _full_spec((1, d)), _full_spec((1, d)), _full_spec((1, d))]

    def gate_args(dd):
        return (u, p["lru_conv_w"][dd], p["lru_conv_b"][dd].reshape(1, d), wg[dd], p["lru_ba"][dd].reshape(1, d),
                p["lru_bx"][dd].reshape(1, d), p["lru_lam"][dd].reshape(1, d))

    hf = pl.pallas_call(
        _lru_fwd_kernel, grid=(b, nt),
        in_specs=gate_specs(lambda t: t),
        out_specs=_row_spec(TR, d),
        out_shape=jax.ShapeDtypeStruct((b, n, d), F32),
        scratch_shapes=scratch,
        compiler_params=_params(("arbitrary", "arbitrary")),
        name="lru_fwd",
    )(*gate_args(0))

    def rev(t):
        return jnp.where(t == 0, 0, nt - t)

    def rev_spec():
        return pl.BlockSpec((1, TR, d), lambda bb, t: (bb, rev(t), 0))

    r_in, r_out, r_shape, r_scratch = _route_io(b, n, d, TR, rev)
    return pl.pallas_call(
        _lru_bwd_kernel, grid=(b, nt),
        in_specs=gate_specs(rev) + [rev_spec(), rev_spec(), rev_spec(), _seg_mod_spec(d), _full_spec((d, d))] + r_in,
        out_specs=[rev_spec()] + r_out,
        out_shape=[jax.ShapeDtypeStruct((b, n, d), F32)] + r_shape,
        scratch_shapes=scratch + r_scratch,
        compiler_params=_params(("arbitrary", "arbitrary")),
        name="lru_bwd",
    )(*gate_args(1), hf, gg, x, mods, p["lru_w_out"].astype(BF16), *route_args)


def _dispatch_kernel(slot_ref, zrow_ref, nu_ref, h_ref, xs_ref, ring, zbuf, sem, zsem, *, n_tok, n_blocks):
    s = pl.program_id(0)
    ns = pl.num_programs(0)
    tr = h_ref.shape[0] // SUBLANES
    blk = zbuf.shape[0] // SUBLANES
    par = s % 2

    @pl.when(s == 0)
    def _():
        zbuf[...] = jnp.zeros_like(zbuf)
        for e in range(N_EXPERTS):
            pltpu.make_async_copy(zbuf, _tile_rows(xs_ref, zrow_ref[e], blk), zsem).start()
        for e in range(N_EXPERTS):
            pltpu.make_async_copy(zbuf, _tile_rows(xs_ref, 0, blk), zsem).wait()

        def zero_blk(i, carry):
            pltpu.make_async_copy(zbuf, _tile_rows(xs_ref, i * blk, blk), zsem).start()
            return carry

        def zero_blk_wait(i, carry):
            pltpu.make_async_copy(zbuf, _tile_rows(xs_ref, 0, blk), zsem).wait()
            return carry

        lax.fori_loop(nu_ref[0], n_blocks, zero_blk, 0)
        lax.fori_loop(nu_ref[0], n_blocks, zero_blk_wait, 0)

    def wait_ring(p):
        for _ in range(TOP_K):
            pltpu.make_async_copy(ring.at[p], _tile_rows(xs_ref, 0, tr), sem.at[p]).wait()

    @pl.when(s >= 2)
    def _():
        wait_ring(par)

    ring[par] = h_ref[...]
    base = s * tr

    def issue(i, carry):
        for k in range(TOP_K):
            pltpu.make_async_copy(_tile_rows(ring.at[par], i, 1),
                                  _tile_rows(xs_ref, slot_ref[k * n_tok + base + i], 1),
                                  sem.at[par]).start(priority=k)
        return carry

    lax.fori_loop(0, tr, issue, 0, unroll=8)

    @pl.when(s == ns - 1)
    def _():
        wait_ring(par)

        @pl.when(ns >= 2)
        def _():
            wait_ring(1 - par)


def _expert_kernel(be_ref, nu_ref, xs_ref, w1_ref, w3_ref, w2_ref, ys_ref, w1b, w3b, w2b):
    i = pl.program_id(0)
    used = i < nu_ref[0]
    new_expert = jnp.logical_or(i == 0, be_ref[i] != be_ref[jnp.maximum(i - 1, 0)])

    @pl.when(jnp.logical_and(used, new_expert))
    def _():
        w1b[...] = w1_ref[0, 0].astype(BF16)
        w3b[...] = w3_ref[0, 0].astype(BF16)
        w2b[...] = w2_ref[0, 0].astype(BF16)

    @pl.when(used)
    def _():
        xb = _load_token_tiles(xs_ref, xs_ref.shape[0] // SUBLANES).astype(BF16)
        a = jnp.dot(xb, w1b[...], preferred_element_type=F32)
        b = jnp.dot(xb, w3b[...], preferred_element_type=F32)
        hid = (a * jax.nn.sigmoid(a)) * b
        _store_token_tiles(ys_ref, jnp.dot(hid.astype(BF16), w2b[...], preferred_element_type=F32))

    @pl.when(jnp.logical_not(used))
    def _():
        ys_ref[...] = jnp.zeros_like(ys_ref)


def _combine_kernel(slot_ref, x_ref, mod_ref, gt_ref, ys_ref, *refs, n_tok, rows_per_batch, tile_off, tail, n_tail_in):
    if tail is None:
        nf_ref, o_ref, ybuf, sem = refs
    else:
        modn_ref, gn_ref = refs[0], refs[1]
        tail_in = refs[2:2 + n_tail_in]
        o_ref = refs[2 + n_tail_in]
        tail_out = refs[3 + n_tail_in:-2]
        ybuf, sem = refs[-2], refs[-1]
    bb, t = pl.program_id(0), pl.program_id(1)
    nt = pl.num_programs(1)
    tr = x_ref.shape[1]
    step = bb * nt + t
    nsteps = pl.num_programs(0) * nt

    def tok_base(s):
        return (s // nt) * rows_per_batch + (s % nt + tile_off) * tr

    def copy_row(base, i, ring, k):
        pltpu.make_async_copy(_tile_rows(ys_ref, slot_ref[k * n_tok + base + i], 1),
                              _tile_rows(ybuf.at[ring, k], i, 1), sem.at[ring]).start(priority=k)

    def wait_ring(ring):
        for k in range(TOP_K):
            pltpu.make_async_copy(_tile_rows(ys_ref, 0, tr), ybuf.at[ring, k], sem.at[ring]).wait()

    @pl.when(step == 0)
    def _():
        for first in range(COMBINE_RING - 1):
            base = tok_base(jnp.minimum(first, nsteps - 1))

            def one(i, carry, base=base, first=first):
                for k in range(TOP_K):
                    copy_row(base, i, first, k)
                return carry

            lax.fori_loop(0, tr, one, 0, unroll=8)

    ring = step % COMBINE_RING
    wait_ring(ring)
    gt = gt_ref[0]
    y = (gt[:, 0:1] * _load_token_tiles(ybuf.at[ring, 0], tr)
         + gt[:, 1:2] * _load_token_tiles(ybuf.at[ring, 1], tr))

    ahead = COMBINE_RING - 1
    fetch_base = tok_base(jnp.minimum(step + ahead, nsteps - 1))
    fetch_ring = (step + ahead) % COMBINE_RING
    for i in range(tr):
        for k in range(TOP_K):
            copy_row(fetch_base, i, fetch_ring, k)

    out = x_ref[0] + mod_ref[0, 0][5:6] * y
    if tail is None:
        o_ref[0] = out * lax.rsqrt(jnp.mean(out * out, axis=-1, keepdims=True) + EPS) * nf_ref[...]
    else:
        o_ref[0] = out
        mn = modn_ref[0, 0]
        tail(_norm_mod(out, gn_ref[...], mn[0:1], mn[1:2]), *tail_in, *tail_out)

    @pl.when(step == nsteps - 1)
    def _():
        for later in range(1, COMBINE_RING):
            wait_ring((step + later) % COMBINE_RING)


def _moe(x, mods, routed, layer, w1, w3, w2, ctx_len, norm_f=None, next_ip=None, next_mods=None, next_g=None):
    b, n, d = x.shape
    nt = n // TR
    n_tok = b * n
    de = w1.shape[-1]
    h, e0, e1, p0, p1, gtab, counts = routed

    counts = counts[:, 0]
    padded = (counts + MOE_BLK - 1) // MOE_BLK * MOE_BLK
    pend = jnp.cumsum(padded)
    pstart = (pend - padded).astype(I32)
    n_slots = n_tok * TOP_K + N_EXPERTS * MOE_BLK
    n_blocks = n_slots // MOE_BLK
    n_used = (pend[-1] // MOE_BLK).astype(I32).reshape(1)
    blk_start = jnp.arange(n_blocks, dtype=I32) * MOE_BLK
    blk_e = jnp.minimum(jnp.sum((pend[None, :] <= blk_start[:, None]).astype(I32), axis=1), N_EXPERTS - 1)

    def slot_of(e, p):
        e, p = e.reshape(n_tok), p.reshape(n_tok)
        start = jnp.zeros_like(e)
        for j in range(N_EXPERTS):
            start = jnp.where(e == j, pstart[j], start)
        return start + p

    slots = jnp.concatenate([slot_of(e0, p0), slot_of(e1, p1)])
    zrow = (pstart + counts).astype(I32)

    xs = pl.pallas_call(
        functools.partial(_dispatch_kernel, n_tok=n_tok, n_blocks=n_blocks),
        grid_spec=pltpu.PrefetchScalarGridSpec(
            num_scalar_prefetch=3, grid=(n_tok // TR,),
            in_specs=[pl.BlockSpec((TR * SUBLANES, LANES), lambda s, *_: (s, 0))],
            out_specs=pl.BlockSpec(memory_space=pl.ANY),
            scratch_shapes=[pltpu.VMEM((2, TR * SUBLANES, LANES), F32), pltpu.VMEM((MOE_BLK * SUBLANES, LANES), F32),
                            pltpu.SemaphoreType.DMA((2,)), pltpu.SemaphoreType.DMA]),
        out_shape=jax.ShapeDtypeStruct((n_slots * SUBLANES, LANES), F32),
        compiler_params=_params(("arbitrary",)),
        name="moe_dispatch",
    )(slots, zrow, n_used, h)

    def wspec(shape):
        return pl.BlockSpec((1, 1) + shape, lambda i, be, nu: (layer, be[i], 0, 0))

    def blk_spec():
        return pl.BlockSpec((MOE_BLK * SUBLANES, LANES), lambda i, be, nu: (i, 0))

    ys = pl.pallas_call(
        _expert_kernel,
        grid_spec=pltpu.PrefetchScalarGridSpec(
            num_scalar_prefetch=2, grid=(n_blocks,),
            in_specs=[blk_spec(), wspec((d, de)), wspec((d, de)), wspec((de, d))],
            out_specs=blk_spec(),
            scratch_shapes=[pltpu.VMEM((d, de), BF16), pltpu.VMEM((d, de), BF16), pltpu.VMEM((de, d), BF16)]),
        out_shape=jax.ShapeDtypeStruct((n_slots * SUBLANES, LANES), F32),
        compiler_params=_params(("arbitrary",)),
        name="moe_experts",
    )(blk_e, n_used, xs, w1, w3, w2)

    final = next_ip is None
    tile_off = ctx_len // TR if final else 0
    nt_out = nt - tile_off
    common_specs = [_row_spec(TR, d, tile_off), _seg_mod_spec(d, tile_off=tile_off), _row_spec(TR, LANES, tile_off),
                    pl.BlockSpec(memory_space=pl.ANY)]
    scratch = [pltpu.VMEM((COMBINE_RING, TOP_K, TR * SUBLANES, LANES), F32),
               pltpu.SemaphoreType.DMA((COMBINE_RING,))]
    stream_spec = _row_spec(TR, d)
    stream_shape = jax.ShapeDtypeStruct((b, nt_out * TR, d), F32)
    if final:
        return pl.pallas_call(
            functools.partial(_combine_kernel, n_tok=n_tok, rows_per_batch=n, tile_off=tile_off, tail=None,
                              n_tail_in=0),
            grid_spec=pltpu.PrefetchScalarGridSpec(
                num_scalar_prefetch=1, grid=(b, nt_out),
                in_specs=common_specs + [_full_spec((1, d))],
                out_specs=stream_spec, scratch_shapes=scratch),
            out_shape=stream_shape,
            compiler_params=_params(("arbitrary", "arbitrary")),
            name="moe_combine_final",
        )(slots, x, mods, gtab, ys, norm_f.reshape(1, d))
    outs = pl.pallas_call(
        functools.partial(_combine_kernel, n_tok=n_tok, rows_per_batch=n, tile_off=tile_off, tail=next_ip["tail"],
                          n_tail_in=len(next_ip["args"])),
        grid_spec=pltpu.PrefetchScalarGridSpec(
            num_scalar_prefetch=1, grid=(b, nt_out),
            in_specs=common_specs + [_seg_mod_spec(d), _full_spec((1, d))] + next_ip["in_specs"],
            out_specs=[stream_spec] + next_ip["out_specs"], scratch_shapes=scratch),
        out_shape=[stream_shape] + next_ip["out_shape"],
        compiler_params=_params(("arbitrary", "arbitrary")),
        name="moe_combine_in_proj",
    )(slots, x, mods, gtab, ys, next_mods, next_g.reshape(1, d), *next_ip["args"])
    return outs[0], outs[1:]


def kernel(x, c, ctx, c_ctx, w_mod, b_mod, norm_mix, norm_ffn, norm_f, conv_pw1, conv_pw1_b, conv_dw, conv_dw_b, conv_ln_g, conv_ln_b, conv_pw2, conv_pw2_b, attn_w_qkv, attn_w_o, attn_sink, lru_w_in, lru_conv_w, lru_conv_b, lru_wa, lru_ba, lru_wx, lru_bx, lru_lam, lru_w_out, moe_w_router, moe_router_bias, moe_w1, moe_w3, moe_w2):
    b, s, d = x.shape
    ctx_len = ctx.shape[1]
    depth = w_mod.shape[0]
    n = ctx_len + s
    assert ctx_len == TR and s % TR == 0 and d == SUBLANES * LANES
    assert moe_w_router.shape[1] == N_EXPERTS

    mods = _modulation(c, c_ctx, w_mod, b_mod)
    xs = jnp.concatenate([ctx, x], axis=1)

    wr = jnp.zeros((EXPERTS_PER_GROUP, SUBLANES, d), F32).at[:, :N_GROUPS].set(
        moe_w_router.T.reshape(N_GROUPS, EXPERTS_PER_GROUP, d).swapaxes(0, 1))
    wr = wr.reshape(ROUTER_ROWS, d).astype(BF16)
    rb = jnp.zeros((EXPERTS_PER_GROUP, SUBLANES), F32).at[:, :N_GROUPS].set(
        moe_router_bias.astype(F32).reshape(N_GROUPS, EXPERTS_PER_GROUP).T).reshape(-1, 1)

    def layer_params(i):
        slot = i // N_MIXERS
        return [dict(conv_pw1=conv_pw1, conv_pw1_b=conv_pw1_b, conv_dw=conv_dw, conv_dw_b=conv_dw_b,
                     conv_ln_g=conv_ln_g, conv_ln_b=conv_ln_b, conv_pw2=conv_pw2, conv_pw2_b=conv_pw2_b),
                dict(attn_w_qkv=attn_w_qkv, attn_w_o=attn_w_o, attn_sink=attn_sink),
                dict(lru_w_in=lru_w_in, lru_conv_w=lru_conv_w, lru_conv_b=lru_conv_b, lru_wa=lru_wa, lru_ba=lru_ba,
                     lru_wx=lru_wx, lru_bx=lru_bx, lru_lam=lru_lam, lru_w_out=lru_w_out)][i % N_MIXERS], slot

    def params_of(i):
        group, slot = layer_params(i)
        return {k: v[slot] for k, v in group.items()}

    p = params_of(0)
    ip = _in_proj(0, p, b, n, d, ctx_len)
    a_out = _in_proj_standalone(ip, xs, mods[0], norm_mix[0])
    if not isinstance(a_out, (list, tuple)):
        a_out = [a_out]
    for i in range(depth):
        kind = i % N_MIXERS
        route_args = (norm_ffn[i].reshape(1, d), wr, rb)
        if kind == 0:
            res = _conv_core(xs, mods[i], a_out[0], p, route_args)
        elif kind == 1:
            res = _attn_core(xs, mods[i], a_out, p, route_args, ctx_len)
        else:
            res = _lru_core(xs, mods[i], a_out, p, route_args)
        x_new, routed = res[0], res[1:]
        if i == depth - 1:
            return _moe(x_new, mods[i], routed, i, moe_w1, moe_w3, moe_w2, ctx_len, norm_f=norm_f)
        p = params_of(i + 1)
        ip = _in_proj((i + 1) % N_MIXERS, p, b, n, d, ctx_len)
        xs, a_out = _moe(x_new, mods[i], routed, i, moe_w1, moe_w3, moe_w2, ctx_len,
                         next_ip=ip, next_mods=mods[i + 1], next_g=norm_mix[i + 1])
```

```python
import functools

import jax
import jax.numpy as jnp
from jax import lax
from jax.experimental import pallas as pl
from jax.experimental.pallas import tpu as pltpu

F32 = jnp.float32
BF16 = jnp.bfloat16
I32 = jnp.int32

EPS = 1e-6
N_MOD = 6
N_MIXERS = 3
GRID_W = 64
CONV_WIDTH = 31
CONV_HALF = (CONV_WIDTH - 1) // 2
HEAD_DIM = 64
N_KV_HEADS = 4
GQA_GROUP = 4
WINDOW = 128
ROPE_BASE = 10000.0
ROPE_PAIRS = HEAD_DIM // 4
LRU_BLOCKS = 8
LRU_CONV_W = 4
LRU_C = 8.0
N_EXPERTS = 16
N_GROUPS = 4
EXPERTS_PER_GROUP = 4
TOP_K = 2

LANES = 128
SUBLANES = 8
TR = 256
TQ = 128
HALO = 16
MOE_BLK = 512
COMBINE_RING = 3
EXPERT_RING = 3
ROUTER_ROWS = SUBLANES * EXPERTS_PER_GROUP
NEG = -1e30
VMEM_LIMIT = 56 * 1024 * 1024


def _params(sem, vmem=VMEM_LIMIT):
    return pltpu.CompilerParams(dimension_semantics=sem, vmem_limit_bytes=vmem)


def _norm_mod(x, g, shift, scale):
    y = x * lax.rsqrt(jnp.mean(x * x, axis=-1, keepdims=True) + EPS)
    return (y * g) * (1.0 + scale) + shift


def _seg_mod_spec(d, tiles_per_seg=1, tile_off=0):
    return pl.BlockSpec((1, 1, N_MOD, d),
                        lambda b, t, *_: (b, jnp.minimum((t + tile_off) // tiles_per_seg, 1), 0, 0))


def _row_spec(tr, d, tile_off=0):
    return pl.BlockSpec((1, tr, d), lambda b, t, *_: (b, t + tile_off, 0))


def _full_spec(shape):
    nd = len(shape)
    return pl.BlockSpec(shape, lambda b, t, *_: (0,) * nd)


def _mod_kernel(c_ref, w_ref, b_ref, o_ref):
    c = c_ref[...]
    sc = c * jax.nn.sigmoid(c)
    o_ref[0] = jnp.dot(sc.astype(BF16), w_ref[0].astype(BF16), preferred_element_type=F32) + b_ref[0]


def _modulation(c, c_ctx, w_mod, b_mod):
    depth, d, nout = w_mod.shape
    b = c.shape[0]
    assert b + 1 <= SUBLANES
    rows = jnp.zeros((SUBLANES, d), F32).at[:b].set(c).at[b].set(c_ctx)
    tn = 1536
    raw = pl.pallas_call(
        _mod_kernel,
        grid=(depth, nout // tn),
        in_specs=[pl.BlockSpec((SUBLANES, d), lambda i, j: (0, 0)),
                  pl.BlockSpec((1, d, tn), lambda i, j: (i, 0, j)),
                  pl.BlockSpec((1, 1, tn), lambda i, j: (i, 0, j))],
        out_specs=pl.BlockSpec((1, SUBLANES, tn), lambda i, j: (i, 0, j)),
        out_shape=jax.ShapeDtypeStruct((depth, SUBLANES, nout), F32),
        compiler_params=_params(("arbitrary", "arbitrary")),
        name="modulation",
    )(rows, w_mod, b_mod.reshape(depth, 1, nout))
    raw = raw.reshape(depth, SUBLANES, N_MOD, d)
    lat = raw[:, :b]
    ctx = jnp.broadcast_to(raw[:, b][:, None], lat.shape)
    return jnp.stack([ctx, lat], axis=2)


def _store_token_tiles(ref, mat):
    rows = mat.shape[0]
    for s in range(SUBLANES):
        ref[pl.ds(s, rows, stride=SUBLANES), :] = mat[:, s * LANES:(s + 1) * LANES]


def _load_token_tiles(ref, rows):
    return jnp.concatenate([ref[pl.ds(s, rows, stride=SUBLANES), :] for s in range(SUBLANES)], axis=1)


def _tile_rows(ref, first_token, n_tokens):
    first_row = first_token * SUBLANES
    if not isinstance(first_row, int):
        first_row = pl.multiple_of(first_row, SUBLANES)
    return ref.at[pl.ds(first_row, n_tokens * SUBLANES)]


def _route_tile(x, m, g_ref, wr_ref, rb_ref, carry, first, h_ref, e0_ref, e1_ref, p0_ref, p1_ref, gt_ref, cnt_ref):
    tr = x.shape[0]

    @pl.when(first)
    def _():
        carry[...] = jnp.zeros_like(carry)

    h = _norm_mod(x, g_ref[...], m[3:4], m[4:5])
    _store_token_tiles(h_ref, h)
    logit = lax.dot_general(wr_ref[...], h.astype(BF16), (((1,), (1,)), ((), ())),
                            preferred_element_type=F32)
    s = jax.nn.sigmoid(logit)
    sel = s + rb_ref[...]
    sj = [s[SUBLANES * j:SUBLANES * (j + 1)] for j in range(EXPERTS_PER_GROUP)]
    cj = [sel[SUBLANES * j:SUBLANES * (j + 1)] for j in range(EXPERTS_PER_GROUP)]

    hi1, lo1 = jnp.maximum(cj[0], cj[1]), jnp.minimum(cj[0], cj[1])
    hi2, lo2 = jnp.maximum(cj[2], cj[3]), jnp.minimum(cj[2], cj[3])
    top1 = jnp.maximum(hi1, hi2)
    top2 = jnp.maximum(jnp.minimum(hi1, hi2), jnp.maximum(lo1, lo2))
    row = lax.broadcasted_iota(I32, (SUBLANES, tr), 0)
    gscore = jnp.where(row < N_GROUPS, top1 + top2, -jnp.inf)
    gmax = jnp.max(gscore, axis=0, keepdims=True)
    grp = jnp.min(jnp.where(gscore == gmax, row, SUBLANES), axis=0, keepdims=True)
    pick = row == grp
    c = [jnp.sum(jnp.where(pick, v, 0.0), axis=0, keepdims=True) for v in cj]
    w = [jnp.sum(jnp.where(pick, v, 0.0), axis=0, keepdims=True) for v in sj]

    def argtop(vals):
        best, bi, bw = vals[0], jnp.zeros_like(grp), w[0]
        for j in range(1, EXPERTS_PER_GROUP):
            better = vals[j] > best
            best = jnp.where(better, vals[j], best)
            bi = jnp.where(better, j, bi)
            bw = jnp.where(better, w[j], bw)
        return bi, bw

    i0, w0 = argtop(c)
    i1, w1 = argtop([jnp.where(i0 == j, -jnp.inf, c[j]) for j in range(EXPERTS_PER_GROUP)])
    e0 = grp * EXPERTS_PER_GROUP + i0
    e1 = grp * EXPERTS_PER_GROUP + i1
    wsum = w0 + w1
    g0 = w0 / wsum
    g1 = w1 / wsum

    erow = lax.broadcasted_iota(I32, (N_EXPERTS, tr), 0)
    member = jnp.logical_or(erow == e0, erow == e1)
    upper = (lax.broadcasted_iota(I32, (tr, tr), 0) < lax.broadcasted_iota(I32, (tr, tr), 1))
    before = jnp.dot(member.astype(BF16), upper.astype(BF16), preferred_element_type=F32) + carry[...]
    p0 = jnp.sum(jnp.where(erow == e0, before, 0.0), axis=0, keepdims=True)
    p1 = jnp.sum(jnp.where(erow == e1, before, 0.0), axis=0, keepdims=True)
    total = carry[...] + jnp.sum(member.astype(F32), axis=1, keepdims=True)
    carry[...] = total
    cnt_ref[...] = total.astype(I32)

    e0_ref[0] = e0
    e1_ref[0] = e1
    p0_ref[0] = p0.astype(I32)
    p1_ref[0] = p1.astype(I32)
    grow = lax.broadcasted_iota(I32, (LANES, tr), 0)
    gmat = jnp.where(grow == 0, g0, jnp.where(grow == 1, g1, 0.0))
    gt_ref[0] = gmat.T


def _route_io(b, n, d, tile, tile_of):
    nt = n // tile

    def flat(bb, t):
        return bb * nt + tile_of(t)

    lane_major = pl.BlockSpec((1, 1, tile), lambda bb, t, *_: (flat(bb, t), 0, 0))
    lane_major_shape = jax.ShapeDtypeStruct((b * nt, 1, tile), I32)
    in_specs = [_full_spec((1, d)), _full_spec((ROUTER_ROWS, d)), _full_spec((ROUTER_ROWS, 1))]
    out_specs = [pl.BlockSpec((tile * SUBLANES, LANES), lambda bb, t, *_: (flat(bb, t), 0)),
                 lane_major, lane_major, lane_major, lane_major,
                 pl.BlockSpec((1, tile, LANES), lambda bb, t, *_: (bb, tile_of(t), 0)),
                 pl.BlockSpec((N_EXPERTS, 1), lambda bb, t, *_: (0, 0))]
    out_shape = [jax.ShapeDtypeStruct((b * n * SUBLANES, LANES), F32), lane_major_shape, lane_major_shape,
                 lane_major_shape, lane_major_shape,
                 jax.ShapeDtypeStruct((b, n, LANES), F32), jax.ShapeDtypeStruct((N_EXPERTS, 1), I32)]
    scratch = [pltpu.VMEM((N_EXPERTS, 1), F32)]
    return in_specs, out_specs, out_shape, scratch


def _first_step():
    return jnp.logical_and(pl.program_id(0) == 0, pl.program_id(1) == 0)


def _glu_tail(h, w_ref, b_ref, o_ref):
    d = h.shape[-1]
    u = jnp.dot(h.astype(BF16), w_ref[...], preferred_element_type=F32) + b_ref[...]
    o_ref[0] = u[:, :d] * jax.nn.sigmoid(u[:, d:])


def _qkv_tail(h, w_ref, cos_ref, sin_ref, q_ref, k_ref, v_ref):
    d = h.shape[-1]
    nk = k_ref.shape[-1]
    t = jnp.dot(h.astype(BF16), w_ref[...], preferred_element_type=F32)
    cos = cos_ref[...]
    sin = sin_ref[...]
    lane = lax.broadcasted_iota(I32, cos.shape, 1)
    first_half = (lane % (2 * ROPE_PAIRS)) < ROPE_PAIRS

    def rope(xg):
        partner = jnp.where(first_half, pltpu.roll(xg, LANES - ROPE_PAIRS, 1), pltpu.roll(xg, ROPE_PAIRS, 1))
        return xg * cos + partner * sin

    for j in range(d // LANES):
        cs = slice(j * LANES, (j + 1) * LANES)
        q_ref[0, :, cs] = rope(t[:, cs]).astype(BF16)
    for j in range(nk // LANES):
        cs = slice(j * LANES, (j + 1) * LANES)
        k_ref[0, :, cs] = rope(t[:, d + j * LANES:d + (j + 1) * LANES]).astype(BF16)
    v_ref[0] = t[:, d + nk:].astype(BF16)


def _lru_in_tail(h, w_ref, u_ref, gg_ref):
    d = h.shape[-1]
    t = jnp.dot(h.astype(BF16), w_ref[...], preferred_element_type=F32)
    u_ref[0] = t[:, :d]
    gg_ref[0] = jax.nn.gelu(t[:, d:])


def _rope_tables(n, ctx_len):
    pos = jnp.arange(n - ctx_len, dtype=I32)
    inv = ROPE_BASE ** (-jnp.arange(ROPE_PAIRS, dtype=F32) / ROPE_PAIRS)
    ar = (pos // GRID_W).astype(F32)[:, None] * inv
    ac = (pos % GRID_W).astype(F32)[:, None] * inv
    ang = jnp.concatenate([ar, ar, ac, ac], axis=-1)
    sign = jnp.tile(jnp.concatenate([-jnp.ones((ROPE_PAIRS,), F32), jnp.ones((ROPE_PAIRS,), F32)]), 2)
    cos = jnp.concatenate([jnp.ones((ctx_len, HEAD_DIM), F32), jnp.cos(ang)], axis=0)
    sin = jnp.concatenate([jnp.zeros((ctx_len, HEAD_DIM), F32), jnp.sin(ang) * sign], axis=0)
    return jnp.tile(cos, (1, 2)), jnp.tile(sin, (1, 2))


def _in_proj(kind, p, b, n, d, ctx_len):
    if kind == 0:
        args = (p["conv_pw1"].astype(BF16), p["conv_pw1_b"].reshape(1, 2 * d))
        return dict(tail=_glu_tail, args=args, in_specs=[_full_spec((d, 2 * d)), _full_spec((1, 2 * d))],
                    out_specs=[_row_spec(TR, d)], out_shape=[jax.ShapeDtypeStruct((b, n, d), F32)])
    if kind == 1:
        kvd = N_KV_HEADS * HEAD_DIM
        nk = N_KV_HEADS * LANES
        assert d == 2 * nk
        scale = HEAD_DIM ** -0.5
        w_qkv = p["attn_w_qkv"]

        def dup(w):
            w = w.reshape(d, N_KV_HEADS, HEAD_DIM)
            return jnp.concatenate([w, w], axis=-1).reshape(d, nk)

        w_all = jnp.concatenate([w_qkv[:, :d] * scale, dup(w_qkv[:, d:d + kvd]), dup(w_qkv[:, d + kvd:])],
                                axis=1).astype(BF16)
        cos, sin = _rope_tables(n, ctx_len)
        table = pl.BlockSpec((TR, LANES), lambda bb, t, *_: (t, 0))
        return dict(tail=_qkv_tail, args=(w_all, cos, sin), in_specs=[_full_spec((d, d + 2 * nk)), table, table],
                    out_specs=[_row_spec(TR, d), _row_spec(TR, nk), _row_spec(TR, nk)],
                    out_shape=[jax.ShapeDtypeStruct((b, n, d), BF16), jax.ShapeDtypeStruct((b, n, nk), BF16),
                               jax.ShapeDtypeStruct((b, n, nk), BF16)])
    return dict(tail=_lru_in_tail, args=(p["lru_w_in"].astype(BF16),), in_specs=[_full_spec((d, 2 * d))],
                out_specs=[_row_spec(TR, d), _row_spec(TR, d)],
                out_shape=[jax.ShapeDtypeStruct((b, n, d), F32), jax.ShapeDtypeStruct((b, n, d), F32)])


def _in_proj_kernel(x_ref, mod_ref, g_ref, *refs, tail):
    m = mod_ref[0, 0]
    tail(_norm_mod(x_ref[0], g_ref[...], m[0:1], m[1:2]), *refs)


def _in_proj_standalone(ip, x, mods, g):
    b, n, d = x.shape
    return pl.pallas_call(
        functools.partial(_in_proj_kernel, tail=ip["tail"]), grid=(b, n // TR),
        in_specs=[_row_spec(TR, d), _seg_mod_spec(d), _full_spec((1, d))] + ip["in_specs"],
        out_specs=ip["out_specs"], out_shape=ip["out_shape"],
        compiler_params=_params(("parallel", "parallel")),
        name="in_proj",
    )(x, mods, g.reshape(1, d), *ip["args"])


def _conv_core_kernel(x_ref, mod_ref, gp_ref, gc_ref, gn_ref, dw_ref, dwb_ref, lng_ref, lnb_ref, w_ref, b_ref,
                      gffn_ref, wr_ref, rb_ref, o_ref, h_ref, e0_ref, e1_ref, p0_ref, p1_ref, gt_ref, cnt_ref,
                      gbuf, cbuf, carry):
    t = pl.program_id(1)
    nt = pl.num_programs(1)
    first = _first_step()
    tr, d = gc_ref.shape[1], gc_ref.shape[2]
    seg_first = t <= 1
    seg_last = jnp.logical_or(t == 0, t == nt - 1)
    gbuf[0, 0:HALO, :] = jnp.where(seg_first, 0.0, gp_ref[0])
    gbuf[0, HALO:HALO + tr, :] = gc_ref[0]
    gbuf[0, HALO + tr:HALO + tr + HALO, :] = jnp.where(seg_last, 0.0, gn_ref[0])
    span = tr + 2 * HALO - SUBLANES
    for r in range(1, SUBLANES):
        gbuf[r, 0:span, :] = gbuf[0, r:r + span, :]

    rc = 32
    base = HALO - CONV_HALF

    def chunk(i, carry_):
        r0 = pl.multiple_of(i * rc, rc)
        for c in range(d // LANES):
            cs = slice(c * LANES, (c + 1) * LANES)
            acc = jnp.zeros((rc, LANES), F32)
            for k in range(CONV_WIDTH):
                off = base + k
                rows = pl.ds(r0 + (off // SUBLANES) * SUBLANES, rc)
                acc = acc + dw_ref[k:k + 1, cs] * gbuf[off % SUBLANES, rows, cs]
            cbuf[pl.ds(r0, rc), cs] = acc
        return carry_

    lax.fori_loop(0, tr // rc, chunk, 0)

    u = cbuf[...] + dwb_ref[...]
    mu = jnp.mean(u, axis=-1, keepdims=True)
    uc = u - mu
    var = jnp.mean(uc * uc, axis=-1, keepdims=True)
    v = uc * lax.rsqrt(var + EPS) * lng_ref[...] + lnb_ref[...]
    v = v * jax.nn.sigmoid(v)
    y = jnp.dot(v.astype(BF16), w_ref[...], preferred_element_type=F32) + b_ref[...]
    m = mod_ref[0, 0]
    x_new = x_ref[0] + m[2:3] * y
    o_ref[0] = x_new
    _route_tile(x_new, m, gffn_ref, wr_ref, rb_ref, carry, first,
                h_ref, e0_ref, e1_ref, p0_ref, p1_ref, gt_ref, cnt_ref)


def _conv_core(x, mods, glu, p, route_args):
    b, n, d = x.shape
    nt = n // TR
    hb = TR // HALO
    nh = n // HALO
    dwp = jnp.zeros((32, d), F32).at[:CONV_WIDTH].set(p["conv_dw"])
    r_in, r_out, r_shape, r_scratch = _route_io(b, n, d, TR, lambda t: t)
    return pl.pallas_call(
        _conv_core_kernel, grid=(b, nt),
        in_specs=[_row_spec(TR, d), _seg_mod_spec(d),
                  pl.BlockSpec((1, HALO, d), lambda bb, t: (bb, jnp.maximum(t * hb - 1, 0), 0)),
                  _row_spec(TR, d),
                  pl.BlockSpec((1, HALO, d), lambda bb, t: (bb, jnp.minimum((t + 1) * hb, nh - 1), 0)),
                  _full_spec((32, d)), _full_spec((1, d)), _full_spec((1, d)), _full_spec((1, d)),
                  _full_spec((d, d)), _full_spec((1, d))] + r_in,
        out_specs=[_row_spec(TR, d)] + r_out,
        out_shape=[jax.ShapeDtypeStruct((b, n, d), F32)] + r_shape,
        scratch_shapes=[pltpu.VMEM((SUBLANES, TR + 2 * HALO, d), F32), pltpu.VMEM((TR, d), F32)] + r_scratch,
        compiler_params=_params(("arbitrary", "arbitrary")),
        name="conv_core",
    )(x, mods, glu, glu, glu, dwp, p["conv_dw_b"].reshape(1, d), p["conv_ln_g"].reshape(1, d),
      p["conv_ln_b"].reshape(1, d), p["conv_pw2"].astype(BF16), p["conv_pw2_b"].reshape(1, d), *route_args)


def _attn_core_kernel(sink_ref, x_ref, mod_ref, q_ref, kc_ref, vc_ref, kp_ref, km_ref, kn_ref,
                      vp_ref, vm_ref, vn_ref, wo_ref, gffn_ref, wr_ref, rb_ref,
                      o_ref, h_ref, e0_ref, e1_ref, p0_ref, p1_ref, gt_ref, cnt_ref, carry, *, ctx_len, n_rows):
    first = _first_step()
    tq = kp_ref.shape[1]
    nctx = kc_ref.shape[1]
    nkeys = nctx + 3 * tq
    lane = lax.broadcasted_iota(I32, (tq, LANES), 1)
    lo = lane < HEAD_DIM
    col = lax.broadcasted_iota(I32, (tq, nkeys), 1)
    row = lax.broadcasted_iota(I32, (tq, nkeys), 0)

    def attend(t, q_rows, k_parts, v_parts):
        rq = t * tq + row
        rk = (t - 1) * tq + (col - nctx)
        win_ok = (rk >= ctx_len) & (rk < n_rows) & (jnp.abs(rq - rk) <= WINDOW) & (t * tq >= ctx_len)
        bias = jnp.where((col < nctx) | win_ok, 0.0, NEG).astype(F32)
        out_cols = []
        for g in range(N_KV_HEADS):
            gs = slice(g * LANES, (g + 1) * LANES)
            kg = jnp.concatenate([kc_ref[0, :, gs]] + [part(gs) for part in k_parts], axis=0)
            vg = jnp.concatenate([vc_ref[0, :, gs]] + [part(gs) for part in v_parts], axis=0)
            qa = q_ref[0, q_rows, (2 * g) * LANES:(2 * g + 1) * LANES]
            qb = q_ref[0, q_rows, (2 * g + 1) * LANES:(2 * g + 2) * LANES]
            zero = jnp.zeros_like(qa)
            q4 = jnp.concatenate([jnp.where(lo, qa, zero), jnp.where(lo, zero, qa),
                                  jnp.where(lo, qb, zero), jnp.where(lo, zero, qb)], axis=0)
            s = lax.dot_general(q4, kg, (((1,), (1,)), ((), ())), preferred_element_type=F32)
            ps, dens = [], []
            for hh in range(GQA_GROUP):
                sk = sink_ref[g * GQA_GROUP + hh]
                sh = s[hh * tq:(hh + 1) * tq] + bias
                mx = jnp.maximum(jnp.max(sh, axis=-1, keepdims=True), sk)
                pr = jnp.exp(sh - mx)
                dens.append(jnp.sum(pr, axis=-1, keepdims=True) + jnp.exp(sk - mx))
                ps.append(pr.astype(BF16))
            pv = jnp.dot(jnp.concatenate(ps, axis=0), vg, preferred_element_type=F32)
            oh = [pv[hh * tq:(hh + 1) * tq] / dens[hh] for hh in range(GQA_GROUP)]
            out_cols.append(jnp.where(lo, oh[0], oh[1]))
            out_cols.append(jnp.where(lo, oh[2], oh[3]))
        return jnp.concatenate(out_cols, axis=1).astype(BF16)

    def rows_of(ref, r0):
        return lambda gs: ref[0, r0:r0 + tq, gs]

    t2 = 2 * pl.program_id(1)
    o = jnp.concatenate([
        attend(t2, slice(0, tq), [rows_of(kp_ref, 0), rows_of(km_ref, 0), rows_of(km_ref, tq)],
               [rows_of(vp_ref, 0), rows_of(vm_ref, 0), rows_of(vm_ref, tq)]),
        attend(t2 + 1, slice(tq, 2 * tq), [rows_of(km_ref, 0), rows_of(km_ref, tq), rows_of(kn_ref, 0)],
               [rows_of(vm_ref, 0), rows_of(vm_ref, tq), rows_of(vn_ref, 0)])], axis=0)
    y = jnp.dot(o, wo_ref[...], preferred_element_type=F32)
    m = mod_ref[0, 0]
    x_new = x_ref[0] + m[2:3] * y
    o_ref[0] = x_new
    _route_tile(x_new, m, gffn_ref, wr_ref, rb_ref, carry, first,
                h_ref, e0_ref, e1_ref, p0_ref, p1_ref, gt_ref, cnt_ref)


def _attn_core(x, mods, qkv, p, route_args, ctx_len):
    b, n, d = x.shape
    q, k, v = qkv
    nk = k.shape[-1]
    assert ctx_len == TR and TR == 2 * TQ
    ntq = n // TQ

    prev_spec = pl.BlockSpec((1, TQ, nk), lambda bb, t, s: (bb, jnp.maximum(2 * t - 1, 0), 0))
    next_spec = pl.BlockSpec((1, TQ, nk), lambda bb, t, s: (bb, jnp.minimum(2 * t + 2, ntq - 1), 0))
    ctx_spec = pl.BlockSpec((1, ctx_len, nk), lambda bb, t, s: (bb, 0, 0))
    r_in, r_out, r_shape, r_scratch = _route_io(b, n, d, TR, lambda t: t)
    grid_spec = pltpu.PrefetchScalarGridSpec(
        num_scalar_prefetch=1, grid=(b, n // TR),
        in_specs=[_row_spec(TR, d), _seg_mod_spec(d), _row_spec(TR, d),
                  ctx_spec, ctx_spec, prev_spec, _row_spec(TR, nk), next_spec, prev_spec, _row_spec(TR, nk), next_spec,
                  _full_spec((d, d))] + r_in,
        out_specs=[_row_spec(TR, d)] + r_out,
        scratch_shapes=r_scratch)
    return pl.pallas_call(
        functools.partial(_attn_core_kernel, ctx_len=ctx_len, n_rows=n),
        grid_spec=grid_spec,
        out_shape=[jax.ShapeDtypeStruct((b, n, d), F32)] + r_shape,
        compiler_params=_params(("arbitrary", "arbitrary")),
        name="attn_core",
    )(p["attn_sink"].astype(F32), x, mods, q, k, v, k, k, k, v, v, v, p["attn_w_o"].astype(BF16), *route_args)


def _lru_gates_scan(back, t, u_ref, cw_ref, cb_ref, wg_ref, ba_ref, bx_ref, lam_ref,
                    ubuf, halo, hcar, abuf, bbuf, hbuf):
    tr, d = u_ref.shape[1], u_ref.shape[2]
    nb = d // LRU_BLOCKS

    @pl.when(t <= 1)
    def _():
        halo[...] = jnp.zeros_like(halo)

    @pl.when(t == 0)
    def _():
        hcar[...] = jnp.zeros_like(hcar)

    u = u_ref[0]
    if not back:
        ubuf[0:SUBLANES, :] = halo[...]
        ubuf[SUBLANES:SUBLANES + tr, :] = u
        halo[...] = u[tr - SUBLANES:, :]
        taps = [ubuf[SUBLANES - (LRU_CONV_W - 1) + k:SUBLANES - (LRU_CONV_W - 1) + k + tr, :]
                for k in range(LRU_CONV_W)]
    else:
        ubuf[0:tr, :] = u
        ubuf[tr:tr + SUBLANES, :] = halo[...]
        halo[...] = u[:SUBLANES, :]
        taps = [ubuf[(LRU_CONV_W - 1) - k:(LRU_CONV_W - 1) - k + tr, :] for k in range(LRU_CONV_W)]
    cc = cb_ref[...] + taps[0] * cw_ref[0:1, :]
    for k in range(1, LRU_CONV_W):
        cc = cc + taps[k] * cw_ref[k:k + 1, :]

    ccb = cc.astype(BF16)
    lam = lam_ref[...]
    neg_c_softplus = -LRU_C * (jnp.maximum(-lam, 0.0) + jnp.log(1.0 + jnp.exp(-jnp.abs(lam))))
    for blk in range(LRU_BLOCKS):
        cs = slice(blk * nb, (blk + 1) * nb)
        z = jnp.dot(ccb[:, cs], wg_ref[blk], preferred_element_type=F32)
        r = jax.nn.sigmoid(z[:, :nb] + ba_ref[:, cs])
        gi = jax.nn.sigmoid(z[:, nb:] + bx_ref[:, cs])
        log_a = neg_c_softplus[:, cs] * r
        a = jnp.exp(log_a)
        abuf[:, cs] = a
        bbuf[:, cs] = jnp.sqrt(1.0 - a * a) * (gi * cc[:, cs])

    row = lax.broadcasted_iota(I32, (SUBLANES, d), 0)
    nchunk = tr // SUBLANES

    def chunk(i, h):
        ci = (nchunk - 1 - i) if back else i
        r0 = pl.multiple_of(ci * SUBLANES, SUBLANES)
        a = abuf[pl.ds(r0, SUBLANES), :]
        bb = bbuf[pl.ds(r0, SUBLANES), :]
        for k in (1, 2, 4):
            if back:
                sh, ok = SUBLANES - k, row < SUBLANES - k
            else:
                sh, ok = k, row >= k
            a_s = pltpu.roll(a, sh, 0)
            b_s = pltpu.roll(bb, sh, 0)
            bb = jnp.where(ok, a * b_s + bb, bb)
            a = jnp.where(ok, a * a_s, a)
        hh = a * h + bb
        hbuf[pl.ds(r0, SUBLANES), :] = hh
        return hh[0:1, :] if back else hh[SUBLANES - 1:SUBLANES, :]

    hcar[0:1, :] = lax.fori_loop(0, nchunk, chunk, hcar[0:1, :])


def _lru_fwd_kernel(u_ref, cw_ref, cb_ref, wg_ref, ba_ref, bx_ref, lam_ref, hf_ref,
                    ubuf, halo, hcar, abuf, bbuf, hbuf):
    _lru_gates_scan(False, pl.program_id(1), u_ref, cw_ref, cb_ref, wg_ref, ba_ref, bx_ref, lam_ref,
                    ubuf, halo, hcar, abuf, bbuf, hbuf)
    hf_ref[0] = hbuf[...]


def _lru_bwd_kernel(u_ref, cw_ref, cb_ref, wg_ref, ba_ref, bx_ref, lam_ref, hf_ref, gg_ref, x_ref, mod_ref,
                    wo_ref, gffn_ref, wr_ref, rb_ref, o_ref, h_ref, e0_ref, e1_ref, p0_ref, p1_ref, gt_ref, cnt_ref,
                    ubuf, halo, hcar, abuf, bbuf, hbuf, carry):
    first = _first_step()
    _lru_gates_scan(True, pl.program_id(1), u_ref, cw_ref, cb_ref, wg_ref, ba_ref, bx_ref, lam_ref,
                    ubuf, halo, hcar, abuf, bbuf, hbuf)
    y = (hf_ref[0] + hbuf[...]) * gg_ref[0]
    out = jnp.dot(y.astype(BF16), wo_ref[...], preferred_element_type=F32)
    m = mod_ref[0, 0]
    x_new = x_ref[0] + m[2:3] * out
    o_ref[0] = x_new
    _route_tile(x_new, m, gffn_ref, wr_ref, rb_ref, carry, first,
                h_ref, e0_ref, e1_ref, p0_ref, p1_ref, gt_ref, cnt_ref)


def _lru_core(x, mods, ugg, p, route_args):
    b, n, d = x.shape
    u, gg = ugg
    nt = n // TR
    nb = d // LRU_BLOCKS
    wg = jnp.concatenate([p["lru_wa"], p["lru_wx"]], axis=-1).astype(BF16)
    scratch = [pltpu.VMEM((TR + SUBLANES, d), F32), pltpu.VMEM((SUBLANES, d), F32),
               pltpu.VMEM((SUBLANES, d), F32), pltpu.VMEM((TR, d), F32), pltpu.VMEM((TR, d), F32),
               pltpu.VMEM((TR, d), F32)]

    def gate_specs(order):
        return [pl.BlockSpec((1, TR, d), lambda bb, t: (bb, order(t), 0)),
                _full_spec((LRU_CONV_W, d)), _full_spec((1, d)), _full_spec((LRU_BLOCKS, nb, 2 * nb)),
                _full_spec((1, d)), _full_spec((1, d)), _full_spec((1, d))]

    def gate_args(dd):
        return (u, p["lru_conv_w"][dd], p["lru_conv_b"][dd].reshape(1, d), wg[dd], p["lru_ba"][dd].reshape(1, d),
                p["lru_bx"][dd].reshape(1, d), p["lru_lam"][dd].reshape(1, d))

    hf = pl.pallas_call(
        _lru_fwd_kernel, grid=(b, nt),
        in_specs=gate_specs(lambda t: t),
        out_specs=_row_spec(TR, d),
        out_shape=jax.ShapeDtypeStruct((b, n, d), F32),
        scratch_shapes=scratch,
        compiler_params=_params(("arbitrary", "arbitrary")),
        name="lru_fwd",
    )(*gate_args(0))

    def rev(t):
        return jnp.where(t == 0, 0, nt - t)

    def rev_spec():
        return pl.BlockSpec((1, TR, d), lambda bb, t: (bb, rev(t), 0))

    r_in, r_out, r_shape, r_scratch = _route_io(b, n, d, TR, rev)
    return pl.pallas_call(
        _lru_bwd_kernel, grid=(b, nt),
        in_specs=gate_specs(rev) + [rev_spec(), rev_spec(), rev_spec(), _seg_mod_spec(d), _full_spec((d, d))] + r_in,
        out_specs=[rev_spec()] + r_out,
        out_shape=[jax.ShapeDtypeStruct((b, n, d), F32)] + r_shape,
        scratch_shapes=scratch + r_scratch,
        compiler_params=_params(("arbitrary", "arbitrary")),
        name="lru_bwd",
    )(*gate_args(1), hf, gg, x, mods, p["lru_w_out"].astype(BF16), *route_args)


def _slot_token_kernel(slot_ref, tok_ref, *, n_tok):
    n_slots = tok_ref.shape[0]

    def clear(i, carry):
        tok_ref[i] = 0
        return carry

    lax.fori_loop(0, n_slots, clear, 0, unroll=8)

    def assign(t, carry):
        for k in range(TOP_K):
            tok_ref[slot_ref[k * n_tok + t]] = t
        return carry

    lax.fori_loop(0, n_tok, assign, 0, unroll=8)


def _expert_kernel(be_ref, nu_ref, tok_ref, h_ref, w1_ref, w3_ref, w2_ref, ys_ref, xring, sem, w1b, w3b, w2b):
    i = pl.program_id(0)
    nb = pl.num_programs(0)
    blk = xring.shape[1] // SUBLANES
    used = i < nu_ref[0]
    new_expert = jnp.logical_or(i == 0, be_ref[i] != be_ref[jnp.maximum(i - 1, 0)])
    ahead = EXPERT_RING - 1

    def copy_row(block, r, ring, queue):
        pltpu.make_async_copy(_tile_rows(h_ref, tok_ref[block * blk + r], 1),
                              _tile_rows(xring.at[ring], r, 1), sem.at[ring]).start(priority=queue)

    def wait_ring(ring):
        pltpu.make_async_copy(_tile_rows(h_ref, 0, blk), xring.at[ring], sem.at[ring]).wait()

    @pl.when(i == 0)
    def _():
        for first in range(ahead):
            block = jnp.minimum(first, nb - 1)

            def one(j, carry, block=block, first=first):
                for u in range(2):
                    copy_row(block, 2 * j + u, first, u)
                return carry

            lax.fori_loop(0, blk // 2, one, 0, unroll=4)

    @pl.when(jnp.logical_and(used, new_expert))
    def _():
        w1b[...] = w1_ref[0, 0].astype(BF16)
        w3b[...] = w3_ref[0, 0].astype(BF16)
        w2b[...] = w2_ref[0, 0].astype(BF16)

    @pl.when(used)
    def _():
        ring = i % EXPERT_RING
        wait_ring(ring)
        xb = _load_token_tiles(xring.at[ring], blk).astype(BF16)
        fetch_block = jnp.minimum(i + ahead, nb - 1)
        fetch_ring = (i + ahead) % EXPERT_RING
        for r in range(blk):
            copy_row(fetch_block, r, fetch_ring, r % 2)
        a = jnp.dot(xb, w1b[...], preferred_element_type=F32)
        b = jnp.dot(xb, w3b[...], preferred_element_type=F32)
        hid = (a * jax.nn.sigmoid(a)) * b
        _store_token_tiles(ys_ref, jnp.dot(hid.astype(BF16), w2b[...], preferred_element_type=F32))

        @pl.when(i == nu_ref[0] - 1)
        def _():
            for later in range(1, EXPERT_RING):
                wait_ring((i + later) % EXPERT_RING)

    @pl.when(jnp.logical_not(used))
    def _():
        ys_ref[...] = jnp.zeros_like(ys_ref)


def _combine_kernel(slot_ref, x_ref, mod_ref, gt_ref, ys_ref, *refs, n_tok, rows_per_batch, tile_off, tail, n_tail_in):
    if tail is None:
        nf_ref, o_ref, ybuf, sem = refs
    else:
        modn_ref, gn_ref = refs[0], refs[1]
        tail_in = refs[2:2 + n_tail_in]
        o_ref = refs[2 + n_tail_in]
        tail_out = refs[3 + n_tail_in:-2]
        ybuf, sem = refs[-2], refs[-1]
    bb, t = pl.program_id(0), pl.program_id(1)
    nt = pl.num_programs(1)
    tr = x_ref.shape[1]
    step = bb * nt + t
    nsteps = pl.num_programs(0) * nt

    def tok_base(s):
        return (s // nt) * rows_per_batch + (s % nt + tile_off) * tr

    def copy_row(base, i, ring, k):
        pltpu.make_async_copy(_tile_rows(ys_ref, slot_ref[k * n_tok + base + i], 1),
                              _tile_rows(ybuf.at[ring, k], i, 1), sem.at[ring]).start(priority=k)

    def wait_ring(ring):
        for k in range(TOP_K):
            pltpu.make_async_copy(_tile_rows(ys_ref, 0, tr), ybuf.at[ring, k], sem.at[ring]).wait()

    @pl.when(step == 0)
    def _():
        for first in range(COMBINE_RING - 1):
            base = tok_base(jnp.minimum(first, nsteps - 1))

            def one(i, carry, base=base, first=first):
                for k in range(TOP_K):
                    copy_row(base, i, first, k)
                return carry

            lax.fori_loop(0, tr, one, 0, unroll=8)

    ring = step % COMBINE_RING
    wait_ring(ring)
    gt = gt_ref[0]
    y = (gt[:, 0:1] * _load_token_tiles(ybuf.at[ring, 0], tr)
         + gt[:, 1:2] * _load_token_tiles(ybuf.at[ring, 1], tr))

    ahead = COMBINE_RING - 1
    fetch_base = tok_base(jnp.minimum(step + ahead, nsteps - 1))
    fetch_ring = (step + ahead) % COMBINE_RING
    for i in range(tr):
        for k in range(TOP_K):
            copy_row(fetch_base, i, fetch_ring, k)

    out = x_ref[0] + mod_ref[0, 0][5:6] * y
    if tail is None:
        o_ref[0] = out * lax.rsqrt(jnp.mean(out * out, axis=-1, keepdims=True) + EPS) * nf_ref[...]
    else:
        o_ref[0] = out
        mn = modn_ref[0, 0]
        tail(_norm_mod(out, gn_ref[...], mn[0:1], mn[1:2]), *tail_in, *tail_out)

    @pl.when(step == nsteps - 1)
    def _():
        for later in range(1, COMBINE_RING):
            wait_ring((step + later) % COMBINE_RING)


def _moe(x, mods, routed, layer, w1, w3, w2, ctx_len, norm_f=None, next_ip=None, next_mods=None, next_g=None):
    b, n, d = x.shape
    nt = n // TR
    n_tok = b * n
    de = w1.shape[-1]
    h, e0, e1, p0, p1, gtab, counts = routed

    counts = counts[:, 0]
    padded = (counts + MOE_BLK - 1) // MOE_BLK * MOE_BLK
    pend = jnp.cumsum(padded)
    pstart = (pend - padded).astype(I32)
    n_slots = n_tok * TOP_K + N_EXPERTS * MOE_BLK
    n_blocks = n_slots // MOE_BLK
    n_used = (pend[-1] // MOE_BLK).astype(I32).reshape(1)
    blk_start = jnp.arange(n_blocks, dtype=I32) * MOE_BLK
    blk_e = jnp.minimum(jnp.sum((pend[None, :] <= blk_start[:, None]).astype(I32), axis=1), N_EXPERTS - 1)

    def slot_of(e, p):
        e, p = e.reshape(n_tok), p.reshape(n_tok)
        start = jnp.zeros_like(e)
        for j in range(N_EXPERTS):
            start = jnp.where(e == j, pstart[j], start)
        return start + p

    slots = jnp.concatenate([slot_of(e0, p0), slot_of(e1, p1)])

    slot_tok = pl.pallas_call(
        functools.partial(_slot_token_kernel, n_tok=n_tok),
        in_specs=[pl.BlockSpec(memory_space=pltpu.SMEM)],
        out_specs=pl.BlockSpec(memory_space=pltpu.SMEM),
        out_shape=jax.ShapeDtypeStruct((n_slots,), I32),
        name="moe_slot_tokens",
    )(slots)

    def wspec(shape):
        return pl.BlockSpec((1, 1) + shape, lambda i, be, nu, st: (layer, be[i], 0, 0))

    ys = pl.pallas_call(
        _expert_kernel,
        grid_spec=pltpu.PrefetchScalarGridSpec(
            num_scalar_prefetch=3, grid=(n_blocks,),
            in_specs=[pl.BlockSpec(memory_space=pl.ANY), wspec((d, de)), wspec((d, de)), wspec((de, d))],
            out_specs=pl.BlockSpec((MOE_BLK * SUBLANES, LANES), lambda i, be, nu, st: (i, 0)),
            scratch_shapes=[pltpu.VMEM((EXPERT_RING, MOE_BLK * SUBLANES, LANES), F32),
                            pltpu.SemaphoreType.DMA((EXPERT_RING,)),
                            pltpu.VMEM((d, de), BF16), pltpu.VMEM((d, de), BF16), pltpu.VMEM((de, d), BF16)]),
        out_shape=jax.ShapeDtypeStruct((n_slots * SUBLANES, LANES), F32),
        compiler_params=_params(("arbitrary",)),
        name="moe_experts",
    )(blk_e, n_used, slot_tok, h, w1, w3, w2)

    final = next_ip is None
    tile_off = ctx_len // TR if final else 0
    nt_out = nt - tile_off
    common_specs = [_row_spec(TR, d, tile_off), _seg_mod_spec(d, tile_off=tile_off), _row_spec(TR, LANES, tile_off),
                    pl.BlockSpec(memory_space=pl.ANY)]
    scratch = [pltpu.VMEM((COMBINE_RING, TOP_K, TR * SUBLANES, LANES), F32),
               pltpu.SemaphoreType.DMA((COMBINE_RING,))]
    stream_spec = _row_spec(TR, d)
    stream_shape = jax.ShapeDtypeStruct((b, nt_out * TR, d), F32)
    if final:
        return pl.pallas_call(
            functools.partial(_combine_kernel, n_tok=n_tok, rows_per_batch=n, tile_off=tile_off, tail=None,
                              n_tail_in=0),
            grid_spec=pltpu.PrefetchScalarGridSpec(
                num_scalar_prefetch=1, grid=(b, nt_out),
                in_specs=common_specs + [_full_spec((1, d))],
                out_specs=stream_spec, scratch_shapes=scratch),
            out_shape=stream_shape,
            compiler_params=_params(("arbitrary", "arbitrary")),
            name="moe_combine_final",
        )(slots, x, mods, gtab, ys, norm_f.reshape(1, d))
    outs = pl.pallas_call(
        functools.partial(_combine_kernel, n_tok=n_tok, rows_per_batch=n, tile_off=tile_off, tail=next_ip["tail"],
                          n_tail_in=len(next_ip["args"])),
        grid_spec=pltpu.PrefetchScalarGridSpec(
            num_scalar_prefetch=1, grid=(b, nt_out),
            in_specs=common_specs + [_seg_mod_spec(d), _full_spec((1, d))] + next_ip["in_specs"],
            out_specs=[stream_spec] + next_ip["out_specs"], scratch_shapes=scratch),
        out_shape=[stream_shape] + next_ip["out_shape"],
        compiler_params=_params(("arbitrary", "arbitrary")),
        name="moe_combine_in_proj",
    )(slots, x, mods, gtab, ys, next_mods, next_g.reshape(1, d), *next_ip["args"])
    return outs[0], outs[1:]


def kernel(x, c, ctx, c_ctx, w_mod, b_mod, norm_mix, norm_ffn, norm_f, conv_pw1, conv_pw1_b, conv_dw, conv_dw_b, conv_ln_g, conv_ln_b, conv_pw2, conv_pw2_b, attn_w_qkv, attn_w_o, attn_sink, lru_w_in, lru_conv_w, lru_conv_b, lru_wa, lru_ba, lru_wx, lru_bx, lru_lam, lru_w_out, moe_w_router, moe_router_bias, moe_w1, moe_w3, moe_w2):
    b, s, d = x.shape
    ctx_len = ctx.shape[1]
    depth = w_mod.shape[0]
    n = ctx_len + s
    assert ctx_len == TR and s % TR == 0 and d == SUBLANES * LANES
    assert moe_w_router.shape[1] == N_EXPERTS

    mods = _modulation(c, c_ctx, w_mod, b_mod)
    xs = jnp.concatenate([ctx, x], axis=1)

    wr = jnp.zeros((EXPERTS_PER_GROUP, SUBLANES, d), F32).at[:, :N_GROUPS].set(
        moe_w_router.T.reshape(N_GROUPS, EXPERTS_PER_GROUP, d).swapaxes(0, 1))
    wr = wr.reshape(ROUTER_ROWS, d).astype(BF16)
    rb = jnp.zeros((EXPERTS_PER_GROUP, SUBLANES), F32).at[:, :N_GROUPS].set(
        moe_router_bias.astype(F32).reshape(N_GROUPS, EXPERTS_PER_GROUP).T).reshape(-1, 1)

    def layer_params(i):
        slot = i // N_MIXERS
        return [dict(conv_pw1=conv_pw1, conv_pw1_b=conv_pw1_b, conv_dw=conv_dw, conv_dw_b=conv_dw_b,
                     conv_ln_g=conv_ln_g, conv_ln_b=conv_ln_b, conv_pw2=conv_pw2, conv_pw2_b=conv_pw2_b),
                dict(attn_w_qkv=attn_w_qkv, attn_w_o=attn_w_o, attn_sink=attn_sink),
                dict(lru_w_in=lru_w_in, lru_conv_w=lru_conv_w, lru_conv_b=lru_conv_b, lru_wa=lru_wa, lru_ba=lru_ba,
                     lru_wx=lru_wx, lru_bx=lru_bx, lru_lam=lru_lam, lru_w_out=lru_w_out)][i % N_MIXERS], slot

    def params_of(i):
        group, slot = layer_params(i)
        return {k: v[slot] for k, v in group.items()}

    p = params_of(0)
    ip = _in_proj(0, p, b, n, d, ctx_len)
    a_out = _in_proj_standalone(ip, xs, mods[0], norm_mix[0])
    if not isinstance(a_out, (list, tuple)):
        a_out = [a_out]
    for i in range(depth):
        kind = i % N_MIXERS
        route_args = (norm_ffn[i].reshape(1, d), wr, rb)
        if kind == 0:
            res = _conv_core(xs, mods[i], a_out[0], p, route_args)
        elif kind == 1:
            res = _attn_core(xs, mods[i], a_out, p, route_args, ctx_len)
        else:
            res = _lru_core(xs, mods[i], a_out, p, route_args)
        x_new, routed = res[0], res[1:]
        if i == depth - 1:
            return _moe(x_new, mods[i], routed, i, moe_w1, moe_w3, moe_w2, ctx_len, norm_f=norm_f)
        p = params_of(i + 1)
        ip = _in_proj((i + 1) % N_MIXERS, p, b, n, d, ctx_len)
        xs, a_out = _moe(x_new, mods[i], routed, i, moe_w1, moe_w3, moe_w2, ctx_len,
                         next_ip=ip, next_mods=mods[i + 1], next_g=norm_mix[i + 1])
```

```python
import functools

import jax
import jax.numpy as jnp
from jax import lax
from jax.experimental import pallas as pl
from jax.experimental.pallas import tpu as pltpu

F32 = jnp.float32
BF16 = jnp.bfloat16
I32 = jnp.int32

EPS = 1e-6
N_MOD = 6
N_MIXERS = 3
GRID_W = 64
CONV_WIDTH = 31
CONV_HALF = (CONV_WIDTH - 1) // 2
HEAD_DIM = 64
N_KV_HEADS = 4
GQA_GROUP = 4
WINDOW = 128
ROPE_BASE = 10000.0
ROPE_PAIRS = HEAD_DIM // 4
LRU_BLOCKS = 8
LRU_CONV_W = 4
LRU_C = 8.0
N_EXPERTS = 16
N_GROUPS = 4
EXPERTS_PER_GROUP = 4
TOP_K = 2

LANES = 128
SUBLANES = 8
TR = 256
TQ = 128
HALO = 16
MOE_BLK = 512
COMBINE_RING = 3
EXPERT_RING = 3
ROUTER_ROWS = SUBLANES * EXPERTS_PER_GROUP
NEG = -1e30
VMEM_LIMIT = 56 * 1024 * 1024


def _params(sem, vmem=VMEM_LIMIT):
    return pltpu.CompilerParams(dimension_semantics=sem, vmem_limit_bytes=vmem)


def _norm_mod(x, g, shift, scale):
    y = x * lax.rsqrt(jnp.mean(x * x, axis=-1, keepdims=True) + EPS)
    return (y * g) * (1.0 + scale) + shift


def _seg_mod_spec(d, tiles_per_seg=1, tile_off=0):
    return pl.BlockSpec((1, 1, N_MOD, d),
                        lambda b, t, *_: (b, jnp.minimum((t + tile_off) // tiles_per_seg, 1), 0, 0))


def _row_spec(tr, d, tile_off=0):
    return pl.BlockSpec((1, tr, d), lambda b, t, *_: (b, t + tile_off, 0))


def _full_spec(shape):
    nd = len(shape)
    return pl.BlockSpec(shape, lambda b, t, *_: (0,) * nd)


def _mod_kernel(c_ref, w_ref, b_ref, o_ref):
    c = c_ref[...]
    sc = c * jax.nn.sigmoid(c)
    o_ref[0] = jnp.dot(sc.astype(BF16), w_ref[0].astype(BF16), preferred_element_type=F32) + b_ref[0]


def _modulation(c, c_ctx, w_mod, b_mod):
    depth, d, nout = w_mod.shape
    b = c.shape[0]
    assert b + 1 <= SUBLANES
    rows = jnp.zeros((SUBLANES, d), F32).at[:b].set(c).at[b].set(c_ctx)
    tn = 1536
    raw = pl.pallas_call(
        _mod_kernel,
        grid=(depth, nout // tn),
        in_specs=[pl.BlockSpec((SUBLANES, d), lambda i, j: (0, 0)),
                  pl.BlockSpec((1, d, tn), lambda i, j: (i, 0, j)),
                  pl.BlockSpec((1, 1, tn), lambda i, j: (i, 0, j))],
        out_specs=pl.BlockSpec((1, SUBLANES, tn), lambda i, j: (i, 0, j)),
        out_shape=jax.ShapeDtypeStruct((depth, SUBLANES, nout), F32),
        compiler_params=_params(("arbitrary", "arbitrary")),
        name="modulation",
    )(rows, w_mod, b_mod.reshape(depth, 1, nout))
    raw = raw.reshape(depth, SUBLANES, N_MOD, d)
    lat = raw[:, :b]
    ctx = jnp.broadcast_to(raw[:, b][:, None], lat.shape)
    return jnp.stack([ctx, lat], axis=2)


def _store_token_tiles(ref, mat):
    rows = mat.shape[0]
    for s in range(SUBLANES):
        ref[pl.ds(s, rows, stride=SUBLANES), :] = mat[:, s * LANES:(s + 1) * LANES]


def _load_token_tiles(ref, rows):
    return jnp.concatenate([ref[pl.ds(s, rows, stride=SUBLANES), :] for s in range(SUBLANES)], axis=1)


def _tile_rows(ref, first_token, n_tokens):
    first_row = first_token * SUBLANES
    if not isinstance(first_row, int):
        first_row = pl.multiple_of(first_row, SUBLANES)
    return ref.at[pl.ds(first_row, n_tokens * SUBLANES)]


def _route_tile(x, m, g_ref, wr_ref, rb_ref, carry, first, h_ref, e0_ref, e1_ref, p0_ref, p1_ref, gt_ref, cnt_ref):
    tr = x.shape[0]

    @pl.when(first)
    def _():
        carry[...] = jnp.zeros_like(carry)

    h = _norm_mod(x, g_ref[...], m[3:4], m[4:5])
    _store_token_tiles(h_ref, h)
    logit = lax.dot_general(wr_ref[...], h.astype(BF16), (((1,), (1,)), ((), ())),
                            preferred_element_type=F32)
    s = jax.nn.sigmoid(logit)
    sel = s + rb_ref[...]
    sj = [s[SUBLANES * j:SUBLANES * (j + 1)] for j in range(EXPERTS_PER_GROUP)]
    cj = [sel[SUBLANES * j:SUBLANES * (j + 1)] for j in range(EXPERTS_PER_GROUP)]

    hi1, lo1 = jnp.maximum(cj[0], cj[1]), jnp.minimum(cj[0], cj[1])
    hi2, lo2 = jnp.maximum(cj[2], cj[3]), jnp.minimum(cj[2], cj[3])
    top1 = jnp.maximum(hi1, hi2)
    top2 = jnp.maximum(jnp.minimum(hi1, hi2), jnp.maximum(lo1, lo2))
    row = lax.broadcasted_iota(I32, (SUBLANES, tr), 0)
    gscore = jnp.where(row < N_GROUPS, top1 + top2, -jnp.inf)
    gmax = jnp.max(gscore, axis=0, keepdims=True)
    grp = jnp.min(jnp.where(gscore == gmax, row, SUBLANES), axis=0, keepdims=True)
    pick = row == grp
    c = [jnp.sum(jnp.where(pick, v, 0.0), axis=0, keepdims=True) for v in cj]
    w = [jnp.sum(jnp.where(pick, v, 0.0), axis=0, keepdims=True) for v in sj]

    def argtop(vals):
        best, bi, bw = vals[0], jnp.zeros_like(grp), w[0]
        for j in range(1, EXPERTS_PER_GROUP):
            better = vals[j] > best
            best = jnp.where(better, vals[j], best)
            bi = jnp.where(better, j, bi)
            bw = jnp.where(better, w[j], bw)
        return bi, bw

    i0, w0 = argtop(c)
    i1, w1 = argtop([jnp.where(i0 == j, -jnp.inf, c[j]) for j in range(EXPERTS_PER_GROUP)])
    e0 = grp * EXPERTS_PER_GROUP + i0
    e1 = grp * EXPERTS_PER_GROUP + i1
    wsum = w0 + w1
    g0 = w0 / wsum
    g1 = w1 / wsum

    erow = lax.broadcasted_iota(I32, (N_EXPERTS, tr), 0)
    member = jnp.logical_or(erow == e0, erow == e1)
    upper = (lax.broadcasted_iota(I32, (tr, tr), 0) < lax.broadcasted_iota(I32, (tr, tr), 1))
    before = jnp.dot(member.astype(BF16), upper.astype(BF16), preferred_element_type=F32) + carry[...]
    p0 = jnp.sum(jnp.where(erow == e0, before, 0.0), axis=0, keepdims=True)
    p1 = jnp.sum(jnp.where(erow == e1, before, 0.0), axis=0, keepdims=True)
    total = carry[...] + jnp.sum(member.astype(F32), axis=1, keepdims=True)
    carry[...] = total
    cnt_ref[...] = total.astype(I32)

    e0_ref[0] = e0
    e1_ref[0] = e1
    p0_ref[0] = p0.astype(I32)
    p1_ref[0] = p1.astype(I32)
    grow = lax.broadcasted_iota(I32, (LANES, tr), 0)
    gmat = jnp.where(grow == 0, g0, jnp.where(grow == 1, g1, 0.0))
    gt_ref[0] = gmat.T


def _route_io(b, n, d, tile, tile_of):
    nt = n // tile

    def flat(bb, t):
        return bb * nt + tile_of(t)

    lane_major = pl.BlockSpec((1, 1, tile), lambda bb, t, *_: (flat(bb, t), 0, 0))
    lane_major_shape = jax.ShapeDtypeStruct((b * nt, 1, tile), I32)
    in_specs = [_full_spec((1, d)), _full_spec((ROUTER_ROWS, d)), _full_spec((ROUTER_ROWS, 1))]
    out_specs = [pl.BlockSpec((tile * SUBLANES, LANES), lambda bb, t, *_: (flat(bb, t), 0)),
                 lane_major, lane_major, lane_major, lane_major,
                 pl.BlockSpec((1, tile, LANES), lambda bb, t, *_: (bb, tile_of(t), 0)),
                 pl.BlockSpec((N_EXPERTS, 1), lambda bb, t, *_: (0, 0))]
    out_shape = [jax.ShapeDtypeStruct((b * n * SUBLANES, LANES), F32), lane_major_shape, lane_major_shape,
                 lane_major_shape, lane_major_shape,
                 jax.ShapeDtypeStruct((b, n, LANES), F32), jax.ShapeDtypeStruct((N_EXPERTS, 1), I32)]
    scratch = [pltpu.VMEM((N_EXPERTS, 1), F32)]
    return in_specs, out_specs, out_shape, scratch


def _first_step():
    return jnp.logical_and(pl.program_id(0) == 0, pl.program_id(1) == 0)


def _glu_tail(h, w_ref, b_ref, o_ref):
    d = h.shape[-1]
    u = jnp.dot(h.astype(BF16), w_ref[...], preferred_element_type=F32) + b_ref[...]
    o_ref[0] = u[:, :d] * jax.nn.sigmoid(u[:, d:])


def _qkv_tail(h, w_ref, cos_ref, sin_ref, q_ref, k_ref, v_ref):
    d = h.shape[-1]
    nk = k_ref.shape[-1]
    t = jnp.dot(h.astype(BF16), w_ref[...], preferred_element_type=F32)
    cos = cos_ref[...]
    sin = sin_ref[...]
    lane = lax.broadcasted_iota(I32, cos.shape, 1)
    first_half = (lane % (2 * ROPE_PAIRS)) < ROPE_PAIRS

    def rope(xg):
        partner = jnp.where(first_half, pltpu.roll(xg, LANES - ROPE_PAIRS, 1), pltpu.roll(xg, ROPE_PAIRS, 1))
        return xg * cos + partner * sin

    for j in range(d // LANES):
        cs = slice(j * LANES, (j + 1) * LANES)
        q_ref[0, :, cs] = rope(t[:, cs]).astype(BF16)
    for j in range(nk // LANES):
        cs = slice(j * LANES, (j + 1) * LANES)
        k_ref[0, :, cs] = rope(t[:, d + j * LANES:d + (j + 1) * LANES]).astype(BF16)
    v_ref[0] = t[:, d + nk:].astype(BF16)


def _lru_in_tail(h, w_ref, u_ref, gg_ref):
    d = h.shape[-1]
    t = jnp.dot(h.astype(BF16), w_ref[...], preferred_element_type=F32)
    u_ref[0] = t[:, :d]
    gg_ref[0] = jax.nn.gelu(t[:, d:])


def _rope_tables(n, ctx_len):
    pos = jnp.arange(n - ctx_len, dtype=I32)
    inv = ROPE_BASE ** (-jnp.arange(ROPE_PAIRS, dtype=F32) / ROPE_PAIRS)
    ar = (pos // GRID_W).astype(F32)[:, None] * inv
    ac = (pos % GRID_W).astype(F32)[:, None] * inv
    ang = jnp.concatenate([ar, ar, ac, ac], axis=-1)
    sign = jnp.tile(jnp.concatenate([-jnp.ones((ROPE_PAIRS,), F32), jnp.ones((ROPE_PAIRS,), F32)]), 2)
    cos = jnp.concatenate([jnp.ones((ctx_len, HEAD_DIM), F32), jnp.cos(ang)], axis=0)
    sin = jnp.concatenate([jnp.zeros((ctx_len, HEAD_DIM), F32), jnp.sin(ang) * sign], axis=0)
    return jnp.tile(cos, (1, 2)), jnp.tile(sin, (1, 2))


def _in_proj(kind, p, b, n, d, ctx_len):
    if kind == 0:
        args = (p["conv_pw1"].astype(BF16), p["conv_pw1_b"].reshape(1, 2 * d))
        return dict(tail=_glu_tail, args=args, in_specs=[_full_spec((d, 2 * d)), _full_spec((1, 2 * d))],
                    out_specs=[_row_spec(TR, d)], out_shape=[jax.ShapeDtypeStruct((b, n, d), F32)])
    if kind == 1:
        kvd = N_KV_HEADS * HEAD_DIM
        nk = N_KV_HEADS * LANES
        assert d == 2 * nk
        scale = HEAD_DIM ** -0.5
        w_qkv = p["attn_w_qkv"]

        def dup(w):
            w = w.reshape(d, N_KV_HEADS, HEAD_DIM)
            return jnp.concatenate([w, w], axis=-1).reshape(d, nk)

        w_all = jnp.concatenate([w_qkv[:, :d] * scale, dup(w_qkv[:, d:d + kvd]), dup(w_qkv[:, d + kvd:])],
                                axis=1).astype(BF16)
        cos, sin = _rope_tables(n, ctx_len)
        table = pl.BlockSpec((TR, LANES), lambda bb, t, *_: (t, 0))
        return dict(tail=_qkv_tail, args=(w_all, cos, sin), in_specs=[_full_spec((d, d + 2 * nk)), table, table],
                    out_specs=[_row_spec(TR, d), _row_spec(TR, nk), _row_spec(TR, nk)],
                    out_shape=[jax.ShapeDtypeStruct((b, n, d), BF16), jax.ShapeDtypeStruct((b, n, nk), BF16),
                               jax.ShapeDtypeStruct((b, n, nk), BF16)])
    return dict(tail=_lru_in_tail, args=(p["lru_w_in"].astype(BF16),), in_specs=[_full_spec((d, 2 * d))],
                out_specs=[_row_spec(TR, d), _row_spec(TR, d)],
                out_shape=[jax.ShapeDtypeStruct((b, n, d), F32), jax.ShapeDtypeStruct((b, n, d), F32)])


def _in_proj_kernel(x_ref, mod_ref, g_ref, *refs, tail):
    m = mod_ref[0, 0]
    tail(_norm_mod(x_ref[0], g_ref[...], m[0:1], m[1:2]), *refs)


def _in_proj_standalone(ip, x, mods, g):
    b, n, d = x.shape
    return pl.pallas_call(
        functools.partial(_in_proj_kernel, tail=ip["tail"]), grid=(b, n // TR),
        in_specs=[_row_spec(TR, d), _seg_mod_spec(d), _full_spec((1, d))] + ip["in_specs"],
        out_specs=ip["out_specs"], out_shape=ip["out_shape"],
        compiler_params=_params(("parallel", "parallel")),
        name="in_proj",
    )(x, mods, g.reshape(1, d), *ip["args"])


def _conv_core_kernel(x_ref, mod_ref, gp_ref, gc_ref, gn_ref, dw_ref, dwb_ref, lng_ref, lnb_ref, w_ref, b_ref,
                      gffn_ref, wr_ref, rb_ref, o_ref, h_ref, e0_ref, e1_ref, p0_ref, p1_ref, gt_ref, cnt_ref,
                      gbuf, cbuf, carry):
    t = pl.program_id(1)
    nt = pl.num_programs(1)
    first = _first_step()
    tr, d = gc_ref.shape[1], gc_ref.shape[2]
    seg_first = t <= 1
    seg_last = jnp.logical_or(t == 0, t == nt - 1)
    gbuf[0, 0:HALO, :] = jnp.where(seg_first, 0.0, gp_ref[0])
    gbuf[0, HALO:HALO + tr, :] = gc_ref[0]
    gbuf[0, HALO + tr:HALO + tr + HALO, :] = jnp.where(seg_last, 0.0, gn_ref[0])
    span = tr + 2 * HALO - SUBLANES
    for r in range(1, SUBLANES):
        gbuf[r, 0:span, :] = gbuf[0, r:r + span, :]

    rc = 32
    base = HALO - CONV_HALF

    def chunk(i, carry_):
        r0 = pl.multiple_of(i * rc, rc)
        for c in range(d // LANES):
            cs = slice(c * LANES, (c + 1) * LANES)
            acc = jnp.zeros((rc, LANES), F32)
            for k in range(CONV_WIDTH):
                off = base + k
                rows = pl.ds(r0 + (off // SUBLANES) * SUBLANES, rc)
                acc = acc + dw_ref[k:k + 1, cs] * gbuf[off % SUBLANES, rows, cs]
            cbuf[pl.ds(r0, rc), cs] = acc
        return carry_

    lax.fori_loop(0, tr // rc, chunk, 0)

    u = cbuf[...] + dwb_ref[...]
    mu = jnp.mean(u, axis=-1, keepdims=True)
    uc = u - mu
    var = jnp.mean(uc * uc, axis=-1, keepdims=True)
    v = uc * lax.rsqrt(var + EPS) * lng_ref[...] + lnb_ref[...]
    v = v * jax.nn.sigmoid(v)
    y = jnp.dot(v.astype(BF16), w_ref[...], preferred_element_type=F32) + b_ref[...]
    m = mod_ref[0, 0]
    x_new = x_ref[0] + m[2:3] * y
    o_ref[0] = x_new
    _route_tile(x_new, m, gffn_ref, wr_ref, rb_ref, carry, first,
                h_ref, e0_ref, e1_ref, p0_ref, p1_ref, gt_ref, cnt_ref)


def _conv_core(x, mods, glu, p, route_args):
    b, n, d = x.shape
    nt = n // TR
    hb = TR // HALO
    nh = n // HALO
    dwp = jnp.zeros((32, d), F32).at[:CONV_WIDTH].set(p["conv_dw"])
    r_in, r_out, r_shape, r_scratch = _route_io(b, n, d, TR, lambda t: t)
    return pl.pallas_call(
        _conv_core_kernel, grid=(b, nt),
        in_specs=[_row_spec(TR, d), _seg_mod_spec(d),
                  pl.BlockSpec((1, HALO, d), lambda bb, t: (bb, jnp.maximum(t * hb - 1, 0), 0)),
                  _row_spec(TR, d),
                  pl.BlockSpec((1, HALO, d), lambda bb, t: (bb, jnp.minimum((t + 1) * hb, nh - 1), 0)),
                  _full_spec((32, d)), _full_spec((1, d)), _full_spec((1, d)), _full_spec((1, d)),
                  _full_spec((d, d)), _full_spec((1, d))] + r_in,
        out_specs=[_row_spec(TR, d)] + r_out,
        out_shape=[jax.ShapeDtypeStruct((b, n, d), F32)] + r_shape,
        scratch_shapes=[pltpu.VMEM((SUBLANES, TR + 2 * HALO, d), F32), pltpu.VMEM((TR, d), F32)] + r_scratch,
        compiler_params=_params(("arbitrary", "arbitrary")),
        name="conv_core",
    )(x, mods, glu, glu, glu, dwp, p["conv_dw_b"].reshape(1, d), p["conv_ln_g"].reshape(1, d),
      p["conv_ln_b"].reshape(1, d), p["conv_pw2"].astype(BF16), p["conv_pw2_b"].reshape(1, d), *route_args)


def _attn_core_kernel(sink_ref, x_ref, mod_ref, q_ref, kc_ref, vc_ref, kp_ref, km_ref, kn_ref,
                      vp_ref, vm_ref, vn_ref, wo_ref, gffn_ref, wr_ref, rb_ref,
                      o_ref, h_ref, e0_ref, e1_ref, p0_ref, p1_ref, gt_ref, cnt_ref, carry, *, ctx_len, n_rows):
    first = _first_step()
    tq = kp_ref.shape[1]
    nctx = kc_ref.shape[1]
    nkeys = nctx + 3 * tq
    lane = lax.broadcasted_iota(I32, (tq, LANES), 1)
    lo = lane < HEAD_DIM
    col = lax.broadcasted_iota(I32, (tq, nkeys), 1)
    row = lax.broadcasted_iota(I32, (tq, nkeys), 0)

    def attend(t, q_rows, k_parts, v_parts):
        rq = t * tq + row
        rk = (t - 1) * tq + (col - nctx)
        win_ok = (rk >= ctx_len) & (rk < n_rows) & (jnp.abs(rq - rk) <= WINDOW) & (t * tq >= ctx_len)
        bias = jnp.where((col < nctx) | win_ok, 0.0, NEG).astype(F32)
        out_cols = []
        for g in range(N_KV_HEADS):
            gs = slice(g * LANES, (g + 1) * LANES)
            kg = jnp.concatenate([kc_ref[0, :, gs]] + [part(gs) for part in k_parts], axis=0)
            vg = jnp.concatenate([vc_ref[0, :, gs]] + [part(gs) for part in v_parts], axis=0)
            qa = q_ref[0, q_rows, (2 * g) * LANES:(2 * g + 1) * LANES]
            qb = q_ref[0, q_rows, (2 * g + 1) * LANES:(2 * g + 2) * LANES]
            zero = jnp.zeros_like(qa)
            q4 = jnp.concatenate([jnp.where(lo, qa, zero), jnp.where(lo, zero, qa),
                                  jnp.where(lo, qb, zero), jnp.where(lo, zero, qb)], axis=0)
            s = lax.dot_general(q4, kg, (((1,), (1,)), ((), ())), preferred_element_type=F32)
            ps, dens = [], []
            for hh in range(GQA_GROUP):
                sk = sink_ref[g * GQA_GROUP + hh]
                sh = s[hh * tq:(hh + 1) * tq] + bias
                mx = jnp.maximum(jnp.max(sh, axis=-1, keepdims=True), sk)
                pr = jnp.exp(sh - mx)
                dens.append(jnp.sum(pr, axis=-1, keepdims=True) + jnp.exp(sk - mx))
                ps.append(pr.astype(BF16))
            pv = jnp.dot(jnp.concatenate(ps, axis=0), vg, preferred_element_type=F32)
            oh = [pv[hh * tq:(hh + 1) * tq] / dens[hh] for hh in range(GQA_GROUP)]
            out_cols.append(jnp.where(lo, oh[0], oh[1]))
            out_cols.append(jnp.where(lo, oh[2], oh[3]))
        return jnp.concatenate(out_cols, axis=1).astype(BF16)

    def rows_of(ref, r0):
        return lambda gs: ref[0, r0:r0 + tq, gs]

    t2 = 2 * pl.program_id(1)
    o = jnp.concatenate([
        attend(t2, slice(0, tq), [rows_of(kp_ref, 0), rows_of(km_ref, 0), rows_of(km_ref, tq)],
               [rows_of(vp_ref, 0), rows_of(vm_ref, 0), rows_of(vm_ref, tq)]),
        attend(t2 + 1, slice(tq, 2 * tq), [rows_of(km_ref, 0), rows_of(km_ref, tq), rows_of(kn_ref, 0)],
               [rows_of(vm_ref, 0), rows_of(vm_ref, tq), rows_of(vn_ref, 0)])], axis=0)
    y = jnp.dot(o, wo_ref[...], preferred_element_type=F32)
    m = mod_ref[0, 0]
    x_new = x_ref[0] + m[2:3] * y
    o_ref[0] = x_new
    _route_tile(x_new, m, gffn_ref, wr_ref, rb_ref, carry, first,
                h_ref, e0_ref, e1_ref, p0_ref, p1_ref, gt_ref, cnt_ref)


def _attn_core(x, mods, qkv, p, route_args, ctx_len):
    b, n, d = x.shape
    q, k, v = qkv
    nk = k.shape[-1]
    assert ctx_len == TR and TR == 2 * TQ
    ntq = n // TQ

    prev_spec = pl.BlockSpec((1, TQ, nk), lambda bb, t, s: (bb, jnp.maximum(2 * t - 1, 0), 0))
    next_spec = pl.BlockSpec((1, TQ, nk), lambda bb, t, s: (bb, jnp.minimum(2 * t + 2, ntq - 1), 0))
    ctx_spec = pl.BlockSpec((1, ctx_len, nk), lambda bb, t, s: (bb, 0, 0))
    r_in, r_out, r_shape, r_scratch = _route_io(b, n, d, TR, lambda t: t)
    grid_spec = pltpu.PrefetchScalarGridSpec(
        num_scalar_prefetch=1, grid=(b, n // TR),
        in_specs=[_row_spec(TR, d), _seg_mod_spec(d), _row_spec(TR, d),
                  ctx_spec, ctx_spec, prev_spec, _row_spec(TR, nk), next_spec, prev_spec, _row_spec(TR, nk), next_spec,
                  _full_spec((d, d))] + r_in,
        out_specs=[_row_spec(TR, d)] + r_out,
        scratch_shapes=r_scratch)
    return pl.pallas_call(
        functools.partial(_attn_core_kernel, ctx_len=ctx_len, n_rows=n),
        grid_spec=grid_spec,
        out_shape=[jax.ShapeDtypeStruct((b, n, d), F32)] + r_shape,
        compiler_params=_params(("arbitrary", "arbitrary")),
        name="attn_core",
    )(p["attn_sink"].astype(F32), x, mods, q, k, v, k, k, k, v, v, v, p["attn_w_o"].astype(BF16), *route_args)


def _lru_gates_scan(back, t, u_ref, cw_ref, cb_ref, wg_ref, ba_ref, bx_ref, lam_ref,
                    ubuf, halo, hcar, abuf, bbuf, hbuf):
    tr, d = u_ref.shape[1], u_ref.shape[2]
    nb = d // LRU_BLOCKS

    @pl.when(t <= 1)
    def _():
        halo[...] = jnp.zeros_like(halo)

    @pl.when(t == 0)
    def _():
        hcar[...] = jnp.zeros_like(hcar)

    u = u_ref[0]
    if not back:
        ubuf[0:SUBLANES, :] = halo[...]
        ubuf[SUBLANES:SUBLANES + tr, :] = u
        halo[...] = u[tr - SUBLANES:, :]
        taps = [ubuf[SUBLANES - (LRU_CONV_W - 1) + k:SUBLANES - (LRU_CONV_W - 1) + k + tr, :]
                for k in range(LRU_CONV_W)]
    else:
        ubuf[0:tr, :] = u
        ubuf[tr:tr + SUBLANES, :] = halo[...]
        halo[...] = u[:SUBLANES, :]
        taps = [ubuf[(LRU_CONV_W - 1) - k:(LRU_CONV_W - 1) - k + tr, :] for k in range(LRU_CONV_W)]
    cc = cb_ref[...] + taps[0] * cw_ref[0:1, :]
    for k in range(1, LRU_CONV_W):
        cc = cc + taps[k] * cw_ref[k:k + 1, :]

    ccb = cc.astype(BF16)
    lam = lam_ref[...]
    neg_c_softplus = -LRU_C * (jnp.maximum(-lam, 0.0) + jnp.log(1.0 + jnp.exp(-jnp.abs(lam))))
    for blk in range(LRU_BLOCKS):
        cs = slice(blk * nb, (blk + 1) * nb)
        z = jnp.dot(ccb[:, cs], wg_ref[blk], preferred_element_type=F32)
        r = jax.nn.sigmoid(z[:, :nb] + ba_ref[:, cs])
        gi = jax.nn.sigmoid(z[:, nb:] + bx_ref[:, cs])
        log_a = neg_c_softplus[:, cs] * r
        a = jnp.exp(log_a)
        abuf[:, cs] = a
        bbuf[:, cs] = jnp.sqrt(1.0 - a * a) * (gi * cc[:, cs])

    row = lax.broadcasted_iota(I32, (SUBLANES, d), 0)
    nchunk = tr // SUBLANES

    def chunk(i, h):
        ci = (nchunk - 1 - i) if back else i
        r0 = pl.multiple_of(ci * SUBLANES, SUBLANES)
        a = abuf[pl.ds(r0, SUBLANES), :]
        bb = bbuf[pl.ds(r0, SUBLANES), :]
        for k in (1, 2, 4):
            if back:
                sh, ok = SUBLANES - k, row < SUBLANES - k
            else:
                sh, ok = k, row >= k
            a_s = pltpu.roll(a, sh, 0)
            b_s = pltpu.roll(bb, sh, 0)
            bb = jnp.where(ok, a * b_s + bb, bb)
            a = jnp.where(ok, a * a_s, a)
        hh = a * h + bb
        hbuf[pl.ds(r0, SUBLANES), :] = hh
        return hh[0:1, :] if back else hh[SUBLANES - 1:SUBLANES, :]

    hcar[0:1, :] = lax.fori_loop(0, nchunk, chunk, hcar[0:1, :])


def _lru_fwd_kernel(u_ref, cw_ref, cb_ref, wg_ref, ba_ref, bx_ref, lam_ref, hf_ref,
                    ubuf, halo, hcar, abuf, bbuf, hbuf):
    _lru_gates_scan(False, pl.program_id(1), u_ref, cw_ref, cb_ref, wg_ref, ba_ref, bx_ref, lam_ref,
                    ubuf, halo, hcar, abuf, bbuf, hbuf)
    hf_ref[0] = hbuf[...]


def _lru_bwd_kernel(u_ref, cw_ref, cb_ref, wg_ref, ba_ref, bx_ref, lam_ref, hf_ref, gg_ref, x_ref, mod_ref,
                    wo_ref, gffn_ref, wr_ref, rb_ref, o_ref, h_ref, e0_ref, e1_ref, p0_ref, p1_ref, gt_ref, cnt_ref,
                    ubuf, halo, hcar, abuf, bbuf, hbuf, carry):
    first = _first_step()
    _lru_gates_scan(True, pl.program_id(1), u_ref, cw_ref, cb_ref, wg_ref, ba_ref, bx_ref, lam_ref,
                    ubuf, halo, hcar, abuf, bbuf, hbuf)
    y = (hf_ref[0] + hbuf[...]) * gg_ref[0]
    out = jnp.dot(y.astype(BF16), wo_ref[...], preferred_element_type=F32)
    m = mod_ref[0, 0]
    x_new = x_ref[0] + m[2:3] * out
    o_ref[0] = x_new
    _route_tile(x_new, m, gffn_ref, wr_ref, rb_ref, carry, first,
                h_ref, e0_ref, e1_ref, p0_ref, p1_ref, gt_ref, cnt_ref)


def _lru_core(x, mods, ugg, p, route_args):
    b, n, d = x.shape
    u, gg = ugg
    nt = n // TR
    nb = d // LRU_BLOCKS
    wg = jnp.concatenate([p["lru_wa"], p["lru_wx"]], axis=-1).astype(BF16)
    scratch = [pltpu.VMEM((TR + SUBLANES, d), F32), pltpu.VMEM((SUBLANES, d), F32),
               pltpu.VMEM((SUBLANES, d), F32), pltpu.VMEM((TR, d), F32), pltpu.VMEM((TR, d), F32),
               pltpu.VMEM((TR, d), F32)]

    def gate_specs(order):
        return [pl.BlockSpec((1, TR, d), lambda bb, t: (bb, order(t), 0)),
                _full_spec((LRU_CONV_W, d)), _full_spec((1, d)), _full_spec((LRU_BLOCKS, nb, 2 * nb)),
                _full_spec((1, d)), _full_spec((1, d)), _full_spec((1, d))]

    def gate_args(dd):
        return (u, p["lru_conv_w"][dd], p["lru_conv_b"][dd].reshape(1, d), wg[dd], p["lru_ba"][dd].reshape(1, d),
                p["lru_bx"][dd].reshape(1, d), p["lru_lam"][dd].reshape(1, d))

    hf = pl.pallas_call(
        _lru_fwd_kernel, grid=(b, nt),
        in_specs=gate_specs(lambda t: t),
        out_specs=_row_spec(TR, d),
        out_shape=jax.ShapeDtypeStruct((b, n, d), F32),
        scratch_shapes=scratch,
        compiler_params=_params(("arbitrary", "arbitrary")),
        name="lru_fwd",
    )(*gate_args(0))

    def rev(t):
        return jnp.where(t == 0, 0, nt - t)

    def rev_spec():
        return pl.BlockSpec((1, TR, d), lambda bb, t: (bb, rev(t), 0))

    r_in, r_out, r_shape, r_scratch = _route_io(b, n, d, TR, rev)
    return pl.pallas_call(
        _lru_bwd_kernel, grid=(b, nt),
        in_specs=gate_specs(rev) + [rev_spec(), rev_spec(), rev_spec(), _seg_mod_spec(d), _full_spec((d, d))] + r_in,
        out_specs=[rev_spec()] + r_out,
        out_shape=[jax.ShapeDtypeStruct((b, n, d), F32)] + r_shape,
        scratch_shapes=scratch + r_scratch,
        compiler_params=_params(("arbitrary", "arbitrary")),
        name="lru_bwd",
    )(*gate_args(1), hf, gg, x, mods, p["lru_w_out"].astype(BF16), *route_args)


def _dispatch_kernel(slot_ref, zrow_ref, nu_ref, h_ref, xs_ref, ring, zbuf, sem, zsem, *, n_tok, n_blocks):
    s = pl.program_id(0)
    ns = pl.num_programs(0)
    tr = h_ref.shape[0] // SUBLANES
    blk = zbuf.shape[0] // SUBLANES
    par = s % 2

    @pl.when(s == 0)
    def _():
        zbuf[...] = jnp.zeros_like(zbuf)
        for e in range(N_EXPERTS):
            pltpu.make_async_copy(zbuf, _tile_rows(xs_ref, zrow_ref[e], blk), zsem).start()
        for e in range(N_EXPERTS):
            pltpu.make_async_copy(zbuf, _tile_rows(xs_ref, 0, blk), zsem).wait()

        def zero_blk(i, carry):
            pltpu.make_async_copy(zbuf, _tile_rows(xs_ref, i * blk, blk), zsem).start()
            return carry

        def zero_blk_wait(i, carry):
            pltpu.make_async_copy(zbuf, _tile_rows(xs_ref, 0, blk), zsem).wait()
            return carry

        lax.fori_loop(nu_ref[0], n_blocks, zero_blk, 0)
        lax.fori_loop(nu_ref[0], n_blocks, zero_blk_wait, 0)

    def wait_ring(p):
        for _ in range(TOP_K):
            pltpu.make_async_copy(ring.at[p], _tile_rows(xs_ref, 0, tr), sem.at[p]).wait()

    @pl.when(s >= 2)
    def _():
        wait_ring(par)

    ring[par] = h_ref[...]
    base = s * tr

    def issue(i, carry):
        for k in range(TOP_K):
            pltpu.make_async_copy(_tile_rows(ring.at[par], i, 1),
                                  _tile_rows(xs_ref, slot_ref[k * n_tok + base + i], 1),
                                  sem.at[par]).start(priority=k)
        return carry

    lax.fori_loop(0, tr, issue, 0, unroll=8)

    @pl.when(s == ns - 1)
    def _():
        wait_ring(par)

        @pl.when(ns >= 2)
        def _():
            wait_ring(1 - par)


def _expert_kernel(be_ref, nu_ref, xs_ref, w1_ref, w3_ref, w2_ref, ys_ref, xring, sem, w1b, w3b, w2b):
    i = pl.program_id(0)
    nb = pl.num_programs(0)
    blk = xring.shape[1] // SUBLANES
    used = i < nu_ref[0]
    new_expert = jnp.logical_or(i == 0, be_ref[i] != be_ref[jnp.maximum(i - 1, 0)])
    ahead = EXPERT_RING - 1

    def fetch(block, ring):
        return pltpu.make_async_copy(_tile_rows(xs_ref, block * blk, blk), xring.at[ring], sem.at[ring])

    @pl.when(i == 0)
    def _():
        for first in range(ahead):
            fetch(jnp.minimum(first, nb - 1), first).start()

    @pl.when(jnp.logical_and(used, new_expert))
    def _():
        w1b[...] = w1_ref[0, 0].astype(BF16)
        w3b[...] = w3_ref[0, 0].astype(BF16)
        w2b[...] = w2_ref[0, 0].astype(BF16)

    @pl.when(used)
    def _():
        ring = i % EXPERT_RING
        fetch(0, ring).wait()
        fetch(jnp.minimum(i + ahead, nb - 1), (i + ahead) % EXPERT_RING).start()
        xb = _load_token_tiles(xring.at[ring], blk).astype(BF16)
        a = jnp.dot(xb, w1b[...], preferred_element_type=F32)
        b = jnp.dot(xb, w3b[...], preferred_element_type=F32)
        hid = (a * jax.nn.sigmoid(a)) * b
        _store_token_tiles(ys_ref, jnp.dot(hid.astype(BF16), w2b[...], preferred_element_type=F32))

        @pl.when(i == nu_ref[0] - 1)
        def _():
            for later in range(1, EXPERT_RING):
                fetch(0, (i + later) % EXPERT_RING).wait()

    @pl.when(jnp.logical_not(used))
    def _():
        ys_ref[...] = jnp.zeros_like(ys_ref)


def _combine_kernel(slot_ref, x_ref, mod_ref, gt_ref, ys_ref, *refs, n_tok, rows_per_batch, tile_off, tail, n_tail_in):
    if tail is None:
        nf_ref, o_ref, ybuf, sem = refs
    else:
        modn_ref, gn_ref = refs[0], refs[1]
        tail_in = refs[2:2 + n_tail_in]
        o_ref = refs[2 + n_tail_in]
        tail_out = refs[3 + n_tail_in:-2]
        ybuf, sem = refs[-2], refs[-1]
    bb, t = pl.program_id(0), pl.program_id(1)
    nt = pl.num_programs(1)
    tr = x_ref.shape[1]
    step = bb * nt + t
    nsteps = pl.num_programs(0) * nt

    def tok_base(s):
        return (s // nt) * rows_per_batch + (s % nt + tile_off) * tr

    def copy_row(base, i, ring, k):
        pltpu.make_async_copy(_tile_rows(ys_ref, slot_ref[k * n_tok + base + i], 1),
                              _tile_rows(ybuf.at[ring, k], i, 1), sem.at[ring]).start(priority=k)

    def wait_ring(ring):
        for k in range(TOP_K):
            pltpu.make_async_copy(_tile_rows(ys_ref, 0, tr), ybuf.at[ring, k], sem.at[ring]).wait()

    @pl.when(step == 0)
    def _():
        for first in range(COMBINE_RING - 1):
            base = tok_base(jnp.minimum(first, nsteps - 1))

            def one(i, carry, base=base, first=first):
                for k in range(TOP_K):
                    copy_row(base, i, first, k)
                return carry

            lax.fori_loop(0, tr, one, 0, unroll=8)

    ring = step % COMBINE_RING
    wait_ring(ring)
    gt = gt_ref[0]
    y = (gt[:, 0:1] * _load_token_tiles(ybuf.at[ring, 0], tr)
         + gt[:, 1:2] * _load_token_tiles(ybuf.at[ring, 1], tr))

    ahead = COMBINE_RING - 1
    fetch_base = tok_base(jnp.minimum(step + ahead, nsteps - 1))
    fetch_ring = (step + ahead) % COMBINE_RING
    for i in range(tr):
        for k in range(TOP_K):
            copy_row(fetch_base, i, fetch_ring, k)

    out = x_ref[0] + mod_ref[0, 0][5:6] * y
    if tail is None:
        o_ref[0] = out * lax.rsqrt(jnp.mean(out * out, axis=-1, keepdims=True) + EPS) * nf_ref[...]
    else:
        o_ref[0] = out
        mn = modn_ref[0, 0]
        tail(_norm_mod(out, gn_ref[...], mn[0:1], mn[1:2]), *tail_in, *tail_out)

    @pl.when(step == nsteps - 1)
    def _():
        for later in range(1, COMBINE_RING):
            wait_ring((step + later) % COMBINE_RING)


def _moe(x, mods, routed, layer, w1, w3, w2, ctx_len, norm_f=None, next_ip=None, next_mods=None, next_g=None):
    b, n, d = x.shape
    nt = n // TR
    n_tok = b * n
    de = w1.shape[-1]
    h, e0, e1, p0, p1, gtab, counts = routed

    counts = counts[:, 0]
    padded = (counts + MOE_BLK - 1) // MOE_BLK * MOE_BLK
    pend = jnp.cumsum(padded)
    pstart = (pend - padded).astype(I32)
    n_slots = n_tok * TOP_K + N_EXPERTS * MOE_BLK
    n_blocks = n_slots // MOE_BLK
    n_used = (pend[-1] // MOE_BLK).astype(I32).reshape(1)
    blk_start = jnp.arange(n_blocks, dtype=I32) * MOE_BLK
    blk_e = jnp.minimum(jnp.sum((pend[None, :] <= blk_start[:, None]).astype(I32), axis=1), N_EXPERTS - 1)

    def slot_of(e, p):
        e, p = e.reshape(n_tok), p.reshape(n_tok)
        start = jnp.zeros_like(e)
        for j in range(N_EXPERTS):
            start = jnp.where(e == j, pstart[j], start)
        return start + p

    slots = jnp.concatenate([slot_of(e0, p0), slot_of(e1, p1)])
    zrow = (pstart + counts).astype(I32)

    xs = pl.pallas_call(
        functools.partial(_dispatch_kernel, n_tok=n_tok, n_blocks=n_blocks),
        grid_spec=pltpu.PrefetchScalarGridSpec(
            num_scalar_prefetch=3, grid=(n_tok // TR,),
            in_specs=[pl.BlockSpec((TR * SUBLANES, LANES), lambda s, *_: (s, 0))],
            out_specs=pl.BlockSpec(memory_space=pl.ANY),
            scratch_shapes=[pltpu.VMEM((2, TR * SUBLANES, LANES), F32), pltpu.VMEM((MOE_BLK * SUBLANES, LANES), F32),
                            pltpu.SemaphoreType.DMA((2,)), pltpu.SemaphoreType.DMA]),
        out_shape=jax.ShapeDtypeStruct((n_slots * SUBLANES, LANES), F32),
        compiler_params=_params(("arbitrary",)),
        name="moe_dispatch",
    )(slots, zrow, n_used, h)

    def wspec(shape):
        return pl.BlockSpec((1, 1) + shape, lambda i, be, nu: (layer, be[i], 0, 0))

    ys = pl.pallas_call(
        _expert_kernel,
        grid_spec=pltpu.PrefetchScalarGridSpec(
            num_scalar_prefetch=2, grid=(n_blocks,),
            in_specs=[pl.BlockSpec(memory_space=pl.ANY), wspec((d, de)), wspec((d, de)), wspec((de, d))],
            out_specs=pl.BlockSpec((MOE_BLK * SUBLANES, LANES), lambda i, be, nu: (i, 0)),
            scratch_shapes=[pltpu.VMEM((EXPERT_RING, MOE_BLK * SUBLANES, LANES), F32),
                            pltpu.SemaphoreType.DMA((EXPERT_RING,)),
                            pltpu.VMEM((d, de), BF16), pltpu.VMEM((d, de), BF16), pltpu.VMEM((de, d), BF16)]),
        out_shape=jax.ShapeDtypeStruct((n_slots * SUBLANES, LANES), F32),
        compiler_params=_params(("arbitrary",)),
        name="moe_experts",
    )(blk_e, n_used, xs, w1, w3, w2)

    final = next_ip is None
    tile_off = ctx_len // TR if final else 0
    nt_out = nt - tile_off
    common_specs = [_row_spec(TR, d, tile_off), _seg_mod_spec(d, tile_off=tile_off), _row_spec(TR, LANES, tile_off),
                    pl.BlockSpec(memory_space=pl.ANY)]
    scratch = [pltpu.VMEM((COMBINE_RING, TOP_K, TR * SUBLANES, LANES), F32),
               pltpu.SemaphoreType.DMA((COMBINE_RING,))]
    stream_spec = _row_spec(TR, d)
    stream_shape = jax.ShapeDtypeStruct((b, nt_out * TR, d), F32)
    if final:
        return pl.pallas_call(
            functools.partial(_combine_kernel, n_tok=n_tok, rows_per_batch=n, tile_off=tile_off, tail=None,
                              n_tail_in=0),
            grid_spec=pltpu.PrefetchScalarGridSpec(
                num_scalar_prefetch=1, grid=(b, nt_out),
                in_specs=common_specs + [_full_spec((1, d))],
                out_specs=stream_spec, scratch_shapes=scratch),
            out_shape=stream_shape,
            compiler_params=_params(("arbitrary", "arbitrary")),
            name="moe_combine_final",
        )(slots, x, mods, gtab, ys, norm_f.reshape(1, d))
    outs = pl.pallas_call(
        functools.partial(_combine_kernel, n_tok=n_tok, rows_per_batch=n, tile_off=tile_off, tail=next_ip["tail"],
                          n_tail_in=len(next_ip["args"])),
        grid_spec=pltpu.PrefetchScalarGridSpec(
            num_scalar_prefetch=1, grid=(b, nt_out),
            in_specs=common_specs + [_seg_mod_spec(d), _full_spec((1, d))] + next_ip["in_specs"],
            out_specs=[stream_spec] + next_ip["out_specs"], scratch_shapes=scratch),
        out_shape=[stream_shape] + next_ip["out_shape"],
        compiler_params=_params(("arbitrary", "arbitrary")),
        name="moe_combine_in_proj",
    )(slots, x, mods, gtab, ys, next_mods, next_g.reshape(1, d), *next_ip["args"])
    return outs[0], outs[1:]


def kernel(x, c, ctx, c_ctx, w_mod, b_mod, norm_mix, norm_ffn, norm_f, conv_pw1, conv_pw1_b, conv_dw, conv_dw_b, conv_ln_g, conv_ln_b, conv_pw2, conv_pw2_b, attn_w_qkv, attn_w_o, attn_sink, lru_w_in, lru_conv_w, lru_conv_b, lru_wa, lru_ba, lru_wx, lru_bx, lru_lam, lru_w_out, moe_w_router, moe_router_bias, moe_w1, moe_w3, moe_w2):
    b, s, d = x.shape
    ctx_len = ctx.shape[1]
    depth = w_mod.shape[0]
    n = ctx_len + s
    assert ctx_len == TR and s % TR == 0 and d == SUBLANES * LANES
    assert moe_w_router.shape[1] == N_EXPERTS

    mods = _modulation(c, c_ctx, w_mod, b_mod)
    xs = jnp.concatenate([ctx, x], axis=1)

    wr = jnp.zeros((EXPERTS_PER_GROUP, SUBLANES, d), F32).at[:, :N_GROUPS].set(
        moe_w_router.T.reshape(N_GROUPS, EXPERTS_PER_GROUP, d).swapaxes(0, 1))
    wr = wr.reshape(ROUTER_ROWS, d).astype(BF16)
    rb = jnp.zeros((EXPERTS_PER_GROUP, SUBLANES), F32).at[:, :N_GROUPS].set(
        moe_router_bias.astype(F32).reshape(N_GROUPS, EXPERTS_PER_GROUP).T).reshape(-1, 1)

    def layer_params(i):
        slot = i // N_MIXERS
        return [dict(conv_pw1=conv_pw1, conv_pw1_b=conv_pw1_b, conv_dw=conv_dw, conv_dw_b=conv_dw_b,
                     conv_ln_g=conv_ln_g, conv_ln_b=conv_ln_b, conv_pw2=conv_pw2, conv_pw2_b=conv_pw2_b),
                dict(attn_w_qkv=attn_w_qkv, attn_w_o=attn_w_o, attn_sink=attn_sink),
                dict(lru_w_in=lru_w_in, lru_conv_w=lru_conv_w, lru_conv_b=lru_conv_b, lru_wa=lru_wa, lru_ba=lru_ba,
                     lru_wx=lru_wx, lru_bx=lru_bx, lru_lam=lru_lam, lru_w_out=lru_w_out)][i % N_MIXERS], slot

    def params_of(i):
        group, slot = layer_params(i)
        return {k: v[slot] for k, v in group.items()}

    p = params_of(0)
    ip = _in_proj(0, p, b, n, d, ctx_len)
    a_out = _in_proj_standalone(ip, xs, mods[0], norm_mix[0])
    if not isinstance(a_out, (list, tuple)):
        a_out = [a_out]
    for i in range(depth):
        kind = i % N_MIXERS
        route_args = (norm_ffn[i].reshape(1, d), wr, rb)
        if kind == 0:
            res = _conv_core(xs, mods[i], a_out[0], p, route_args)
        elif kind == 1:
            res = _attn_core(xs, mods[i], a_out, p, route_args, ctx_len)
        else:
            res = _lru_core(xs, mods[i], a_out, p, route_args)
        x_new, routed = res[0], res[1:]
        if i == depth - 1:
            return _moe(x_new, mods[i], routed, i, moe_w1, moe_w3, moe_w2, ctx_len, norm_f=norm_f)
        p = params_of(i + 1)
        ip = _in_proj((i + 1) % N_MIXERS, p, b, n, d, ctx_len)
        xs, a_out = _moe(x_new, mods[i], routed, i, moe_w1, moe_w3, moe_w2, ctx_len,
                         next_ip=ip, next_mods=mods[i + 1], next_g=norm_mix[i + 1])
```

```python
import functools

import jax
import jax.numpy as jnp
from jax import lax
from jax.experimental import pallas as pl
from jax.experimental.pallas import tpu as pltpu

F32 = jnp.float32
BF16 = jnp.bfloat16
I32 = jnp.int32

EPS = 1e-6
N_MOD = 6
N_MIXERS = 3
GRID_W = 64
CONV_WIDTH = 31
CONV_HALF = (CONV_WIDTH - 1) // 2
HEAD_DIM = 64
N_KV_HEADS = 4
GQA_GROUP = 4
WINDOW = 128
ROPE_BASE = 10000.0
ROPE_PAIRS = HEAD_DIM // 4
LRU_BLOCKS = 8
LRU_CONV_W = 4
LRU_C = 8.0
N_EXPERTS = 16
N_GROUPS = 4
EXPERTS_PER_GROUP = 4
TOP_K = 2

LANES = 128
SUBLANES = 8
TR = 256
TQ = 128
HALO = 16
MOE_BLK = 512
COMBINE_RING = 3
EXPERT_RING = 3
ROUTER_ROWS = SUBLANES * EXPERTS_PER_GROUP
NEG = -1e30
VMEM_LIMIT = 56 * 1024 * 1024


def _params(sem, vmem=VMEM_LIMIT):
    return pltpu.CompilerParams(dimension_semantics=sem, vmem_limit_bytes=vmem)


def _norm_mod(x, g, shift, scale):
    y = x * lax.rsqrt(jnp.mean(x * x, axis=-1, keepdims=True) + EPS)
    return (y * g) * (1.0 + scale) + shift


def _seg_mod_spec(d, tiles_per_seg=1, tile_off=0):
    return pl.BlockSpec((1, 1, N_MOD, d),
                        lambda b, t, *_: (b, jnp.minimum((t + tile_off) // tiles_per_seg, 1), 0, 0))


def _row_spec(tr, d, tile_off=0):
    return pl.BlockSpec((1, tr, d), lambda b, t, *_: (b, t + tile_off, 0))


def _full_spec(shape):
    nd = len(shape)
    return pl.BlockSpec(shape, lambda b, t, *_: (0,) * nd)


def _mod_kernel(c_ref, w_ref, b_ref, o_ref):
    c = c_ref[...]
    sc = c * jax.nn.sigmoid(c)
    o_ref[0] = jnp.dot(sc.astype(BF16), w_ref[0].astype(BF16), preferred_element_type=F32) + b_ref[0]


def _modulation(c, c_ctx, w_mod, b_mod):
    depth, d, nout = w_mod.shape
    b = c.shape[0]
    assert b + 1 <= SUBLANES
    rows = jnp.zeros((SUBLANES, d), F32).at[:b].set(c).at[b].set(c_ctx)
    tn = 1536
    raw = pl.pallas_call(
        _mod_kernel,
        grid=(depth, nout // tn),
        in_specs=[pl.BlockSpec((SUBLANES, d), lambda i, j: (0, 0)),
                  pl.BlockSpec((1, d, tn), lambda i, j: (i, 0, j)),
                  pl.BlockSpec((1, 1, tn), lambda i, j: (i, 0, j))],
        out_specs=pl.BlockSpec((1, SUBLANES, tn), lambda i, j: (i, 0, j)),
        out_shape=jax.ShapeDtypeStruct((depth, SUBLANES, nout), F32),
        compiler_params=_params(("arbitrary", "arbitrary")),
        name="modulation",
    )(rows, w_mod, b_mod.reshape(depth, 1, nout))
    raw = raw.reshape(depth, SUBLANES, N_MOD, d)
    lat = raw[:, :b]
    ctx = jnp.broadcast_to(raw[:, b][:, None], lat.shape)
    return jnp.stack([ctx, lat], axis=2)


def _store_token_tiles(ref, mat):
    rows = mat.shape[0]
    for s in range(SUBLANES):
        ref[pl.ds(s, rows, stride=SUBLANES), :] = mat[:, s * LANES:(s + 1) * LANES]


def _load_token_tiles(ref, rows):
    return jnp.concatenate([ref[pl.ds(s, rows, stride=SUBLANES), :] for s in range(SUBLANES)], axis=1)


def _tile_rows(ref, first_token, n_tokens):
    first_row = first_token * SUBLANES
    if not isinstance(first_row, int):
        first_row = pl.multiple_of(first_row, SUBLANES)
    return ref.at[pl.ds(first_row, n_tokens * SUBLANES)]


def _route_tile(x, m, g_ref, wr_ref, rb_ref, carry, first, h_ref, e0_ref, e1_ref, p0_ref, p1_ref, gt_ref, cnt_ref):
    tr = x.shape[0]

    @pl.when(first)
    def _():
        carry[...] = jnp.zeros_like(carry)

    h = _norm_mod(x, g_ref[...], m[3:4], m[4:5])
    _store_token_tiles(h_ref, h)
    logit = lax.dot_general(wr_ref[...], h.astype(BF16), (((1,), (1,)), ((), ())),
                            preferred_element_type=F32)
    s = jax.nn.sigmoid(logit)
    sel = s + rb_ref[...]
    sj = [s[SUBLANES * j:SUBLANES * (j + 1)] for j in range(EXPERTS_PER_GROUP)]
    cj = [sel[SUBLANES * j:SUBLANES * (j + 1)] for j in range(EXPERTS_PER_GROUP)]

    hi1, lo1 = jnp.maximum(cj[0], cj[1]), jnp.minimum(cj[0], cj[1])
    hi2, lo2 = jnp.maximum(cj[2], cj[3]), jnp.minimum(cj[2], cj[3])
    top1 = jnp.maximum(hi1, hi2)
    top2 = jnp.maximum(jnp.minimum(hi1, hi2), jnp.maximum(lo1, lo2))
    row = lax.broadcasted_iota(I32, (SUBLANES, tr), 0)
    gscore = jnp.where(row < N_GROUPS, top1 + top2, -jnp.inf)
    gmax = jnp.max(gscore, axis=0, keepdims=True)
    grp = jnp.min(jnp.where(gscore == gmax, row, SUBLANES), axis=0, keepdims=True)
    pick = row == grp
    c = [jnp.sum(jnp.where(pick, v, 0.0), axis=0, keepdims=True) for v in cj]
    w = [jnp.sum(jnp.where(pick, v, 0.0), axis=0, keepdims=True) for v in sj]

    def argtop(vals):
        best, bi, bw = vals[0], jnp.zeros_like(grp), w[0]
        for j in range(1, EXPERTS_PER_GROUP):
            better = vals[j] > best
            best = jnp.where(better, vals[j], best)
            bi = jnp.where(better, j, bi)
            bw = jnp.where(better, w[j], bw)
        return bi, bw

    i0, w0 = argtop(c)
    i1, w1 = argtop([jnp.where(i0 == j, -jnp.inf, c[j]) for j in range(EXPERTS_PER_GROUP)])
    e0 = grp * EXPERTS_PER_GROUP + i0
    e1 = grp * EXPERTS_PER_GROUP + i1
    wsum = w0 + w1
    g0 = w0 / wsum
    g1 = w1 / wsum

    erow = lax.broadcasted_iota(I32, (N_EXPERTS, tr), 0)
    member = jnp.logical_or(erow == e0, erow == e1)
    upper = (lax.broadcasted_iota(I32, (tr, tr), 0) < lax.broadcasted_iota(I32, (tr, tr), 1))
    before = jnp.dot(member.astype(BF16), upper.astype(BF16), preferred_element_type=F32) + carry[...]
    p0 = jnp.sum(jnp.where(erow == e0, before, 0.0), axis=0, keepdims=True)
    p1 = jnp.sum(jnp.where(erow == e1, before, 0.0), axis=0, keepdims=True)
    total = carry[...] + jnp.sum(member.astype(F32), axis=1, keepdims=True)
    carry[...] = total
    cnt_ref[...] = total.astype(I32)

    e0_ref[0] = e0
    e1_ref[0] = e1
    p0_ref[0] = p0.astype(I32)
    p1_ref[0] = p1.astype(I32)
    grow = lax.broadcasted_iota(I32, (LANES, tr), 0)
    gmat = jnp.where(grow == 0, g0, jnp.where(grow == 1, g1, 0.0))
    gt_ref[0] = gmat.T


def _route_io(b, n, d, tile, tile_of):
    nt = n // tile

    def flat(bb, t):
        return bb * nt + tile_of(t)

    lane_major = pl.BlockSpec((1, 1, tile), lambda bb, t, *_: (flat(bb, t), 0, 0))
    lane_major_shape = jax.ShapeDtypeStruct((b * nt, 1, tile), I32)
    in_specs = [_full_spec((1, d)), _full_spec((ROUTER_ROWS, d)), _full_spec((ROUTER_ROWS, 1))]
    out_specs = [pl.BlockSpec((tile * SUBLANES, LANES), lambda bb, t, *_: (flat(bb, t), 0)),
                 lane_major, lane_major, lane_major, lane_major,
                 pl.BlockSpec((1, tile, LANES), lambda bb, t, *_: (bb, tile_of(t), 0)),
                 pl.BlockSpec((N_EXPERTS, 1), lambda bb, t, *_: (0, 0))]
    out_shape = [jax.ShapeDtypeStruct((b * n * SUBLANES, LANES), F32), lane_major_shape, lane_major_shape,
                 lane_major_shape, lane_major_shape,
                 jax.ShapeDtypeStruct((b, n, LANES), F32), jax.ShapeDtypeStruct((N_EXPERTS, 1), I32)]
    scratch = [pltpu.VMEM((N_EXPERTS, 1), F32)]
    return in_specs, out_specs, out_shape, scratch


def _first_step():
    return jnp.logical_and(pl.program_id(0) == 0, pl.program_id(1) == 0)


def _glu_tail(h, w_ref, b_ref, o_ref):
    d = h.shape[-1]
    u = jnp.dot(h.astype(BF16), w_ref[...], preferred_element_type=F32) + b_ref[...]
    o_ref[0] = u[:, :d] * jax.nn.sigmoid(u[:, d:])


def _qkv_tail(h, w_ref, cos_ref, sin_ref, q_ref, k_ref, v_ref):
    d = h.shape[-1]
    nk = k_ref.shape[-1]
    t = jnp.dot(h.astype(BF16), w_ref[...], preferred_element_type=F32)
    cos = cos_ref[...]
    sin = sin_ref[...]
    lane = lax.broadcasted_iota(I32, cos.shape, 1)
    first_half = (lane % (2 * ROPE_PAIRS)) < ROPE_PAIRS

    def rope(xg):
        partner = jnp.where(first_half, pltpu.roll(xg, LANES - ROPE_PAIRS, 1), pltpu.roll(xg, ROPE_PAIRS, 1))
        return xg * cos + partner * sin

    for j in range(d // LANES):
        cs = slice(j * LANES, (j + 1) * LANES)
        q_ref[0, :, cs] = rope(t[:, cs]).astype(BF16)
    for j in range(nk // LANES):
        cs = slice(j * LANES, (j + 1) * LANES)
        k_ref[0, :, cs] = rope(t[:, d + j * LANES:d + (j + 1) * LANES]).astype(BF16)
    v_ref[0] = t[:, d + nk:].astype(BF16)


def _lru_in_tail(h, w_ref, u_ref, gg_ref):
    d = h.shape[-1]
    t = jnp.dot(h.astype(BF16), w_ref[...], preferred_element_type=F32)
    u_ref[0] = t[:, :d]
    gg_ref[0] = jax.nn.gelu(t[:, d:])


def _rope_tables(n, ctx_len):
    pos = jnp.arange(n - ctx_len, dtype=I32)
    inv = ROPE_BASE ** (-jnp.arange(ROPE_PAIRS, dtype=F32) / ROPE_PAIRS)
    ar = (pos // GRID_W).astype(F32)[:, None] * inv
    ac = (pos % GRID_W).astype(F32)[:, None] * inv
    ang = jnp.concatenate([ar, ar, ac, ac], axis=-1)
    sign = jnp.tile(jnp.concatenate([-jnp.ones((ROPE_PAIRS,), F32), jnp.ones((ROPE_PAIRS,), F32)]), 2)
    cos = jnp.concatenate([jnp.ones((ctx_len, HEAD_DIM), F32), jnp.cos(ang)], axis=0)
    sin = jnp.concatenate([jnp.zeros((ctx_len, HEAD_DIM), F32), jnp.sin(ang) * sign], axis=0)
    return jnp.tile(cos, (1, 2)), jnp.tile(sin, (1, 2))


def _in_proj(kind, p, b, n, d, ctx_len):
    if kind == 0:
        args = (p["conv_pw1"].astype(BF16), p["conv_pw1_b"].reshape(1, 2 * d))
        return dict(tail=_glu_tail, args=args, in_specs=[_full_spec((d, 2 * d)), _full_spec((1, 2 * d))],
                    out_specs=[_row_spec(TR, d)], out_shape=[jax.ShapeDtypeStruct((b, n, d), F32)])
    if kind == 1:
        kvd = N_KV_HEADS * HEAD_DIM
        nk = N_KV_HEADS * LANES
        assert d == 2 * nk
        scale = HEAD_DIM ** -0.5
        w_qkv = p["attn_w_qkv"]

        def dup(w):
            w = w.reshape(d, N_KV_HEADS, HEAD_DIM)
            return jnp.concatenate([w, w], axis=-1).reshape(d, nk)

        w_all = jnp.concatenate([w_qkv[:, :d] * scale, dup(w_qkv[:, d:d + kvd]), dup(w_qkv[:, d + kvd:])],
                                axis=1).astype(BF16)
        cos, sin = _rope_tables(n, ctx_len)
        table = pl.BlockSpec((TR, LANES), lambda bb, t, *_: (t, 0))
        return dict(tail=_qkv_tail, args=(w_all, cos, sin), in_specs=[_full_spec((d, d + 2 * nk)), table, table],
                    out_specs=[_row_spec(TR, d), _row_spec(TR, nk), _row_spec(TR, nk)],
                    out_shape=[jax.ShapeDtypeStruct((b, n, d), BF16), jax.ShapeDtypeStruct((b, n, nk), BF16),
                               jax.ShapeDtypeStruct((b, n, nk), BF16)])
    return dict(tail=_lru_in_tail, args=(p["lru_w_in"].astype(BF16),), in_specs=[_full_spec((d, 2 * d))],
                out_specs=[_row_spec(TR, d), _row_spec(TR, d)],
                out_shape=[jax.ShapeDtypeStruct((b, n, d), F32), jax.ShapeDtypeStruct((b, n, d), F32)])


def _in_proj_kernel(x_ref, mod_ref, g_ref, *refs, tail):
    m = mod_ref[0, 0]
    tail(_norm_mod(x_ref[0], g_ref[...], m[0:1], m[1:2]), *refs)


def _in_proj_standalone(ip, x, mods, g):
    b, n, d = x.shape
    return pl.pallas_call(
        functools.partial(_in_proj_kernel, tail=ip["tail"]), grid=(b, n // TR),
        in_specs=[_row_spec(TR, d), _seg_mod_spec(d), _full_spec((1, d))] + ip["in_specs"],
        out_specs=ip["out_specs"], out_shape=ip["out_shape"],
        compiler_params=_params(("parallel", "parallel")),
        name="in_proj",
    )(x, mods, g.reshape(1, d), *ip["args"])


def _conv_core_kernel(x_ref, mod_ref, gp_ref, gc_ref, gn_ref, dw_ref, dwb_ref, lng_ref, lnb_ref, w_ref, b_ref,
                      gffn_ref, wr_ref, rb_ref, o_ref, h_ref, e0_ref, e1_ref, p0_ref, p1_ref, gt_ref, cnt_ref,
                      gbuf, cbuf, carry):
    t = pl.program_id(1)
    nt = pl.num_programs(1)
    first = _first_step()
    tr, d = gc_ref.shape[1], gc_ref.shape[2]
    seg_first = t <= 1
    seg_last = jnp.logical_or(t == 0, t == nt - 1)
    gbuf[0, 0:HALO, :] = jnp.where(seg_first, 0.0, gp_ref[0])
    gbuf[0, HALO:HALO + tr, :] = gc_ref[0]
    gbuf[0, HALO + tr:HALO + tr + HALO, :] = jnp.where(seg_last, 0.0, gn_ref[0])
    span = tr + 2 * HALO - SUBLANES
    for r in range(1, SUBLANES):
        gbuf[r, 0:span, :] = gbuf[0, r:r + span, :]

    rc = 32
    base = HALO - CONV_HALF

    def chunk(i, carry_):
        r0 = pl.multiple_of(i * rc, rc)
        for c in range(d // LANES):
            cs = slice(c * LANES, (c + 1) * LANES)
            acc = jnp.zeros((rc, LANES), F32)
            for k in range(CONV_WIDTH):
                off = base + k
                rows = pl.ds(r0 + (off // SUBLANES) * SUBLANES, rc)
                acc = acc + dw_ref[k:k + 1, cs] * gbuf[off % SUBLANES, rows, cs]
            cbuf[pl.ds(r0, rc), cs] = acc
        return carry_

    lax.fori_loop(0, tr // rc, chunk, 0)

    u = cbuf[...] + dwb_ref[...]
    mu = jnp.mean(u, axis=-1, keepdims=True)
    uc = u - mu
    var = jnp.mean(uc * uc, axis=-1, keepdims=True)
    v = uc * lax.rsqrt(var + EPS) * lng_ref[...] + lnb_ref[...]
    v = v * jax.nn.sigmoid(v)
    y = jnp.dot(v.astype(BF16), w_ref[...], preferred_element_type=F32) + b_ref[...]
    m = mod_ref[0, 0]
    x_new = x_ref[0] + m[2:3] * y
    o_ref[0] = x_new
    _route_tile(x_new, m, gffn_ref, wr_ref, rb_ref, carry, first,
                h_ref, e0_ref, e1_ref, p0_ref, p1_ref, gt_ref, cnt_ref)


def _conv_core(x, mods, glu, p, route_args):
    b, n, d = x.shape
    nt = n // TR
    hb = TR // HALO
    nh = n // HALO
    dwp = jnp.zeros((32, d), F32).at[:CONV_WIDTH].set(p["conv_dw"])
    r_in, r_out, r_shape, r_scratch = _route_io(b, n, d, TR, lambda t: t)
    return pl.pallas_call(
        _conv_core_kernel, grid=(b, nt),
        in_specs=[_row_spec(TR, d), _seg_mod_spec(d),
                  pl.BlockSpec((1, HALO, d), lambda bb, t: (bb, jnp.maximum(t * hb - 1, 0), 0)),
                  _row_spec(TR, d),
                  pl.BlockSpec((1, HALO, d), lambda bb, t: (bb, jnp.minimum((t + 1) * hb, nh - 1), 0)),
                  _full_spec((32, d)), _full_spec((1, d)), _full_spec((1, d)), _full_spec((1, d)),
                  _full_spec((d, d)), _full_spec((1, d))] + r_in,
        out_specs=[_row_spec(TR, d)] + r_out,
        out_shape=[jax.ShapeDtypeStruct((b, n, d), F32)] + r_shape,
        scratch_shapes=[pltpu.VMEM((SUBLANES, TR + 2 * HALO, d), F32), pltpu.VMEM((TR, d), F32)] + r_scratch,
        compiler_params=_params(("arbitrary", "arbitrary")),
        name="conv_core",
    )(x, mods, glu, glu, glu, dwp, p["conv_dw_b"].reshape(1, d), p["conv_ln_g"].reshape(1, d),
      p["conv_ln_b"].reshape(1, d), p["conv_pw2"].astype(BF16), p["conv_pw2_b"].reshape(1, d), *route_args)


def _attn_core_kernel(sink_ref, x_ref, mod_ref, q_ref, kc_ref, vc_ref, kp_ref, km_ref, kn_ref,
                      vp_ref, vm_ref, vn_ref, wo_ref, gffn_ref, wr_ref, rb_ref,
                      o_ref, h_ref, e0_ref, e1_ref, p0_ref, p1_ref, gt_ref, cnt_ref, carry, *, ctx_len, n_rows):
    first = _first_step()
    tq = kp_ref.shape[1]
    nctx = kc_ref.shape[1]
    nkeys = nctx + 3 * tq
    lane = lax.broadcasted_iota(I32, (tq, LANES), 1)
    lo = lane < HEAD_DIM
    col = lax.broadcasted_iota(I32, (tq, nkeys), 1)
    row = lax.broadcasted_iota(I32, (tq, nkeys), 0)

    def attend(t, q_rows, k_parts, v_parts):
        rq = t * tq + row
        rk = (t - 1) * tq + (col - nctx)
        win_ok = (rk >= ctx_len) & (rk < n_rows) & (jnp.abs(rq - rk) <= WINDOW) & (t * tq >= ctx_len)
        bias = jnp.where((col < nctx) | win_ok, 0.0, NEG).astype(F32)
        out_cols = []
        for g in range(N_KV_HEADS):
            gs = slice(g * LANES, (g + 1) * LANES)
            kg = jnp.concatenate([kc_ref[0, :, gs]] + [part(gs) for part in k_parts], axis=0)
            vg = jnp.concatenate([vc_ref[0, :, gs]] + [part(gs) for part in v_parts], axis=0)
            qa = q_ref[0, q_rows, (2 * g) * LANES:(2 * g + 1) * LANES]
            qb = q_ref[0, q_rows, (2 * g + 1) * LANES:(2 * g + 2) * LANES]
            zero = jnp.zeros_like(qa)
            q4 = jnp.concatenate([jnp.where(lo, qa, zero), jnp.where(lo, zero, qa),
                                  jnp.where(lo, qb, zero), jnp.where(lo, zero, qb)], axis=0)
            s = lax.dot_general(q4, kg, (((1,), (1,)), ((), ())), preferred_element_type=F32)
            ps, dens = [], []
            for hh in range(GQA_GROUP):
                sk = sink_ref[g * GQA_GROUP + hh]
                sh = s[hh * tq:(hh + 1) * tq] + bias
                mx = jnp.maximum(jnp.max(sh, axis=-1, keepdims=True), sk)
                pr = jnp.exp(sh - mx)
                dens.append(jnp.sum(pr, axis=-1, keepdims=True) + jnp.exp(sk - mx))
                ps.append(pr.astype(BF16))
            pv = jnp.dot(jnp.concatenate(ps, axis=0), vg, preferred_element_type=F32)
            oh = [pv[hh * tq:(hh + 1) * tq] / dens[hh] for hh in range(GQA_GROUP)]
            out_cols.append(jnp.where(lo, oh[0], oh[1]))
            out_cols.append(jnp.where(lo, oh[2], oh[3]))
        return jnp.concatenate(out_cols, axis=1).astype(BF16)

    def rows_of(ref, r0):
        return lambda gs: ref[0, r0:r0 + tq, gs]

    t2 = 2 * pl.program_id(1)
    o = jnp.concatenate([
        attend(t2, slice(0, tq), [rows_of(kp_ref, 0), rows_of(km_ref, 0), rows_of(km_ref, tq)],
               [rows_of(vp_ref, 0), rows_of(vm_ref, 0), rows_of(vm_ref, tq)]),
        attend(t2 + 1, slice(tq, 2 * tq), [rows_of(km_ref, 0), rows_of(km_ref, tq), rows_of(kn_ref, 0)],
               [rows_of(vm_ref, 0), rows_of(vm_ref, tq), rows_of(vn_ref, 0)])], axis=0)
    y = jnp.dot(o, wo_ref[...], preferred_element_type=F32)
    m = mod_ref[0, 0]
    x_new = x_ref[0] + m[2:3] * y
    o_ref[0] = x_new
    _route_tile(x_new, m, gffn_ref, wr_ref, rb_ref, carry, first,
                h_ref, e0_ref, e1_ref, p0_ref, p1_ref, gt_ref, cnt_ref)


def _attn_core(x, mods, qkv, p, route_args, ctx_len):
    b, n, d = x.shape
    q, k, v = qkv
    nk = k.shape[-1]
    assert ctx_len == TR and TR == 2 * TQ
    ntq = n // TQ

    prev_spec = pl.BlockSpec((1, TQ, nk), lambda bb, t, s: (bb, jnp.maximum(2 * t - 1, 0), 0))
    next_spec = pl.BlockSpec((1, TQ, nk), lambda bb, t, s: (bb, jnp.minimum(2 * t + 2, ntq - 1), 0))
    ctx_spec = pl.BlockSpec((1, ctx_len, nk), lambda bb, t, s: (bb, 0, 0))
    r_in, r_out, r_shape, r_scratch = _route_io(b, n, d, TR, lambda t: t)
    grid_spec = pltpu.PrefetchScalarGridSpec(
        num_scalar_prefetch=1, grid=(b, n // TR),
        in_specs=[_row_spec(TR, d), _seg_mod_spec(d), _row_spec(TR, d),
                  ctx_spec, ctx_spec, prev_spec, _row_spec(TR, nk), next_spec, prev_spec, _row_spec(TR, nk), next_spec,
                  _full_spec((d, d))] + r_in,
        out_specs=[_row_spec(TR, d)] + r_out,
        scratch_shapes=r_scratch)
    return pl.pallas_call(
        functools.partial(_attn_core_kernel, ctx_len=ctx_len, n_rows=n),
        grid_spec=grid_spec,
        out_shape=[jax.ShapeDtypeStruct((b, n, d), F32)] + r_shape,
        compiler_params=_params(("arbitrary", "arbitrary")),
        name="attn_core",
    )(p["attn_sink"].astype(F32), x, mods, q, k, v, k, k, k, v, v, v, p["attn_w_o"].astype(BF16), *route_args)


def _lru_gates_scan(back, t, u_ref, cw_ref, cb_ref, wg_ref, ba_ref, bx_ref, lam_ref,
                    ubuf, halo, hcar, abuf, bbuf, hbuf):
    tr, d = u_ref.shape[1], u_ref.shape[2]
    nb = d // LRU_BLOCKS

    @pl.when(t <= 1)
    def _():
        halo[...] = jnp.zeros_like(halo)

    @pl.when(t == 0)
    def _():
        hcar[...] = jnp.zeros_like(hcar)

    u = u_ref[0]
    if not back:
        ubuf[0:SUBLANES, :] = halo[...]
        ubuf[SUBLANES:SUBLANES + tr, :] = u
        halo[...] = u[tr - SUBLANES:, :]
        taps = [ubuf[SUBLANES - (LRU_CONV_W - 1) + k:SUBLANES - (LRU_CONV_W - 1) + k + tr, :]
                for k in range(LRU_CONV_W)]
    else:
        ubuf[0:tr, :] = u
        ubuf[tr:tr + SUBLANES, :] = halo[...]
        halo[...] = u[:SUBLANES, :]
        taps = [ubuf[(LRU_CONV_W - 1) - k:(LRU_CONV_W - 1) - k + tr, :] for k in range(LRU_CONV_W)]
    cc = cb_ref[...] + taps[0] * cw_ref[0:1, :]
    for k in range(1, LRU_CONV_W):
        cc = cc + taps[k] * cw_ref[k:k + 1, :]

    ccb = cc.astype(BF16)
    lam = lam_ref[...]
    neg_c_softplus = -LRU_C * (jnp.maximum(-lam, 0.0) + jnp.log(1.0 + jnp.exp(-jnp.abs(lam))))
    for blk in range(LRU_BLOCKS):
        cs = slice(blk * nb, (blk + 1) * nb)
        z = jnp.dot(ccb[:, cs], wg_ref[blk], preferred_element_type=F32)
        r = jax.nn.sigmoid(z[:, :nb] + ba_ref[:, cs])
        gi = jax.nn.sigmoid(z[:, nb:] + bx_ref[:, cs])
        log_a = neg_c_softplus[:, cs] * r
        a = jnp.exp(log_a)
        abuf[:, cs] = a
        y = 1.0 - a * a
        bbuf[:, cs] = jnp.where(y > 0.0, y * lax.rsqrt(y), 0.0) * (gi * cc[:, cs])

    row = lax.broadcasted_iota(I32, (SUBLANES, d), 0)
    nchunk = tr // SUBLANES

    def chunk(i, h):
        ci = (nchunk - 1 - i) if back else i
        r0 = pl.multiple_of(ci * SUBLANES, SUBLANES)
        a = abuf[pl.ds(r0, SUBLANES), :]
        bb = bbuf[pl.ds(r0, SUBLANES), :]
        for k in (1, 2, 4):
            if back:
                sh, ok = SUBLANES - k, row < SUBLANES - k
            else:
                sh, ok = k, row >= k
            a_s = pltpu.roll(a, sh, 0)
            b_s = pltpu.roll(bb, sh, 0)
            bb = jnp.where(ok, a * b_s + bb, bb)
            a = jnp.where(ok, a * a_s, a)
        hh = a * h + bb
        hbuf[pl.ds(r0, SUBLANES), :] = hh
        return hh[0:1, :] if back else hh[SUBLANES - 1:SUBLANES, :]

    hcar[0:1, :] = lax.fori_loop(0, nchunk, chunk, hcar[0:1, :])


def _lru_fwd_kernel(u_ref, cw_ref, cb_ref, wg_ref, ba_ref, bx_ref, lam_ref, hf_ref,
                    ubuf, halo, hcar, abuf, bbuf, hbuf):
    _lru_gates_scan(False, pl.program_id(1), u_ref, cw_ref, cb_ref, wg_ref, ba_ref, bx_ref, lam_ref,
                    ubuf, halo, hcar, abuf, bbuf, hbuf)
    hf_ref[0] = hbuf[...]


def _lru_bwd_kernel(u_ref, cw_ref, cb_ref, wg_ref, ba_ref, bx_ref, lam_ref, hf_ref, gg_ref, x_ref, mod_ref,
                    wo_ref, gffn_ref, wr_ref, rb_ref, o_ref, h_ref, e0_ref, e1_ref, p0_ref, p1_ref, gt_ref, cnt_ref,
                    ubuf, halo, hcar, abuf, bbuf, hbuf, carry):
    first = _first_step()
    _lru_gates_scan(True, pl.program_id(1), u_ref, cw_ref, cb_ref, wg_ref, ba_ref, bx_ref, lam_ref,
                    ubuf, halo, hcar, abuf, bbuf, hbuf)
    y = (hf_ref[0] + hbuf[...]) * gg_ref[0]
    out = jnp.dot(y.astype(BF16), wo_ref[...], preferred_element_type=F32)
    m = mod_ref[0, 0]
    x_new = x_ref[0] + m[2:3] * out
    o_ref[0] = x_new
    _route_tile(x_new, m, gffn_ref, wr_ref, rb_ref, carry, first,
                h_ref, e0_ref, e1_ref, p0_ref, p1_ref, gt_ref, cnt_ref)


def _lru_core(x, mods, ugg, p, route_args):
    b, n, d = x.shape
    u, gg = ugg
    nt = n // TR
    nb = d // LRU_BLOCKS
    wg = jnp.concatenate([p["lru_wa"], p["lru_wx"]], axis=-1).astype(BF16)
    scratch = [pltpu.VMEM((TR + SUBLANES, d), F32), pltpu.VMEM((SUBLANES, d), F32),
               pltpu.VMEM((SUBLANES, d), F32), pltpu.VMEM((TR, d), F32), pltpu.VMEM((TR, d), F32),
               pltpu.VMEM((TR, d), F32)]

    def gate_specs(order):
        return [pl.BlockSpec((1, TR, d), lambda bb, t: (bb, order(t), 0)),
                _full_spec((LRU_CONV_W, d)), _full_spec((1, d)), _full_spec((LRU_BLOCKS, nb, 2 * nb)),
                _full_spec((1, d)), _full_spec((1, d)), _full_spec((1, d))]

    def gate_args(dd):
        return (u, p["lru_conv_w"][dd], p["lru_conv_b"][dd].reshape(1, d), wg[dd], p["lru_ba"][dd].reshape(1, d),
                p["lru_bx"][dd].reshape(1, d), p["lru_lam"][dd].reshape(1, d))

    hf = pl.pallas_call(
        _lru_fwd_kernel, grid=(b, nt),
        in_specs=gate_specs(lambda t: t),
        out_specs=_row_spec(TR, d),
        out_shape=jax.ShapeDtypeStruct((b, n, d), F32),
        scratch_shapes=scratch,
        compiler_params=_params(("arbitrary", "arbitrary")),
        name="lru_fwd",
    )(*gate_args(0))

    def rev(t):
        return jnp.where(t == 0, 0, nt - t)

    def rev_spec():
        return pl.BlockSpec((1, TR, d), lambda bb, t: (bb, rev(t), 0))

    r_in, r_out, r_shape, r_scratch = _route_io(b, n, d, TR, rev)
    return pl.pallas_call(
        _lru_bwd_kernel, grid=(b, nt),
        in_specs=gate_specs(rev) + [rev_spec(), rev_spec(), rev_spec(), _seg_mod_spec(d), _full_spec((d, d))] + r_in,
        out_specs=[rev_spec()] + r_out,
        out_shape=[jax.ShapeDtypeStruct((b, n, d), F32)] + r_shape,
        scratch_shapes=scratch + r_scratch,
        compiler_params=_params(("arbitrary", "arbitrary")),
        name="lru_bwd",
    )(*gate_args(1), hf, gg, x, mods, p["lru_w_out"].astype(BF16), *route_args)


def _dispatch_kernel(slot_ref, zrow_ref, nu_ref, h_ref, xs_ref, ring, zbuf, sem, zsem, *, n_tok, n_blocks):
    s = pl.program_id(0)
    ns = pl.num_programs(0)
    tr = h_ref.shape[0] // SUBLANES
    blk = zbuf.shape[0] // SUBLANES
    par = s % 2

    @pl.when(s == 0)
    def _():
        zbuf[...] = jnp.zeros_like(zbuf)
        for e in range(N_EXPERTS):
            pltpu.make_async_copy(zbuf, _tile_rows(xs_ref, zrow_ref[e], blk), zsem).start()
        for e in range(N_EXPERTS):
            pltpu.make_async_copy(zbuf, _tile_rows(xs_ref, 0, blk), zsem).wait()

        def zero_blk(i, carry):
            pltpu.make_async_copy(zbuf, _tile_rows(xs_ref, i * blk, blk), zsem).start()
            return carry

        def zero_blk_wait(i, carry):
            pltpu.make_async_copy(zbuf, _tile_rows(xs_ref, 0, blk), zsem).wait()
            return carry

        lax.fori_loop(nu_ref[0], n_blocks, zero_blk, 0)
        lax.fori_loop(nu_ref[0], n_blocks, zero_blk_wait, 0)

    def wait_ring(p):
        for _ in range(TOP_K):
            pltpu.make_async_copy(ring.at[p], _tile_rows(xs_ref, 0, tr), sem.at[p]).wait()

    @pl.when(s >= 2)
    def _():
        wait_ring(par)

    ring[par] = h_ref[...]
    base = s * tr

    def issue(i, carry):
        for k in range(TOP_K):
            pltpu.make_async_copy(_tile_rows(ring.at[par], i, 1),
                                  _tile_rows(xs_ref, slot_ref[k * n_tok + base + i], 1),
                                  sem.at[par]).start(priority=k)
        return carry

    lax.fori_loop(0, tr, issue, 0, unroll=8)

    @pl.when(s == ns - 1)
    def _():
        wait_ring(par)

        @pl.when(ns >= 2)
        def _():
            wait_ring(1 - par)


def _expert_kernel(be_ref, nu_ref, xs_ref, w1_ref, w3_ref, w2_ref, ys_ref, xring, sem, w1b, w3b, w2b):
    i = pl.program_id(0)
    nb = pl.num_programs(0)
    blk = xring.shape[1] // SUBLANES
    used = i < nu_ref[0]
    new_expert = jnp.logical_or(i == 0, be_ref[i] != be_ref[jnp.maximum(i - 1, 0)])
    ahead = EXPERT_RING - 1

    def fetch(block, ring):
        return pltpu.make_async_copy(_tile_rows(xs_ref, block * blk, blk), xring.at[ring], sem.at[ring])

    @pl.when(i == 0)
    def _():
        for first in range(ahead):
            fetch(jnp.minimum(first, nb - 1), first).start()

    @pl.when(jnp.logical_and(used, new_expert))
    def _():
        w1b[...] = w1_ref[0, 0].astype(BF16)
        w3b[...] = w3_ref[0, 0].astype(BF16)
        w2b[...] = w2_ref[0, 0].astype(BF16)

    @pl.when(used)
    def _():
        ring = i % EXPERT_RING
        fetch(0, ring).wait()
        fetch(jnp.minimum(i + ahead, nb - 1), (i + ahead) % EXPERT_RING).start()
        xb = _load_token_tiles(xring.at[ring], blk).astype(BF16)
        a = jnp.dot(xb, w1b[...], preferred_element_type=F32)
        b = jnp.dot(xb, w3b[...], preferred_element_type=F32)
        hid = (a * jax.nn.sigmoid(a)) * b
        _store_token_tiles(ys_ref, jnp.dot(hid.astype(BF16), w2b[...], preferred_element_type=F32))

        @pl.when(i == nu_ref[0] - 1)
        def _():
            for later in range(1, EXPERT_RING):
                fetch(0, (i + later) % EXPERT_RING).wait()

    @pl.when(jnp.logical_not(used))
    def _():
        ys_ref[...] = jnp.zeros_like(ys_ref)


def _combine_kernel(slot_ref, x_ref, mod_ref, gt_ref, ys_ref, *refs, n_tok, rows_per_batch, tile_off, tail, n_tail_in):
    if tail is None:
        nf_ref, o_ref, ybuf, sem = refs
    else:
        modn_ref, gn_ref = refs[0], refs[1]
        tail_in = refs[2:2 + n_tail_in]
        o_ref = refs[2 + n_tail_in]
        tail_out = refs[3 + n_tail_in:-2]
        ybuf, sem = refs[-2], refs[-1]
    bb, t = pl.program_id(0), pl.program_id(1)
    nt = pl.num_programs(1)
    tr = x_ref.shape[1]
    step = bb * nt + t
    nsteps = pl.num_programs(0) * nt

    def tok_base(s):
        return (s // nt) * rows_per_batch + (s % nt + tile_off) * tr

    def copy_row(base, i, ring, k):
        pltpu.make_async_copy(_tile_rows(ys_ref, slot_ref[k * n_tok + base + i], 1),
                              _tile_rows(ybuf.at[ring, k], i, 1), sem.at[ring]).start(priority=k)

    def wait_ring(ring):
        for k in range(TOP_K):
            pltpu.make_async_copy(_tile_rows(ys_ref, 0, tr), ybuf.at[ring, k], sem.at[ring]).wait()

    @pl.when(step == 0)
    def _():
        for first in range(COMBINE_RING - 1):
            base = tok_base(jnp.minimum(first, nsteps - 1))

            def one(i, carry, base=base, first=first):
                for k in range(TOP_K):
                    copy_row(base, i, first, k)
                return carry

            lax.fori_loop(0, tr, one, 0, unroll=8)

    ring = step % COMBINE_RING
    wait_ring(ring)
    gt = gt_ref[0]
    y = (gt[:, 0:1] * _load_token_tiles(ybuf.at[ring, 0], tr)
         + gt[:, 1:2] * _load_token_tiles(ybuf.at[ring, 1], tr))

    ahead = COMBINE_RING - 1
    fetch_base = tok_base(jnp.minimum(step + ahead, nsteps - 1))
    fetch_ring = (step + ahead) % COMBINE_RING
    for i in range(tr):
        for k in range(TOP_K):
            copy_row(fetch_base, i, fetch_ring, k)

    out = x_ref[0] + mod_ref[0, 0][5:6] * y
    if tail is None:
        o_ref[0] = out * lax.rsqrt(jnp.mean(out * out, axis=-1, keepdims=True) + EPS) * nf_ref[...]
    else:
        o_ref[0] = out
        mn = modn_ref[0, 0]
        tail(_norm_mod(out, gn_ref[...], mn[0:1], mn[1:2]), *tail_in, *tail_out)

    @pl.when(step == nsteps - 1)
    def _():
        for later in range(1, COMBINE_RING):
            wait_ring((step + later) % COMBINE_RING)


def _moe(x, mods, routed, layer, w1, w3, w2, ctx_len, norm_f=None, next_ip=None, next_mods=None, next_g=None):
    b, n, d = x.shape
    nt = n // TR
    n_tok = b * n
    de = w1.shape[-1]
    h, e0, e1, p0, p1, gtab, counts = routed

    counts = counts[:, 0]
    padded = (counts + MOE_BLK - 1) // MOE_BLK * MOE_BLK
    pend = jnp.cumsum(padded)
    pstart = (pend - padded).astype(I32)
    n_slots = n_tok * TOP_K + N_EXPERTS * MOE_BLK
    n_blocks = n_slots // MOE_BLK
    n_used = (pend[-1] // MOE_BLK).astype(I32).reshape(1)
    blk_start = jnp.arange(n_blocks, dtype=I32) * MOE_BLK
    blk_e = jnp.minimum(jnp.sum((pend[None, :] <= blk_start[:, None]).astype(I32), axis=1), N_EXPERTS - 1)

    def slot_of(e, p):
        e, p = e.reshape(n_tok), p.reshape(n_tok)
        start = jnp.zeros_like(e)
        for j in range(N_EXPERTS):
            start = jnp.where(e == j, pstart[j], start)
        return start + p

    slots = jnp.concatenate([slot_of(e0, p0), slot_of(e1, p1)])
    zrow = (pstart + counts).astype(I32)

    xs = pl.pallas_call(
        functools.partial(_dispatch_kernel, n_tok=n_tok, n_blocks=n_blocks),
        grid_spec=pltpu.PrefetchScalarGridSpec(
            num_scalar_prefetch=3, grid=(n_tok // TR,),
            in_specs=[pl.BlockSpec((TR * SUBLANES, LANES), lambda s, *_: (s, 0))],
            out_specs=pl.BlockSpec(memory_space=pl.ANY),
            scratch_shapes=[pltpu.VMEM((2, TR * SUBLANES, LANES), F32), pltpu.VMEM((MOE_BLK * SUBLANES, LANES), F32),
                            pltpu.SemaphoreType.DMA((2,)), pltpu.SemaphoreType.DMA]),
        out_shape=jax.ShapeDtypeStruct((n_slots * SUBLANES, LANES), F32),
        compiler_params=_params(("arbitrary",)),
        name="moe_dispatch",
    )(slots, zrow, n_used, h)

    def wspec(shape):
        return pl.BlockSpec((1, 1) + shape, lambda i, be, nu: (layer, be[i], 0, 0))

    ys = pl.pallas_call(
        _expert_kernel,
        grid_spec=pltpu.PrefetchScalarGridSpec(
            num_scalar_prefetch=2, grid=(n_blocks,),
            in_specs=[pl.BlockSpec(memory_space=pl.ANY), wspec((d, de)), wspec((d, de)), wspec((de, d))],
            out_specs=pl.BlockSpec((MOE_BLK * SUBLANES, LANES), lambda i, be, nu: (i, 0)),
            scratch_shapes=[pltpu.VMEM((EXPERT_RING, MOE_BLK * SUBLANES, LANES), F32),
                            pltpu.SemaphoreType.DMA((EXPERT_RING,)),
                            pltpu.VMEM((d, de), BF16), pltpu.VMEM((d, de), BF16), pltpu.VMEM((de, d), BF16)]),
        out_shape=jax.ShapeDtypeStruct((n_slots * SUBLANES, LANES), F32),
        compiler_params=_params(("arbitrary",)),
        name="moe_experts",
    )(blk_e, n_used, xs, w1, w3, w2)

    final = next_ip is None
    tile_off = ctx_len // TR if final else 0
    nt_out = nt - tile_off
    common_specs = [_row_spec(TR, d, tile_off), _seg_mod_spec(d, tile_off=tile_off), _row_spec(TR, LANES, tile_off),
                    pl.BlockSpec(memory_space=pl.ANY)]
    scratch = [pltpu.VMEM((COMBINE_RING, TOP_K, TR * SUBLANES, LANES), F32),
               pltpu.SemaphoreType.DMA((COMBINE_RING,))]
    stream_spec = _row_spec(TR, d)
    stream_shape = jax.ShapeDtypeStruct((b, nt_out * TR, d), F32)
    if final:
        return pl.pallas_call(
            functools.partial(_combine_kernel, n_tok=n_tok, rows_per_batch=n, tile_off=tile_off, tail=None,
                              n_tail_in=0),
            grid_spec=pltpu.PrefetchScalarGridSpec(
                num_scalar_prefetch=1, grid=(b, nt_out),
                in_specs=common_specs + [_full_spec((1, d))],
                out_specs=stream_spec, scratch_shapes=scratch),
            out_shape=stream_shape,
            compiler_params=_params(("arbitrary", "arbitrary")),
            name="moe_combine_final",
        )(slots, x, mods, gtab, ys, norm_f.reshape(1, d))
    outs = pl.pallas_call(
        functools.partial(_combine_kernel, n_tok=n_tok, rows_per_batch=n, tile_off=tile_off, tail=next_ip["tail"],
                          n_tail_in=len(next_ip["args"])),
        grid_spec=pltpu.PrefetchScalarGridSpec(
            num_scalar_prefetch=1, grid=(b, nt_out),
            in_specs=common_specs + [_seg_mod_spec(d), _full_spec((1, d))] + next_ip["in_specs"],
            out_specs=[stream_spec] + next_ip["out_specs"], scratch_shapes=scratch),
        out_shape=[stream_shape] + next_ip["out_shape"],
        compiler_params=_params(("arbitrary", "arbitrary")),
        name="moe_combine_in_proj",
    )(slots, x, mods, gtab, ys, next_mods, next_g.reshape(1, d), *next_ip["args"])
    return outs[0], outs[1:]


def kernel(x, c, ctx, c_ctx, w_mod, b_mod, norm_mix, norm_ffn, norm_f, conv_pw1, conv_pw1_b, conv_dw, conv_dw_b, conv_ln_g, conv_ln_b, conv_pw2, conv_pw2_b, attn_w_qkv, attn_w_o, attn_sink, lru_w_in, lru_conv_w, lru_conv_b, lru_wa, lru_ba, lru_wx, lru_bx, lru_lam, lru_w_out, moe_w_router, moe_router_bias, moe_w1, moe_w3, moe_w2):
    b, s, d = x.shape
    ctx_len = ctx.shape[1]
    depth = w_mod.shape[0]
    n = ctx_len + s
    assert ctx_len == TR and s % TR == 0 and d == SUBLANES * LANES
    assert moe_w_router.shape[1] == N_EXPERTS

    mods = _modulation(c, c_ctx, w_mod, b_mod)
    xs = jnp.concatenate([ctx, x], axis=1)

    wr = jnp.zeros((EXPERTS_PER_GROUP, SUBLANES, d), F32).at[:, :N_GROUPS].set(
        moe_w_router.T.reshape(N_GROUPS, EXPERTS_PER_GROUP, d).swapaxes(0, 1))
    wr = wr.reshape(ROUTER_ROWS, d).astype(BF16)
    rb = jnp.zeros((EXPERTS_PER_GROUP, SUBLANES), F32).at[:, :N_GROUPS].set(
        moe_router_bias.astype(F32).reshape(N_GROUPS, EXPERTS_PER_GROUP).T).reshape(-1, 1)

    def layer_params(i):
        slot = i // N_MIXERS
        return [dict(conv_pw1=conv_pw1, conv_pw1_b=conv_pw1_b, conv_dw=conv_dw, conv_dw_b=conv_dw_b,
                     conv_ln_g=conv_ln_g, conv_ln_b=conv_ln_b, conv_pw2=conv_pw2, conv_pw2_b=conv_pw2_b),
                dict(attn_w_qkv=attn_w_qkv, attn_w_o=attn_w_o, attn_sink=attn_sink),
                dict(lru_w_in=lru_w_in, lru_conv_w=lru_conv_w, lru_conv_b=lru_conv_b, lru_wa=lru_wa, lru_ba=lru_ba,
                     lru_wx=lru_wx, lru_bx=lru_bx, lru_lam=lru_lam, lru_w_out=lru_w_out)][i % N_MIXERS], slot

    def params_of(i):
        group, slot = layer_params(i)
        return {k: v[slot] for k, v in group.items()}

    p = params_of(0)
    ip = _in_proj(0, p, b, n, d, ctx_len)
    a_out = _in_proj_standalone(ip, xs, mods[0], norm_mix[0])
    if not isinstance(a_out, (list, tuple)):
        a_out = [a_out]
    for i in range(depth):
        kind = i % N_MIXERS
        route_args = (norm_ffn[i].reshape(1, d), wr, rb)
        if kind == 0:
            res = _conv_core(xs, mods[i], a_out[0], p, route_args)
        elif kind == 1:
            res = _attn_core(xs, mods[i], a_out, p, route_args, ctx_len)
        else:
            res = _lru_core(xs, mods[i], a_out, p, route_args)
        x_new, routed = res[0], res[1:]
        if i == depth - 1:
            return _moe(x_new, mods[i], routed, i, moe_w1, moe_w3, moe_w2, ctx_len, norm_f=norm_f)
        p = params_of(i + 1)
        ip = _in_proj((i + 1) % N_MIXERS, p, b, n, d, ctx_len)
        xs, a_out = _moe(x_new, mods[i], routed, i, moe_w1, moe_w3, moe_w2, ctx_len,
                         next_ip=ip, next_mods=mods[i + 1], next_g=norm_mix[i + 1])
```

```python
import functools

import jax
import jax.numpy as jnp
from jax import lax
from jax.experimental import pallas as pl
from jax.experimental.pallas import tpu as pltpu

F32 = jnp.float32
BF16 = jnp.bfloat16
I32 = jnp.int32

EPS = 1e-6
N_MOD = 6
N_MIXERS = 3
GRID_W = 64
CONV_WIDTH = 31
CONV_HALF = (CONV_WIDTH - 1) // 2
HEAD_DIM = 64
N_KV_HEADS = 4
GQA_GROUP = 4
WINDOW = 128
ROPE_BASE = 10000.0
ROPE_PAIRS = HEAD_DIM // 4
LRU_BLOCKS = 8
LRU_CONV_W = 4
LRU_C = 8.0
N_EXPERTS = 16
N_GROUPS = 4
EXPERTS_PER_GROUP = 4
TOP_K = 2

LANES = 128
SUBLANES = 8
TR = 256
TQ = 128
HALO = 16
MOE_BLK = 512
COMBINE_RING = 3
EXPERT_RING = 3
ROUTER_ROWS = SUBLANES * EXPERTS_PER_GROUP
NEG = -1e30
VMEM_LIMIT = 56 * 1024 * 1024


def _params(sem, vmem=VMEM_LIMIT):
    return pltpu.CompilerParams(dimension_semantics=sem, vmem_limit_bytes=vmem)


def _norm_mod(x, g, shift, scale):
    y = x * lax.rsqrt(jnp.mean(x * x, axis=-1, keepdims=True) + EPS)
    return (y * g) * (1.0 + scale) + shift


def _seg_mod_spec(d, tiles_per_seg=1, tile_off=0):
    return pl.BlockSpec((1, 1, N_MOD, d),
                        lambda b, t, *_: (b, jnp.minimum((t + tile_off) // tiles_per_seg, 1), 0, 0))


def _row_spec(tr, d, tile_off=0):
    return pl.BlockSpec((1, tr, d), lambda b, t, *_: (b, t + tile_off, 0))


def _full_spec(shape):
    nd = len(shape)
    return pl.BlockSpec(shape, lambda b, t, *_: (0,) * nd)


def _mod_kernel(c_ref, w_ref, b_ref, o_ref):
    c = c_ref[...]
    sc = c * jax.nn.sigmoid(c)
    o_ref[0] = jnp.dot(sc.astype(BF16), w_ref[0].astype(BF16), preferred_element_type=F32) + b_ref[0]


def _modulation(c, c_ctx, w_mod, b_mod):
    depth, d, nout = w_mod.shape
    b = c.shape[0]
    assert b + 1 <= SUBLANES
    rows = jnp.zeros((SUBLANES, d), F32).at[:b].set(c).at[b].set(c_ctx)
    tn = 1536
    raw = pl.pallas_call(
        _mod_kernel,
        grid=(depth, nout // tn),
        in_specs=[pl.BlockSpec((SUBLANES, d), lambda i, j: (0, 0)),
                  pl.BlockSpec((1, d, tn), lambda i, j: (i, 0, j)),
                  pl.BlockSpec((1, 1, tn), lambda i, j: (i, 0, j))],
        out_specs=pl.BlockSpec((1, SUBLANES, tn), lambda i, j: (i, 0, j)),
        out_shape=jax.ShapeDtypeStruct((depth, SUBLANES, nout), F32),
        compiler_params=_params(("arbitrary", "arbitrary")),
        name="modulation",
    )(rows, w_mod, b_mod.reshape(depth, 1, nout))
    raw = raw.reshape(depth, SUBLANES, N_MOD, d)
    lat = raw[:, :b]
    ctx = jnp.broadcast_to(raw[:, b][:, None], lat.shape)
    return jnp.stack([ctx, lat], axis=2)


def _store_token_tiles(ref, mat):
    rows = mat.shape[0]
    for s in range(SUBLANES):
        ref[pl.ds(s, rows, stride=SUBLANES), :] = mat[:, s * LANES:(s + 1) * LANES]


def _load_token_tiles(ref, rows):
    return jnp.concatenate([ref[pl.ds(s, rows, stride=SUBLANES), :] for s in range(SUBLANES)], axis=1)


def _tile_rows(ref, first_token, n_tokens):
    first_row = first_token * SUBLANES
    if not isinstance(first_row, int):
        first_row = pl.multiple_of(first_row, SUBLANES)
    return ref.at[pl.ds(first_row, n_tokens * SUBLANES)]


def _route_tile(x, m, g_ref, wr_ref, rb_ref, carry, first, h_ref, e0_ref, e1_ref, p0_ref, p1_ref, gt_ref, cnt_ref):
    tr = x.shape[0]

    @pl.when(first)
    def _():
        carry[...] = jnp.zeros_like(carry)

    h = _norm_mod(x, g_ref[...], m[3:4], m[4:5])
    _store_token_tiles(h_ref, h)
    logit = lax.dot_general(wr_ref[...], h.astype(BF16), (((1,), (1,)), ((), ())),
                            preferred_element_type=F32)
    s = jax.nn.sigmoid(logit)
    sel = s + rb_ref[...]
    sj = [s[SUBLANES * j:SUBLANES * (j + 1)] for j in range(EXPERTS_PER_GROUP)]
    cj = [sel[SUBLANES * j:SUBLANES * (j + 1)] for j in range(EXPERTS_PER_GROUP)]

    hi1, lo1 = jnp.maximum(cj[0], cj[1]), jnp.minimum(cj[0], cj[1])
    hi2, lo2 = jnp.maximum(cj[2], cj[3]), jnp.minimum(cj[2], cj[3])
    top1 = jnp.maximum(hi1, hi2)
    top2 = jnp.maximum(jnp.minimum(hi1, hi2), jnp.maximum(lo1, lo2))
    row = lax.broadcasted_iota(I32, (SUBLANES, tr), 0)
    gscore = jnp.where(row < N_GROUPS, top1 + top2, -jnp.inf)
    gmax = jnp.max(gscore, axis=0, keepdims=True)
    grp = jnp.min(jnp.where(gscore == gmax, row, SUBLANES), axis=0, keepdims=True)
    pick = row == grp
    c = [jnp.sum(jnp.where(pick, v, 0.0), axis=0, keepdims=True) for v in cj]
    w = [jnp.sum(jnp.where(pick, v, 0.0), axis=0, keepdims=True) for v in sj]

    def argtop(vals):
        best, bi, bw = vals[0], jnp.zeros_like(grp), w[0]
        for j in range(1, EXPERTS_PER_GROUP):
            better = vals[j] > best
            best = jnp.where(better, vals[j], best)
            bi = jnp.where(better, j, bi)
            bw = jnp.where(better, w[j], bw)
        return bi, bw

    i0, w0 = argtop(c)
    i1, w1 = argtop([jnp.where(i0 == j, -jnp.inf, c[j]) for j in range(EXPERTS_PER_GROUP)])
    e0 = grp * EXPERTS_PER_GROUP + i0
    e1 = grp * EXPERTS_PER_GROUP + i1
    wsum = w0 + w1
    g0 = w0 / wsum
    g1 = w1 / wsum

    erow = lax.broadcasted_iota(I32, (N_EXPERTS, tr), 0)
    member = jnp.logical_or(erow == e0, erow == e1)
    upper = (lax.broadcasted_iota(I32, (tr, tr), 0) < lax.broadcasted_iota(I32, (tr, tr), 1))
    before = jnp.dot(member.astype(BF16), upper.astype(BF16), preferred_element_type=F32) + carry[...]
    p0 = jnp.sum(jnp.where(erow == e0, before, 0.0), axis=0, keepdims=True)
    p1 = jnp.sum(jnp.where(erow == e1, before, 0.0), axis=0, keepdims=True)
    total = carry[...] + jnp.sum(member.astype(F32), axis=1, keepdims=True)
    carry[...] = total
    cnt_ref[...] = total.astype(I32)

    e0_ref[0] = e0
    e1_ref[0] = e1
    p0_ref[0] = p0.astype(I32)
    p1_ref[0] = p1.astype(I32)
    grow = lax.broadcasted_iota(I32, (LANES, tr), 0)
    gmat = jnp.where(grow == 0, g0, jnp.where(grow == 1, g1, 0.0))
    gt_ref[0] = gmat.T


def _route_io(b, n, d, tile, tile_of):
    nt = n // tile

    def flat(bb, t):
        return bb * nt + tile_of(t)

    lane_major = pl.BlockSpec((1, 1, tile), lambda bb, t, *_: (flat(bb, t), 0, 0))
    lane_major_shape = jax.ShapeDtypeStruct((b * nt, 1, tile), I32)
    in_specs = [_full_spec((1, d)), _full_spec((ROUTER_ROWS, d)), _full_spec((ROUTER_ROWS, 1))]
    out_specs = [pl.BlockSpec((tile * SUBLANES, LANES), lambda bb, t, *_: (flat(bb, t), 0)),
                 lane_major, lane_major, lane_major, lane_major,
                 pl.BlockSpec((1, tile, LANES), lambda bb, t, *_: (bb, tile_of(t), 0)),
                 pl.BlockSpec((N_EXPERTS, 1), lambda bb, t, *_: (0, 0))]
    out_shape = [jax.ShapeDtypeStruct((b * n * SUBLANES, LANES), F32), lane_major_shape, lane_major_shape,
                 lane_major_shape, lane_major_shape,
                 jax.ShapeDtypeStruct((b, n, LANES), F32), jax.ShapeDtypeStruct((N_EXPERTS, 1), I32)]
    scratch = [pltpu.VMEM((N_EXPERTS, 1), F32)]
    return in_specs, out_specs, out_shape, scratch


def _first_step():
    return jnp.logical_and(pl.program_id(0) == 0, pl.program_id(1) == 0)


def _glu_tail(h, w_ref, b_ref, o_ref):
    d = h.shape[-1]
    u = jnp.dot(h.astype(BF16), w_ref[...], preferred_element_type=F32) + b_ref[...]
    o_ref[0] = u[:, :d] * jax.nn.sigmoid(u[:, d:])


def _qkv_tail(h, w_ref, cos_ref, sin_ref, q_ref, k_ref, v_ref):
    d = h.shape[-1]
    nk = k_ref.shape[-1]
    t = jnp.dot(h.astype(BF16), w_ref[...], preferred_element_type=F32)
    cos = cos_ref[...]
    sin = sin_ref[...]
    lane = lax.broadcasted_iota(I32, cos.shape, 1)
    first_half = (lane % (2 * ROPE_PAIRS)) < ROPE_PAIRS

    def rope(xg):
        partner = jnp.where(first_half, pltpu.roll(xg, LANES - ROPE_PAIRS, 1), pltpu.roll(xg, ROPE_PAIRS, 1))
        return xg * cos + partner * sin

    for j in range(d // LANES):
        cs = slice(j * LANES, (j + 1) * LANES)
        q_ref[0, :, cs] = rope(t[:, cs]).astype(BF16)
    for j in range(nk // LANES):
        cs = slice(j * LANES, (j + 1) * LANES)
        k_ref[0, :, cs] = rope(t[:, d + j * LANES:d + (j + 1) * LANES]).astype(BF16)
    v_ref[0] = t[:, d + nk:].astype(BF16)


def _lru_in_tail(h, w_ref, u_ref, gg_ref):
    d = h.shape[-1]
    t = jnp.dot(h.astype(BF16), w_ref[...], preferred_element_type=F32)
    u_ref[0] = t[:, :d]
    gg_ref[0] = jax.nn.gelu(t[:, d:])


def _rope_tables(n, ctx_len):
    pos = jnp.arange(n - ctx_len, dtype=I32)
    inv = ROPE_BASE ** (-jnp.arange(ROPE_PAIRS, dtype=F32) / ROPE_PAIRS)
    ar = (pos // GRID_W).astype(F32)[:, None] * inv
    ac = (pos % GRID_W).astype(F32)[:, None] * inv
    ang = jnp.concatenate([ar, ar, ac, ac], axis=-1)
    sign = jnp.tile(jnp.concatenate([-jnp.ones((ROPE_PAIRS,), F32), jnp.ones((ROPE_PAIRS,), F32)]), 2)
    cos = jnp.concatenate([jnp.ones((ctx_len, HEAD_DIM), F32), jnp.cos(ang)], axis=0)
    sin = jnp.concatenate([jnp.zeros((ctx_len, HEAD_DIM), F32), jnp.sin(ang) * sign], axis=0)
    return jnp.tile(cos, (1, 2)), jnp.tile(sin, (1, 2))


def _in_proj(kind, p, b, n, d, ctx_len):
    if kind == 0:
        args = (p["conv_pw1"].astype(BF16), p["conv_pw1_b"].reshape(1, 2 * d))
        return dict(tail=_glu_tail, args=args, in_specs=[_full_spec((d, 2 * d)), _full_spec((1, 2 * d))],
                    out_specs=[_row_spec(TR, d)], out_shape=[jax.ShapeDtypeStruct((b, n, d), F32)])
    if kind == 1:
        kvd = N_KV_HEADS * HEAD_DIM
        nk = N_KV_HEADS * LANES
        assert d == 2 * nk
        scale = HEAD_DIM ** -0.5
        w_qkv = p["attn_w_qkv"]

        def dup(w):
            w = w.reshape(d, N_KV_HEADS, HEAD_DIM)
            return jnp.concatenate([w, w], axis=-1).reshape(d, nk)

        w_all = jnp.concatenate([w_qkv[:, :d] * scale, dup(w_qkv[:, d:d + kvd]), dup(w_qkv[:, d + kvd:])],
                                axis=1).astype(BF16)
        cos, sin = _rope_tables(n, ctx_len)
        table = pl.BlockSpec((TR, LANES), lambda bb, t, *_: (t, 0))
        return dict(tail=_qkv_tail, args=(w_all, cos, sin), in_specs=[_full_spec((d, d + 2 * nk)), table, table],
                    out_specs=[_row_spec(TR, d), _row_spec(TR, nk), _row_spec(TR, nk)],
                    out_shape=[jax.ShapeDtypeStruct((b, n, d), BF16), jax.ShapeDtypeStruct((b, n, nk), BF16),
                               jax.ShapeDtypeStruct((b, n, nk), BF16)])
    return dict(tail=_lru_in_tail, args=(p["lru_w_in"].astype(BF16),), in_specs=[_full_spec((d, 2 * d))],
                out_specs=[_row_spec(TR, d), _row_spec(TR, d)],
                out_shape=[jax.ShapeDtypeStruct((b, n, d), F32), jax.ShapeDtypeStruct((b, n, d), F32)])


def _head_body_specs(d, body_off):
    return [pl.BlockSpec((1, TR, d), lambda b, t, *_: (b, 0, 0)),
            pl.BlockSpec((1, TR, d), lambda b, t, *_: (b, jnp.maximum(t - body_off, 0), 0))]


def _head_body_tile(head_ref, body_ref):
    return jnp.where(pl.program_id(1) == 0, head_ref[0], body_ref[0])


def _in_proj_kernel(head_ref, body_ref, mod_ref, g_ref, *refs, tail):
    m = mod_ref[0, 0]
    tail(_norm_mod(_head_body_tile(head_ref, body_ref), g_ref[...], m[0:1], m[1:2]), *refs)


def _in_proj_standalone(ip, ctx, x, mods, g):
    b, s, d = x.shape
    n = ctx.shape[1] + s
    return pl.pallas_call(
        functools.partial(_in_proj_kernel, tail=ip["tail"]), grid=(b, n // TR),
        in_specs=_head_body_specs(d, 1) + [_seg_mod_spec(d), _full_spec((1, d))] + ip["in_specs"],
        out_specs=ip["out_specs"], out_shape=ip["out_shape"],
        compiler_params=_params(("parallel", "parallel")),
        name="in_proj",
    )(ctx, x, mods, g.reshape(1, d), *ip["args"])


def _conv_core_kernel(xh_ref, xb_ref, mod_ref, gp_ref, gc_ref, gn_ref, dw_ref, dwb_ref, lng_ref, lnb_ref, w_ref, b_ref,
                      gffn_ref, wr_ref, rb_ref, o_ref, h_ref, e0_ref, e1_ref, p0_ref, p1_ref, gt_ref, cnt_ref,
                      gbuf, cbuf, carry):
    t = pl.program_id(1)
    nt = pl.num_programs(1)
    first = _first_step()
    tr, d = gc_ref.shape[1], gc_ref.shape[2]
    seg_first = t <= 1
    seg_last = jnp.logical_or(t == 0, t == nt - 1)
    gbuf[0, 0:HALO, :] = jnp.where(seg_first, 0.0, gp_ref[0])
    gbuf[0, HALO:HALO + tr, :] = gc_ref[0]
    gbuf[0, HALO + tr:HALO + tr + HALO, :] = jnp.where(seg_last, 0.0, gn_ref[0])
    span = tr + 2 * HALO - SUBLANES
    for r in range(1, SUBLANES):
        gbuf[r, 0:span, :] = gbuf[0, r:r + span, :]

    rc = 32
    base = HALO - CONV_HALF

    def chunk(i, carry_):
        r0 = pl.multiple_of(i * rc, rc)
        for c in range(d // LANES):
            cs = slice(c * LANES, (c + 1) * LANES)
            acc = jnp.zeros((rc, LANES), F32)
            for k in range(CONV_WIDTH):
                off = base + k
                rows = pl.ds(r0 + (off // SUBLANES) * SUBLANES, rc)
                acc = acc + dw_ref[k:k + 1, cs] * gbuf[off % SUBLANES, rows, cs]
            cbuf[pl.ds(r0, rc), cs] = acc
        return carry_

    lax.fori_loop(0, tr // rc, chunk, 0)

    u = cbuf[...] + dwb_ref[...]
    mu = jnp.mean(u, axis=-1, keepdims=True)
    uc = u - mu
    var = jnp.mean(uc * uc, axis=-1, keepdims=True)
    v = uc * lax.rsqrt(var + EPS) * lng_ref[...] + lnb_ref[...]
    v = v * jax.nn.sigmoid(v)
    y = jnp.dot(v.astype(BF16), w_ref[...], preferred_element_type=F32) + b_ref[...]
    m = mod_ref[0, 0]
    x_new = _head_body_tile(xh_ref, xb_ref) + m[2:3] * y
    o_ref[0] = x_new
    _route_tile(x_new, m, gffn_ref, wr_ref, rb_ref, carry, first,
                h_ref, e0_ref, e1_ref, p0_ref, p1_ref, gt_ref, cnt_ref)


def _conv_core(head, body, body_off, mods, glu, p, route_args):
    b, n, d = glu.shape
    nt = n // TR
    hb = TR // HALO
    nh = n // HALO
    dwp = jnp.zeros((32, d), F32).at[:CONV_WIDTH].set(p["conv_dw"])
    r_in, r_out, r_shape, r_scratch = _route_io(b, n, d, TR, lambda t: t)
    return pl.pallas_call(
        _conv_core_kernel, grid=(b, nt),
        in_specs=_head_body_specs(d, body_off) + [
                  _seg_mod_spec(d),
                  pl.BlockSpec((1, HALO, d), lambda bb, t: (bb, jnp.maximum(t * hb - 1, 0), 0)),
                  _row_spec(TR, d),
                  pl.BlockSpec((1, HALO, d), lambda bb, t: (bb, jnp.minimum((t + 1) * hb, nh - 1), 0)),
                  _full_spec((32, d)), _full_spec((1, d)), _full_spec((1, d)), _full_spec((1, d)),
                  _full_spec((d, d)), _full_spec((1, d))] + r_in,
        out_specs=[_row_spec(TR, d)] + r_out,
        out_shape=[jax.ShapeDtypeStruct((b, n, d), F32)] + r_shape,
        scratch_shapes=[pltpu.VMEM((SUBLANES, TR + 2 * HALO, d), F32), pltpu.VMEM((TR, d), F32)] + r_scratch,
        compiler_params=_params(("arbitrary", "arbitrary")),
        name="conv_core",
    )(head, body, mods, glu, glu, glu, dwp, p["conv_dw_b"].reshape(1, d), p["conv_ln_g"].reshape(1, d),
      p["conv_ln_b"].reshape(1, d), p["conv_pw2"].astype(BF16), p["conv_pw2_b"].reshape(1, d), *route_args)


def _attn_core_kernel(sink_ref, x_ref, mod_ref, q_ref, kc_ref, vc_ref, kp_ref, km_ref, kn_ref,
                      vp_ref, vm_ref, vn_ref, wo_ref, gffn_ref, wr_ref, rb_ref,
                      o_ref, h_ref, e0_ref, e1_ref, p0_ref, p1_ref, gt_ref, cnt_ref, carry, *, ctx_len, n_rows):
    first = _first_step()
    tq = kp_ref.shape[1]
    nctx = kc_ref.shape[1]
    nkeys = nctx + 3 * tq
    lane = lax.broadcasted_iota(I32, (tq, LANES), 1)
    lo = lane < HEAD_DIM
    col = lax.broadcasted_iota(I32, (tq, nkeys), 1)
    row = lax.broadcasted_iota(I32, (tq, nkeys), 0)

    def attend(t, q_rows, k_parts, v_parts):
        rq = t * tq + row
        rk = (t - 1) * tq + (col - nctx)
        win_ok = (rk >= ctx_len) & (rk < n_rows) & (jnp.abs(rq - rk) <= WINDOW) & (t * tq >= ctx_len)
        bias = jnp.where((col < nctx) | win_ok, 0.0, NEG).astype(F32)
        out_cols = []
        for g in range(N_KV_HEADS):
            gs = slice(g * LANES, (g + 1) * LANES)
            kg = jnp.concatenate([kc_ref[0, :, gs]] + [part(gs) for part in k_parts], axis=0)
            vg = jnp.concatenate([vc_ref[0, :, gs]] + [part(gs) for part in v_parts], axis=0)
            qa = q_ref[0, q_rows, (2 * g) * LANES:(2 * g + 1) * LANES]
            qb = q_ref[0, q_rows, (2 * g + 1) * LANES:(2 * g + 2) * LANES]
            zero = jnp.zeros_like(qa)
            q4 = jnp.concatenate([jnp.where(lo, qa, zero), jnp.where(lo, zero, qa),
                                  jnp.where(lo, qb, zero), jnp.where(lo, zero, qb)], axis=0)
            s = lax.dot_general(q4, kg, (((1,), (1,)), ((), ())), preferred_element_type=F32)
            ps, dens = [], []
            for hh in range(GQA_GROUP):
                sk = sink_ref[g * GQA_GROUP + hh]
                sh = s[hh * tq:(hh + 1) * tq] + bias
                mx = jnp.maximum(jnp.max(sh, axis=-1, keepdims=True), sk)
                pr = jnp.exp(sh - mx)
                dens.append(jnp.sum(pr, axis=-1, keepdims=True) + jnp.exp(sk - mx))
                ps.append(pr.astype(BF16))
            pv = jnp.dot(jnp.concatenate(ps, axis=0), vg, preferred_element_type=F32)
            oh = [pv[hh * tq:(hh + 1) * tq] / dens[hh] for hh in range(GQA_GROUP)]
            out_cols.append(jnp.where(lo, oh[0], oh[1]))
            out_cols.append(jnp.where(lo, oh[2], oh[3]))
        return jnp.concatenate(out_cols, axis=1).astype(BF16)

    def rows_of(ref, r0):
        return lambda gs: ref[0, r0:r0 + tq, gs]

    t2 = 2 * pl.program_id(1)
    o = jnp.concatenate([
        attend(t2, slice(0, tq), [rows_of(kp_ref, 0), rows_of(km_ref, 0), rows_of(km_ref, tq)],
               [rows_of(vp_ref, 0), rows_of(vm_ref, 0), rows_of(vm_ref, tq)]),
        attend(t2 + 1, slice(tq, 2 * tq), [rows_of(km_ref, 0), rows_of(km_ref, tq), rows_of(kn_ref, 0)],
               [rows_of(vm_ref, 0), rows_of(vm_ref, tq), rows_of(vn_ref, 0)])], axis=0)
    y = jnp.dot(o, wo_ref[...], preferred_element_type=F32)
    m = mod_ref[0, 0]
    x_new = x_ref[0] + m[2:3] * y
    o_ref[0] = x_new
    _route_tile(x_new, m, gffn_ref, wr_ref, rb_ref, carry, first,
                h_ref, e0_ref, e1_ref, p0_ref, p1_ref, gt_ref, cnt_ref)


def _attn_core(x, mods, qkv, p, route_args, ctx_len):
    b, n, d = x.shape
    q, k, v = qkv
    nk = k.shape[-1]
    assert ctx_len == TR and TR == 2 * TQ
    ntq = n // TQ

    prev_spec = pl.BlockSpec((1, TQ, nk), lambda bb, t, s: (bb, jnp.maximum(2 * t - 1, 0), 0))
    next_spec = pl.BlockSpec((1, TQ, nk), lambda bb, t, s: (bb, jnp.minimum(2 * t + 2, ntq - 1), 0))
    ctx_spec = pl.BlockSpec((1, ctx_len, nk), lambda bb, t, s: (bb, 0, 0))
    r_in, r_out, r_shape, r_scratch = _route_io(b, n, d, TR, lambda t: t)
    grid_spec = pltpu.PrefetchScalarGridSpec(
        num_scalar_prefetch=1, grid=(b, n // TR),
        in_specs=[_row_spec(TR, d), _seg_mod_spec(d), _row_spec(TR, d),
                  ctx_spec, ctx_spec, prev_spec, _row_spec(TR, nk), next_spec, prev_spec, _row_spec(TR, nk), next_spec,
                  _full_spec((d, d))] + r_in,
        out_specs=[_row_spec(TR, d)] + r_out,
        scratch_shapes=r_scratch)
    return pl.pallas_call(
        functools.partial(_attn_core_kernel, ctx_len=ctx_len, n_rows=n),
        grid_spec=grid_spec,
        out_shape=[jax.ShapeDtypeStruct((b, n, d), F32)] + r_shape,
        compiler_params=_params(("arbitrary", "arbitrary")),
        name="attn_core",
    )(p["attn_sink"].astype(F32), x, mods, q, k, v, k, k, k, v, v, v, p["attn_w_o"].astype(BF16), *route_args)


def _lru_gates_scan(back, t, u_ref, cw_ref, cb_ref, wg_ref, ba_ref, bx_ref, lam_ref,
                    ubuf, halo, hcar, abuf, bbuf, hbuf):
    tr, d = u_ref.shape[1], u_ref.shape[2]
    nb = d // LRU_BLOCKS

    @pl.when(t <= 1)
    def _():
        halo[...] = jnp.zeros_like(halo)

    @pl.when(t == 0)
    def _():
        hcar[...] = jnp.zeros_like(hcar)

    u = u_ref[0]
    if not back:
        ubuf[0:SUBLANES, :] = halo[...]
        ubuf[SUBLANES:SUBLANES + tr, :] = u
        halo[...] = u[tr - SUBLANES:, :]
        taps = [ubuf[SUBLANES - (LRU_CONV_W - 1) + k:SUBLANES - (LRU_CONV_W - 1) + k + tr, :]
                for k in range(LRU_CONV_W)]
    else:
        ubuf[0:tr, :] = u
        ubuf[tr:tr + SUBLANES, :] = halo[...]
        halo[...] = u[:SUBLANES, :]
        taps = [ubuf[(LRU_CONV_W - 1) - k:(LRU_CONV_W - 1) - k + tr, :] for k in range(LRU_CONV_W)]
    cc = cb_ref[...] + taps[0] * cw_ref[0:1, :]
    for k in range(1, LRU_CONV_W):
        cc = cc + taps[k] * cw_ref[k:k + 1, :]

    ccb = cc.astype(BF16)
    lam = lam_ref[...]
    neg_c_softplus = -LRU_C * (jnp.maximum(-lam, 0.0) + jnp.log(1.0 + jnp.exp(-jnp.abs(lam))))
    for blk in range(LRU_BLOCKS):
        cs = slice(blk * nb, (blk + 1) * nb)
        z = jnp.dot(ccb[:, cs], wg_ref[blk], preferred_element_type=F32)
        r = jax.nn.sigmoid(z[:, :nb] + ba_ref[:, cs])
        gi = jax.nn.sigmoid(z[:, nb:] + bx_ref[:, cs])
        log_a = neg_c_softplus[:, cs] * r
        a = jnp.exp(log_a)
        abuf[:, cs] = a
        y = 1.0 - a * a
        bbuf[:, cs] = jnp.where(y > 0.0, y * lax.rsqrt(y), 0.0) * (gi * cc[:, cs])

    row = lax.broadcasted_iota(I32, (SUBLANES, d), 0)
    nchunk = tr // SUBLANES

    def chunk(i, h):
        ci = (nchunk - 1 - i) if back else i
        r0 = pl.multiple_of(ci * SUBLANES, SUBLANES)
        a = abuf[pl.ds(r0, SUBLANES), :]
        bb = bbuf[pl.ds(r0, SUBLANES), :]
        for k in (1, 2, 4):
            if back:
                sh, ok = SUBLANES - k, row < SUBLANES - k
            else:
                sh, ok = k, row >= k
            a_s = pltpu.roll(a, sh, 0)
            b_s = pltpu.roll(bb, sh, 0)
            bb = jnp.where(ok, a * b_s + bb, bb)
            a = jnp.where(ok, a * a_s, a)
        hh = a * h + bb
        hbuf[pl.ds(r0, SUBLANES), :] = hh
        return hh[0:1, :] if back else hh[SUBLANES - 1:SUBLANES, :]

    hcar[0:1, :] = lax.fori_loop(0, nchunk, chunk, hcar[0:1, :])


def _lru_fwd_kernel(u_ref, cw_ref, cb_ref, wg_ref, ba_ref, bx_ref, lam_ref, hf_ref,
                    ubuf, halo, hcar, abuf, bbuf, hbuf):
    _lru_gates_scan(False, pl.program_id(1), u_ref, cw_ref, cb_ref, wg_ref, ba_ref, bx_ref, lam_ref,
                    ubuf, halo, hcar, abuf, bbuf, hbuf)
    hf_ref[0] = hbuf[...]


def _lru_bwd_kernel(u_ref, cw_ref, cb_ref, wg_ref, ba_ref, bx_ref, lam_ref, hf_ref, gg_ref, x_ref, mod_ref,
                    wo_ref, gffn_ref, wr_ref, rb_ref, o_ref, h_ref, e0_ref, e1_ref, p0_ref, p1_ref, gt_ref, cnt_ref,
                    ubuf, halo, hcar, abuf, bbuf, hbuf, carry):
    first = _first_step()
    _lru_gates_scan(True, pl.program_id(1), u_ref, cw_ref, cb_ref, wg_ref, ba_ref, bx_ref, lam_ref,
                    ubuf, halo, hcar, abuf, bbuf, hbuf)
    y = (hf_ref[0] + hbuf[...]) * gg_ref[0]
    out = jnp.dot(y.astype(BF16), wo_ref[...], preferred_element_type=F32)
    m = mod_ref[0, 0]
    x_new = x_ref[0] + m[2:3] * out
    o_ref[0] = x_new
    _route_tile(x_new, m, gffn_ref, wr_ref, rb_ref, carry, first,
                h_ref, e0_ref, e1_ref, p0_ref, p1_ref, gt_ref, cnt_ref)


def _lru_core(x, mods, ugg, p, route_args):
    b, n, d = x.shape
    u, gg = ugg
    nt = n // TR
    nb = d // LRU_BLOCKS
    wg = jnp.concatenate([p["lru_wa"], p["lru_wx"]], axis=-1).astype(BF16)
    scratch = [pltpu.VMEM((TR + SUBLANES, d), F32), pltpu.VMEM((SUBLANES, d), F32),
               pltpu.VMEM((SUBLANES, d), F32), pltpu.VMEM((TR, d), F32), pltpu.VMEM((TR, d), F32),
               pltpu.VMEM((TR, d), F32)]

    def gate_specs(order):
        return [pl.BlockSpec((1, TR, d), lambda bb, t: (bb, order(t), 0)),
                _full_spec((LRU_CONV_W, d)), _full_spec((1, d)), _full_spec((LRU_BLOCKS, nb, 2 * nb)),
                _full_spec((1, d)), _full_spec((1, d)), _full_spec((1, d))]

    def gate_args(dd):
        return (u, p["lru_conv_w"][dd], p["lru_conv_b"][dd].reshape(1, d), wg[dd], p["lru_ba"][dd].reshape(1, d),
                p["lru_bx"][dd].reshape(1, d), p["lru_lam"][dd].reshape(1, d))

    hf = pl.pallas_call(
        _lru_fwd_kernel, grid=(b, nt),
        in_specs=gate_specs(lambda t: t),
        out_specs=_row_spec(TR, d),
        out_shape=jax.ShapeDtypeStruct((b, n, d), F32),
        scratch_shapes=scratch,
        compiler_params=_params(("arbitrary", "arbitrary")),
        name="lru_fwd",
    )(*gate_args(0))

    def rev(t):
        return jnp.where(t == 0, 0, nt - t)

    def rev_spec():
        return pl.BlockSpec((1, TR, d), lambda bb, t: (bb, rev(t), 0))

    r_in, r_out, r_shape, r_scratch = _route_io(b, n, d, TR, rev)
    return pl.pallas_call(
        _lru_bwd_kernel, grid=(b, nt),
        in_specs=gate_specs(rev) + [rev_spec(), rev_spec(), rev_spec(), _seg_mod_spec(d), _full_spec((d, d))] + r_in,
        out_specs=[rev_spec()] + r_out,
        out_shape=[jax.ShapeDtypeStruct((b, n, d), F32)] + r_shape,
        scratch_shapes=scratch + r_scratch,
        compiler_params=_params(("arbitrary", "arbitrary")),
        name="lru_bwd",
    )(*gate_args(1), hf, gg, x, mods, p["lru_w_out"].astype(BF16), *route_args)


def _dispatch_kernel(slot_ref, zrow_ref, nu_ref, h_ref, xs_ref, ring, zbuf, sem, zsem, *, n_tok, n_blocks):
    s = pl.program_id(0)
    ns = pl.num_programs(0)
    tr = h_ref.shape[0] // SUBLANES
    blk = zbuf.shape[0] // SUBLANES
    par = s % 2

    @pl.when(s == 0)
    def _():
        zbuf[...] = jnp.zeros_like(zbuf)
        for e in range(N_EXPERTS):
            pltpu.make_async_copy(zbuf, _tile_rows(xs_ref, zrow_ref[e], blk), zsem).start()
        for e in range(N_EXPERTS):
            pltpu.make_async_copy(zbuf, _tile_rows(xs_ref, 0, blk), zsem).wait()

        def zero_blk(i, carry):
            pltpu.make_async_copy(zbuf, _tile_rows(xs_ref, i * blk, blk), zsem).start()
            return carry

        def zero_blk_wait(i, carry):
            pltpu.make_async_copy(zbuf, _tile_rows(xs_ref, 0, blk), zsem).wait()
            return carry

        lax.fori_loop(nu_ref[0], n_blocks, zero_blk, 0)
        lax.fori_loop(nu_ref[0], n_blocks, zero_blk_wait, 0)

    def wait_ring(p):
        for _ in range(TOP_K):
            pltpu.make_async_copy(ring.at[p], _tile_rows(xs_ref, 0, tr), sem.at[p]).wait()

    @pl.when(s >= 2)
    def _():
        wait_ring(par)

    ring[par] = h_ref[...]
    base = s * tr

    def issue(i, carry):
        for k in range(TOP_K):
            pltpu.make_async_copy(_tile_rows(ring.at[par], i, 1),
                                  _tile_rows(xs_ref, slot_ref[k * n_tok + base + i], 1),
                                  sem.at[par]).start(priority=k)
        return carry

    lax.fori_loop(0, tr, issue, 0, unroll=8)

    @pl.when(s == ns - 1)
    def _():
        wait_ring(par)

        @pl.when(ns >= 2)
        def _():
            wait_ring(1 - par)


def _expert_kernel(be_ref, nu_ref, xs_ref, w1_ref, w3_ref, w2_ref, ys_ref, xring, sem, w1b, w3b, w2b):
    i = pl.program_id(0)
    nb = pl.num_programs(0)
    blk = xring.shape[1] // SUBLANES
    used = i < nu_ref[0]
    new_expert = jnp.logical_or(i == 0, be_ref[i] != be_ref[jnp.maximum(i - 1, 0)])
    ahead = EXPERT_RING - 1

    def fetch(block, ring):
        return pltpu.make_async_copy(_tile_rows(xs_ref, block * blk, blk), xring.at[ring], sem.at[ring])

    @pl.when(i == 0)
    def _():
        for first in range(ahead):
            fetch(jnp.minimum(first, nb - 1), first).start()

    @pl.when(jnp.logical_and(used, new_expert))
    def _():
        w1b[...] = w1_ref[0, 0].astype(BF16)
        w3b[...] = w3_ref[0, 0].astype(BF16)
        w2b[...] = w2_ref[0, 0].astype(BF16)

    @pl.when(used)
    def _():
        ring = i % EXPERT_RING
        fetch(0, ring).wait()
        fetch(jnp.minimum(i + ahead, nb - 1), (i + ahead) % EXPERT_RING).start()
        xb = _load_token_tiles(xring.at[ring], blk).astype(BF16)
        a = jnp.dot(xb, w1b[...], preferred_element_type=F32)
        b = jnp.dot(xb, w3b[...], preferred_element_type=F32)
        hid = (a * jax.nn.sigmoid(a)) * b
        _store_token_tiles(ys_ref, jnp.dot(hid.astype(BF16), w2b[...], preferred_element_type=F32))

        @pl.when(i == nu_ref[0] - 1)
        def _():
            for later in range(1, EXPERT_RING):
                fetch(0, (i + later) % EXPERT_RING).wait()

    @pl.when(jnp.logical_not(used))
    def _():
        ys_ref[...] = jnp.zeros_like(ys_ref)


def _combine_kernel(slot_ref, x_ref, mod_ref, gt_ref, ys_ref, *refs, n_tok, rows_per_batch, tile_off, tail, n_tail_in):
    if tail is None:
        nf_ref, o_ref, ybuf, sem = refs
    else:
        modn_ref, gn_ref = refs[0], refs[1]
        tail_in = refs[2:2 + n_tail_in]
        o_ref = refs[2 + n_tail_in]
        tail_out = refs[3 + n_tail_in:-2]
        ybuf, sem = refs[-2], refs[-1]
    bb, t = pl.program_id(0), pl.program_id(1)
    nt = pl.num_programs(1)
    tr = x_ref.shape[1]
    step = bb * nt + t
    nsteps = pl.num_programs(0) * nt

    def tok_base(s):
        return (s // nt) * rows_per_batch + (s % nt + tile_off) * tr

    def copy_row(base, i, ring, k):
        pltpu.make_async_copy(_tile_rows(ys_ref, slot_ref[k * n_tok + base + i], 1),
                              _tile_rows(ybuf.at[ring, k], i, 1), sem.at[ring]).start(priority=k)

    def wait_ring(ring):
        for k in range(TOP_K):
            pltpu.make_async_copy(_tile_rows(ys_ref, 0, tr), ybuf.at[ring, k], sem.at[ring]).wait()

    @pl.when(step == 0)
    def _():
        for first in range(COMBINE_RING - 1):
            base = tok_base(jnp.minimum(first, nsteps - 1))

            def one(i, carry, base=base, first=first):
                for k in range(TOP_K):
                    copy_row(base, i, first, k)
                return carry

            lax.fori_loop(0, tr, one, 0, unroll=8)

    ring = step % COMBINE_RING
    wait_ring(ring)
    gt = gt_ref[0]
    y = (gt[:, 0:1] * _load_token_tiles(ybuf.at[ring, 0], tr)
         + gt[:, 1:2] * _load_token_tiles(ybuf.at[ring, 1], tr))

    ahead = COMBINE_RING - 1
    fetch_base = tok_base(jnp.minimum(step + ahead, nsteps - 1))
    fetch_ring = (step + ahead) % COMBINE_RING
    for i in range(tr):
        for k in range(TOP_K):
            copy_row(fetch_base, i, fetch_ring, k)

    out = x_ref[0] + mod_ref[0, 0][5:6] * y
    if tail is None:
        o_ref[0] = out * lax.rsqrt(jnp.mean(out * out, axis=-1, keepdims=True) + EPS) * nf_ref[...]
    else:
        o_ref[0] = out
        mn = modn_ref[0, 0]
        tail(_norm_mod(out, gn_ref[...], mn[0:1], mn[1:2]), *tail_in, *tail_out)

    @pl.when(step == nsteps - 1)
    def _():
        for later in range(1, COMBINE_RING):
            wait_ring((step + later) % COMBINE_RING)


def _moe(x, mods, routed, layer, w1, w3, w2, ctx_len, norm_f=None, next_ip=None, next_mods=None, next_g=None):
    b, n, d = x.shape
    nt = n // TR
    n_tok = b * n
    de = w1.shape[-1]
    h, e0, e1, p0, p1, gtab, counts = routed

    counts = counts[:, 0]
    padded = (counts + MOE_BLK - 1) // MOE_BLK * MOE_BLK
    pend = jnp.cumsum(padded)
    pstart = (pend - padded).astype(I32)
    n_slots = n_tok * TOP_K + N_EXPERTS * MOE_BLK
    n_blocks = n_slots // MOE_BLK
    n_used = (pend[-1] // MOE_BLK).astype(I32).reshape(1)
    blk_start = jnp.arange(n_blocks, dtype=I32) * MOE_BLK
    blk_e = jnp.minimum(jnp.sum((pend[None, :] <= blk_start[:, None]).astype(I32), axis=1), N_EXPERTS - 1)

    def slot_of(e, p):
        e, p = e.reshape(n_tok), p.reshape(n_tok)
        start = jnp.zeros_like(e)
        for j in range(N_EXPERTS):
            start = jnp.where(e == j, pstart[j], start)
        return start + p

    slots = jnp.concatenate([slot_of(e0, p0), slot_of(e1, p1)])
    zrow = (pstart + counts).astype(I32)

    xs = pl.pallas_call(
        functools.partial(_dispatch_kernel, n_tok=n_tok, n_blocks=n_blocks),
        grid_spec=pltpu.PrefetchScalarGridSpec(
            num_scalar_prefetch=3, grid=(n_tok // TR,),
            in_specs=[pl.BlockSpec((TR * SUBLANES, LANES), lambda s, *_: (s, 0))],
            out_specs=pl.BlockSpec(memory_space=pl.ANY),
            scratch_shapes=[pltpu.VMEM((2, TR * SUBLANES, LANES), F32), pltpu.VMEM((MOE_BLK * SUBLANES, LANES), F32),
                            pltpu.SemaphoreType.DMA((2,)), pltpu.SemaphoreType.DMA]),
        out_shape=jax.ShapeDtypeStruct((n_slots * SUBLANES, LANES), F32),
        compiler_params=_params(("arbitrary",)),
        name="moe_dispatch",
    )(slots, zrow, n_used, h)

    def wspec(shape):
        return pl.BlockSpec((1, 1) + shape, lambda i, be, nu: (layer, be[i], 0, 0))

    ys = pl.pallas_call(
        _expert_kernel,
        grid_spec=pltpu.PrefetchScalarGridSpec(
            num_scalar_prefetch=2, grid=(n_blocks,),
            in_specs=[pl.BlockSpec(memory_space=pl.ANY), wspec((d, de)), wspec((d, de)), wspec((de, d))],
            out_specs=pl.BlockSpec((MOE_BLK * SUBLANES, LANES), lambda i, be, nu: (i, 0)),
            scratch_shapes=[pltpu.VMEM((EXPERT_RING, MOE_BLK * SUBLANES, LANES), F32),
                            pltpu.SemaphoreType.DMA((EXPERT_RING,)),
                            pltpu.VMEM((d, de), BF16), pltpu.VMEM((d, de), BF16), pltpu.VMEM((de, d), BF16)]),
        out_shape=jax.ShapeDtypeStruct((n_slots * SUBLANES, LANES), F32),
        compiler_params=_params(("arbitrary",)),
        name="moe_experts",
    )(blk_e, n_used, xs, w1, w3, w2)

    final = next_ip is None
    tile_off = ctx_len // TR if final else 0
    nt_out = nt - tile_off
    common_specs = [_row_spec(TR, d, tile_off), _seg_mod_spec(d, tile_off=tile_off), _row_spec(TR, LANES, tile_off),
                    pl.BlockSpec(memory_space=pl.ANY)]
    scratch = [pltpu.VMEM((COMBINE_RING, TOP_K, TR * SUBLANES, LANES), F32),
               pltpu.SemaphoreType.DMA((COMBINE_RING,))]
    stream_spec = _row_spec(TR, d)
    stream_shape = jax.ShapeDtypeStruct((b, nt_out * TR, d), F32)
    if final:
        return pl.pallas_call(
            functools.partial(_combine_kernel, n_tok=n_tok, rows_per_batch=n, tile_off=tile_off, tail=None,
                              n_tail_in=0),
            grid_spec=pltpu.PrefetchScalarGridSpec(
                num_scalar_prefetch=1, grid=(b, nt_out),
                in_specs=common_specs + [_full_spec((1, d))],
                out_specs=stream_spec, scratch_shapes=scratch),
            out_shape=stream_shape,
            compiler_params=_params(("arbitrary", "arbitrary")),
            name="moe_combine_final",
        )(slots, x, mods, gtab, ys, norm_f.reshape(1, d))
    outs = pl.pallas_call(
        functools.partial(_combine_kernel, n_tok=n_tok, rows_per_batch=n, tile_off=tile_off, tail=next_ip["tail"],
                          n_tail_in=len(next_ip["args"])),
        grid_spec=pltpu.PrefetchScalarGridSpec(
            num_scalar_prefetch=1, grid=(b, nt_out),
            in_specs=common_specs + [_seg_mod_spec(d), _full_spec((1, d))] + next_ip["in_specs"],
            out_specs=[stream_spec] + next_ip["out_specs"], scratch_shapes=scratch),
        out_shape=[stream_shape] + next_ip["out_shape"],
        compiler_params=_params(("arbitrary", "arbitrary")),
        name="moe_combine_in_proj",
    )(slots, x, mods, gtab, ys, next_mods, next_g.reshape(1, d), *next_ip["args"])
    return outs[0], outs[1:]


def kernel(x, c, ctx, c_ctx, w_mod, b_mod, norm_mix, norm_ffn, norm_f, conv_pw1, conv_pw1_b, conv_dw, conv_dw_b, conv_ln_g, conv_ln_b, conv_pw2, conv_pw2_b, attn_w_qkv, attn_w_o, attn_sink, lru_w_in, lru_conv_w, lru_conv_b, lru_wa, lru_ba, lru_wx, lru_bx, lru_lam, lru_w_out, moe_w_router, moe_router_bias, moe_w1, moe_w3, moe_w2):
    b, s, d = x.shape
    ctx_len = ctx.shape[1]
    depth = w_mod.shape[0]
    n = ctx_len + s
    assert ctx_len == TR and s % TR == 0 and d == SUBLANES * LANES
    assert moe_w_router.shape[1] == N_EXPERTS

    mods = _modulation(c, c_ctx, w_mod, b_mod)
    xs = None

    wr = jnp.zeros((EXPERTS_PER_GROUP, SUBLANES, d), F32).at[:, :N_GROUPS].set(
        moe_w_router.T.reshape(N_GROUPS, EXPERTS_PER_GROUP, d).swapaxes(0, 1))
    wr = wr.reshape(ROUTER_ROWS, d).astype(BF16)
    rb = jnp.zeros((EXPERTS_PER_GROUP, SUBLANES), F32).at[:, :N_GROUPS].set(
        moe_router_bias.astype(F32).reshape(N_GROUPS, EXPERTS_PER_GROUP).T).reshape(-1, 1)

    def layer_params(i):
        slot = i // N_MIXERS
        return [dict(conv_pw1=conv_pw1, conv_pw1_b=conv_pw1_b, conv_dw=conv_dw, conv_dw_b=conv_dw_b,
                     conv_ln_g=conv_ln_g, conv_ln_b=conv_ln_b, conv_pw2=conv_pw2, conv_pw2_b=conv_pw2_b),
                dict(attn_w_qkv=attn_w_qkv, attn_w_o=attn_w_o, attn_sink=attn_sink),
                dict(lru_w_in=lru_w_in, lru_conv_w=lru_conv_w, lru_conv_b=lru_conv_b, lru_wa=lru_wa, lru_ba=lru_ba,
                     lru_wx=lru_wx, lru_bx=lru_bx, lru_lam=lru_lam, lru_w_out=lru_w_out)][i % N_MIXERS], slot

    def params_of(i):
        group, slot = layer_params(i)
        return {k: v[slot] for k, v in group.items()}

    p = params_of(0)
    ip = _in_proj(0, p, b, n, d, ctx_len)
    a_out = _in_proj_standalone(ip, ctx, x, mods[0], norm_mix[0])
    if not isinstance(a_out, (list, tuple)):
        a_out = [a_out]
    for i in range(depth):
        kind = i % N_MIXERS
        route_args = (norm_ffn[i].reshape(1, d), wr, rb)
        if kind == 0:
            head, body, body_off = (ctx, x, 1) if i == 0 else (xs, xs, 0)
            res = _conv_core(head, body, body_off, mods[i], a_out[0], p, route_args)
        elif kind == 1:
            res = _attn_core(xs, mods[i], a_out, p, route_args, ctx_len)
        else:
            res = _lru_core(xs, mods[i], a_out, p, route_args)
        x_new, routed = res[0], res[1:]
        if i == depth - 1:
            return _moe(x_new, mods[i], routed, i, moe_w1, moe_w3, moe_w2, ctx_len, norm_f=norm_f)
        p = params_of(i + 1)
        ip = _in_proj((i + 1) % N_MIXERS, p, b, n, d, ctx_len)
        xs, a_out = _moe(x_new, mods[i], routed, i, moe_w1, moe_w3, moe_w2, ctx_len,
                         next_ip=ip, next_mods=mods[i + 1], next_g=norm_mix[i + 1])
```

```python
import functools

import jax
import jax.numpy as jnp
from jax import lax
from jax.experimental import pallas as pl
from jax.experimental.pallas import tpu as pltpu

F32 = jnp.float32
BF16 = jnp.bfloat16
I32 = jnp.int32

EPS = 1e-6
N_MOD = 6
N_MIXERS = 3
GRID_W = 64
CONV_WIDTH = 31
CONV_HALF = (CONV_WIDTH - 1) // 2
HEAD_DIM = 64
N_KV_HEADS = 4
GQA_GROUP = 4
WINDOW = 128
ROPE_BASE = 10000.0
ROPE_PAIRS = HEAD_DIM // 4
LRU_BLOCKS = 8
LRU_CONV_W = 4
LRU_C = 8.0
N_EXPERTS = 16
N_GROUPS = 4
EXPERTS_PER_GROUP = 4
TOP_K = 2

LANES = 128
SUBLANES = 8
TR = 256
TQ = 128
HALO = 16
MOE_BLK = 512
COMBINE_RING = 3
EXPERT_RING = 3
ROUTER_ROWS = SUBLANES * EXPERTS_PER_GROUP
NEG = -1e30
VMEM_LIMIT = 56 * 1024 * 1024


def _params(sem, vmem=VMEM_LIMIT):
    return pltpu.CompilerParams(dimension_semantics=sem, vmem_limit_bytes=vmem)


def _norm_mod(x, g, shift, scale):
    y = x * lax.rsqrt(jnp.mean(x * x, axis=-1, keepdims=True) + EPS)
    return (y * g) * (1.0 + scale) + shift


def _seg_mod_spec(d, tiles_per_seg=1, tile_off=0):
    return pl.BlockSpec((1, 1, N_MOD, d),
                        lambda b, t, *_: (b, jnp.minimum((t + tile_off) // tiles_per_seg, 1), 0, 0))


def _row_spec(tr, d, tile_off=0):
    return pl.BlockSpec((1, tr, d), lambda b, t, *_: (b, t + tile_off, 0))


def _full_spec(shape):
    nd = len(shape)
    return pl.BlockSpec(shape, lambda b, t, *_: (0,) * nd)


def _mod_kernel(c_ref, w_ref, b_ref, o_ref):
    c = c_ref[...]
    sc = c * jax.nn.sigmoid(c)
    o_ref[0] = jnp.dot(sc.astype(BF16), w_ref[0].astype(BF16), preferred_element_type=F32) + b_ref[0]


def _modulation(c, c_ctx, w_mod, b_mod):
    depth, d, nout = w_mod.shape
    b = c.shape[0]
    assert b + 1 <= SUBLANES
    rows = jnp.zeros((SUBLANES, d), F32).at[:b].set(c).at[b].set(c_ctx)
    tn = 1536
    raw = pl.pallas_call(
        _mod_kernel,
        grid=(depth, nout // tn),
        in_specs=[pl.BlockSpec((SUBLANES, d), lambda i, j: (0, 0)),
                  pl.BlockSpec((1, d, tn), lambda i, j: (i, 0, j)),
                  pl.BlockSpec((1, 1, tn), lambda i, j: (i, 0, j))],
        out_specs=pl.BlockSpec((1, SUBLANES, tn), lambda i, j: (i, 0, j)),
        out_shape=jax.ShapeDtypeStruct((depth, SUBLANES, nout), F32),
        compiler_params=_params(("arbitrary", "arbitrary")),
        name="modulation",
    )(rows, w_mod, b_mod.reshape(depth, 1, nout))
    raw = raw.reshape(depth, SUBLANES, N_MOD, d)
    lat = raw[:, :b]
    ctx = jnp.broadcast_to(raw[:, b][:, None], lat.shape)
    return jnp.stack([ctx, lat], axis=2)


def _store_token_tiles(ref, mat):
    rows = mat.shape[0]
    for s in range(SUBLANES):
        ref[pl.ds(s, rows, stride=SUBLANES), :] = mat[:, s * LANES:(s + 1) * LANES]


def _load_token_tiles(ref, rows):
    return jnp.concatenate([ref[pl.ds(s, rows, stride=SUBLANES), :] for s in range(SUBLANES)], axis=1)


def _tile_rows(ref, first_token, n_tokens):
    first_row = first_token * SUBLANES
    if not isinstance(first_row, int):
        first_row = pl.multiple_of(first_row, SUBLANES)
    return ref.at[pl.ds(first_row, n_tokens * SUBLANES)]


def _route_tile(x, m, g_ref, wr_ref, rb_ref, carry, first, h_ref, e0_ref, e1_ref, p0_ref, p1_ref, gt_ref, cnt_ref):
    tr = x.shape[0]

    @pl.when(first)
    def _():
        carry[...] = jnp.zeros_like(carry)

    h = _norm_mod(x, g_ref[...], m[3:4], m[4:5])
    _store_token_tiles(h_ref, h)
    logit = lax.dot_general(wr_ref[...], h.astype(BF16), (((1,), (1,)), ((), ())),
                            preferred_element_type=F32)
    s = jax.nn.sigmoid(logit)
    sel = s + rb_ref[...]
    sj = [s[SUBLANES * j:SUBLANES * (j + 1)] for j in range(EXPERTS_PER_GROUP)]
    cj = [sel[SUBLANES * j:SUBLANES * (j + 1)] for j in range(EXPERTS_PER_GROUP)]

    hi1, lo1 = jnp.maximum(cj[0], cj[1]), jnp.minimum(cj[0], cj[1])
    hi2, lo2 = jnp.maximum(cj[2], cj[3]), jnp.minimum(cj[2], cj[3])
    top1 = jnp.maximum(hi1, hi2)
    top2 = jnp.maximum(jnp.minimum(hi1, hi2), jnp.maximum(lo1, lo2))
    row = lax.broadcasted_iota(I32, (SUBLANES, tr), 0)
    gscore = jnp.where(row < N_GROUPS, top1 + top2, -jnp.inf)
    gmax = jnp.max(gscore, axis=0, keepdims=True)
    grp = jnp.min(jnp.where(gscore == gmax, row, SUBLANES), axis=0, keepdims=True)
    pick = row == grp
    c = [jnp.sum(jnp.where(pick, v, 0.0), axis=0, keepdims=True) for v in cj]
    w = [jnp.sum(jnp.where(pick, v, 0.0), axis=0, keepdims=True) for v in sj]

    def argtop(vals):
        best, bi, bw = vals[0], jnp.zeros_like(grp), w[0]
        for j in range(1, EXPERTS_PER_GROUP):
            better = vals[j] > best
            best = jnp.where(better, vals[j], best)
            bi = jnp.where(better, j, bi)
            bw = jnp.where(better, w[j], bw)
        return bi, bw

    i0, w0 = argtop(c)
    i1, w1 = argtop([jnp.where(i0 == j, -jnp.inf, c[j]) for j in range(EXPERTS_PER_GROUP)])
    e0 = grp * EXPERTS_PER_GROUP + i0
    e1 = grp * EXPERTS_PER_GROUP + i1
    wsum = w0 + w1
    g0 = w0 / wsum
    g1 = w1 / wsum

    erow = lax.broadcasted_iota(I32, (N_EXPERTS, tr), 0)
    member = jnp.logical_or(erow == e0, erow == e1)
    upper = (lax.broadcasted_iota(I32, (tr, tr), 0) < lax.broadcasted_iota(I32, (tr, tr), 1))
    before = jnp.dot(member.astype(BF16), upper.astype(BF16), preferred_element_type=F32) + carry[...]
    p0 = jnp.sum(jnp.where(erow == e0, before, 0.0), axis=0, keepdims=True)
    p1 = jnp.sum(jnp.where(erow == e1, before, 0.0), axis=0, keepdims=True)
    total = carry[...] + jnp.sum(member.astype(F32), axis=1, keepdims=True)
    carry[...] = total
    cnt_ref[...] = total.astype(I32)

    e0_ref[0] = e0
    e1_ref[0] = e1
    p0_ref[0] = p0.astype(I32)
    p1_ref[0] = p1.astype(I32)
    grow = lax.broadcasted_iota(I32, (LANES, tr), 0)
    gmat = jnp.where(grow == 0, g0, jnp.where(grow == 1, g1, 0.0))
    gt_ref[0] = gmat.T


def _route_io(b, n, d, tile, tile_of):
    nt = n // tile

    def flat(bb, t):
        return bb * nt + tile_of(t)

    lane_major = pl.BlockSpec((1, 1, tile), lambda bb, t, *_: (flat(bb, t), 0, 0))
    lane_major_shape = jax.ShapeDtypeStruct((b * nt, 1, tile), I32)
    in_specs = [_full_spec((1, d)), _full_spec((ROUTER_ROWS, d)), _full_spec((ROUTER_ROWS, 1))]
    out_specs = [pl.BlockSpec((tile * SUBLANES, LANES), lambda bb, t, *_: (flat(bb, t), 0)),
                 lane_major, lane_major, lane_major, lane_major,
                 pl.BlockSpec((1, tile, LANES), lambda bb, t, *_: (bb, tile_of(t), 0)),
                 pl.BlockSpec((N_EXPERTS, 1), lambda bb, t, *_: (0, 0))]
    out_shape = [jax.ShapeDtypeStruct((b * n * SUBLANES, LANES), F32), lane_major_shape, lane_major_shape,
                 lane_major_shape, lane_major_shape,
                 jax.ShapeDtypeStruct((b, n, LANES), F32), jax.ShapeDtypeStruct((N_EXPERTS, 1), I32)]
    scratch = [pltpu.VMEM((N_EXPERTS, 1), F32)]
    return in_specs, out_specs, out_shape, scratch


def _first_step():
    return jnp.logical_and(pl.program_id(0) == 0, pl.program_id(1) == 0)


def _glu_tail(h, w_ref, b_ref, o_ref):
    d = h.shape[-1]
    u = jnp.dot(h.astype(BF16), w_ref[...], preferred_element_type=F32) + b_ref[...]
    o_ref[0] = u[:, :d] * jax.nn.sigmoid(u[:, d:])


def _qkv_tail(h, w_ref, cos_ref, sin_ref, q_ref, k_ref, v_ref):
    d = h.shape[-1]
    nk = k_ref.shape[-1]
    t = jnp.dot(h.astype(BF16), w_ref[...], preferred_element_type=F32)
    cos = cos_ref[...]
    sin = sin_ref[...]
    lane = lax.broadcasted_iota(I32, cos.shape, 1)
    first_half = (lane % (2 * ROPE_PAIRS)) < ROPE_PAIRS

    def rope(xg):
        partner = jnp.where(first_half, pltpu.roll(xg, LANES - ROPE_PAIRS, 1), pltpu.roll(xg, ROPE_PAIRS, 1))
        return xg * cos + partner * sin

    for j in range(d // LANES):
        cs = slice(j * LANES, (j + 1) * LANES)
        q_ref[0, :, cs] = rope(t[:, cs]).astype(BF16)
    for j in range(nk // LANES):
        cs = slice(j * LANES, (j + 1) * LANES)
        k_ref[0, :, cs] = rope(t[:, d + j * LANES:d + (j + 1) * LANES]).astype(BF16)
    v_ref[0] = t[:, d + nk:].astype(BF16)


def _lru_in_tail(h, w_ref, u_ref, gg_ref):
    d = h.shape[-1]
    t = jnp.dot(h.astype(BF16), w_ref[...], preferred_element_type=F32)
    u_ref[0] = t[:, :d]
    gg_ref[0] = jax.nn.gelu(t[:, d:])


def _rope_tables(n, ctx_len):
    pos = jnp.arange(n - ctx_len, dtype=I32)
    inv = ROPE_BASE ** (-jnp.arange(ROPE_PAIRS, dtype=F32) / ROPE_PAIRS)
    ar = (pos // GRID_W).astype(F32)[:, None] * inv
    ac = (pos % GRID_W).astype(F32)[:, None] * inv
    ang = jnp.concatenate([ar, ar, ac, ac], axis=-1)
    sign = jnp.tile(jnp.concatenate([-jnp.ones((ROPE_PAIRS,), F32), jnp.ones((ROPE_PAIRS,), F32)]), 2)
    cos = jnp.concatenate([jnp.ones((ctx_len, HEAD_DIM), F32), jnp.cos(ang)], axis=0)
    sin = jnp.concatenate([jnp.zeros((ctx_len, HEAD_DIM), F32), jnp.sin(ang) * sign], axis=0)
    return jnp.tile(cos, (1, 2)), jnp.tile(sin, (1, 2))


def _in_proj(kind, p, b, n, d, ctx_len):
    if kind == 0:
        args = (p["conv_pw1"].astype(BF16), p["conv_pw1_b"].reshape(1, 2 * d))
        return dict(tail=_glu_tail, args=args, in_specs=[_full_spec((d, 2 * d)), _full_spec((1, 2 * d))],
                    out_specs=[_row_spec(TR, d)], out_shape=[jax.ShapeDtypeStruct((b, n, d), F32)])
    if kind == 1:
        kvd = N_KV_HEADS * HEAD_DIM
        nk = N_KV_HEADS * LANES
        assert d == 2 * nk
        scale = HEAD_DIM ** -0.5
        w_qkv = p["attn_w_qkv"]

        def dup(w):
            w = w.reshape(d, N_KV_HEADS, HEAD_DIM)
            return jnp.concatenate([w, w], axis=-1).reshape(d, nk)

        w_all = jnp.concatenate([w_qkv[:, :d] * scale, dup(w_qkv[:, d:d + kvd]), dup(w_qkv[:, d + kvd:])],
                                axis=1).astype(BF16)
        cos, sin = _rope_tables(n, ctx_len)
        table = pl.BlockSpec((TR, LANES), lambda bb, t, *_: (t, 0))
        return dict(tail=_qkv_tail, args=(w_all, cos, sin), in_specs=[_full_spec((d, d + 2 * nk)), table, table],
                    out_specs=[_row_spec(TR, d), _row_spec(TR, nk), _row_spec(TR, nk)],
                    out_shape=[jax.ShapeDtypeStruct((b, n, d), BF16), jax.ShapeDtypeStruct((b, n, nk), BF16),
                               jax.ShapeDtypeStruct((b, n, nk), BF16)])
    return dict(tail=_lru_in_tail, args=(p["lru_w_in"].astype(BF16),), in_specs=[_full_spec((d, 2 * d))],
                out_specs=[_row_spec(TR, d), _row_spec(TR, d)],
                out_shape=[jax.ShapeDtypeStruct((b, n, d), F32), jax.ShapeDtypeStruct((b, n, d), F32)])


def _head_body_specs(d, body_off):
    return [pl.BlockSpec((1, TR, d), lambda b, t, *_: (b, 0, 0)),
            pl.BlockSpec((1, TR, d), lambda b, t, *_: (b, jnp.maximum(t - body_off, 0), 0))]


def _head_body_tile(head_ref, body_ref):
    return jnp.where(pl.program_id(1) == 0, head_ref[0], body_ref[0])


def _in_proj_kernel(head_ref, body_ref, mod_ref, g_ref, *refs, tail):
    m = mod_ref[0, 0]
    tail(_norm_mod(_head_body_tile(head_ref, body_ref), g_ref[...], m[0:1], m[1:2]), *refs)


def _in_proj_standalone(ip, ctx, x, mods, g):
    b, s, d = x.shape
    n = ctx.shape[1] + s
    return pl.pallas_call(
        functools.partial(_in_proj_kernel, tail=ip["tail"]), grid=(b, n // TR),
        in_specs=_head_body_specs(d, 1) + [_seg_mod_spec(d), _full_spec((1, d))] + ip["in_specs"],
        out_specs=ip["out_specs"], out_shape=ip["out_shape"],
        compiler_params=_params(("parallel", "parallel")),
        name="in_proj",
    )(ctx, x, mods, g.reshape(1, d), *ip["args"])


def _conv_core_kernel(xh_ref, xb_ref, mod_ref, gp_ref, gc_ref, gn_ref, dw_ref, dwb_ref, lng_ref, lnb_ref, w_ref, b_ref,
                      gffn_ref, wr_ref, rb_ref, o_ref, h_ref, e0_ref, e1_ref, p0_ref, p1_ref, gt_ref, cnt_ref,
                      gbuf, cbuf, carry):
    t = pl.program_id(1)
    nt = pl.num_programs(1)
    first = _first_step()
    tr, d = gc_ref.shape[1], gc_ref.shape[2]
    seg_first = t <= 1
    seg_last = jnp.logical_or(t == 0, t == nt - 1)
    gbuf[0, 0:HALO, :] = jnp.where(seg_first, 0.0, gp_ref[0])
    gbuf[0, HALO:HALO + tr, :] = gc_ref[0]
    gbuf[0, HALO + tr:HALO + tr + HALO, :] = jnp.where(seg_last, 0.0, gn_ref[0])
    span = tr + 2 * HALO - SUBLANES
    for r in range(1, SUBLANES):
        gbuf[r, 0:span, :] = gbuf[0, r:r + span, :]

    rc = 32
    base = HALO - CONV_HALF

    def chunk(i, carry_):
        r0 = pl.multiple_of(i * rc, rc)
        for c in range(d // LANES):
            cs = slice(c * LANES, (c + 1) * LANES)
            acc = jnp.zeros((rc, LANES), F32)
            for k in range(CONV_WIDTH):
                off = base + k
                rows = pl.ds(r0 + (off // SUBLANES) * SUBLANES, rc)
                acc = acc + dw_ref[k:k + 1, cs] * gbuf[off % SUBLANES, rows, cs]
            cbuf[pl.ds(r0, rc), cs] = acc
        return carry_

    lax.fori_loop(0, tr // rc, chunk, 0)

    u = cbuf[...] + dwb_ref[...]
    mu = jnp.mean(u, axis=-1, keepdims=True)
    uc = u - mu
    var = jnp.mean(uc * uc, axis=-1, keepdims=True)
    v = uc * lax.rsqrt(var + EPS) * lng_ref[...] + lnb_ref[...]
    v = v * jax.nn.sigmoid(v)
    y = jnp.dot(v.astype(BF16), w_ref[...], preferred_element_type=F32) + b_ref[...]
    m = mod_ref[0, 0]
    x_new = _head_body_tile(xh_ref, xb_ref) + m[2:3] * y
    o_ref[0] = x_new
    _route_tile(x_new, m, gffn_ref, wr_ref, rb_ref, carry, first,
                h_ref, e0_ref, e1_ref, p0_ref, p1_ref, gt_ref, cnt_ref)


def _conv_core(head, body, body_off, mods, glu, p, route_args):
    b, n, d = glu.shape
    nt = n // TR
    hb = TR // HALO
    nh = n // HALO
    dwp = jnp.zeros((32, d), F32).at[:CONV_WIDTH].set(p["conv_dw"])
    r_in, r_out, r_shape, r_scratch = _route_io(b, n, d, TR, lambda t: t)
    return pl.pallas_call(
        _conv_core_kernel, grid=(b, nt),
        in_specs=_head_body_specs(d, body_off) + [
                  _seg_mod_spec(d),
                  pl.BlockSpec((1, HALO, d), lambda bb, t: (bb, jnp.maximum(t * hb - 1, 0), 0)),
                  _row_spec(TR, d),
                  pl.BlockSpec((1, HALO, d), lambda bb, t: (bb, jnp.minimum((t + 1) * hb, nh - 1), 0)),
                  _full_spec((32, d)), _full_spec((1, d)), _full_spec((1, d)), _full_spec((1, d)),
                  _full_spec((d, d)), _full_spec((1, d))] + r_in,
        out_specs=[_row_spec(TR, d)] + r_out,
        out_shape=[jax.ShapeDtypeStruct((b, n, d), F32)] + r_shape,
        scratch_shapes=[pltpu.VMEM((SUBLANES, TR + 2 * HALO, d), F32), pltpu.VMEM((TR, d), F32)] + r_scratch,
        compiler_params=_params(("arbitrary", "arbitrary")),
        name="conv_core",
    )(head, body, mods, glu, glu, glu, dwp, p["conv_dw_b"].reshape(1, d), p["conv_ln_g"].reshape(1, d),
      p["conv_ln_b"].reshape(1, d), p["conv_pw2"].astype(BF16), p["conv_pw2_b"].reshape(1, d), *route_args)


def _attn_core_kernel(sink_ref, x_ref, mod_ref, q_ref, kc_ref, vc_ref, kp_ref, km_ref, kn_ref,
                      vp_ref, vm_ref, vn_ref, wo_ref, gffn_ref, wr_ref, rb_ref,
                      o_ref, h_ref, e0_ref, e1_ref, p0_ref, p1_ref, gt_ref, cnt_ref, carry, *, ctx_len, n_rows):
    first = _first_step()
    tq = kp_ref.shape[1]
    nctx = kc_ref.shape[1]
    nkeys = nctx + 3 * tq
    lane = lax.broadcasted_iota(I32, (tq, LANES), 1)
    lo = lane < HEAD_DIM
    col = lax.broadcasted_iota(I32, (tq, nkeys), 1)
    row = lax.broadcasted_iota(I32, (tq, nkeys), 0)

    def attend(t, q_rows, k_parts, v_parts):
        rq = t * tq + row
        rk = (t - 1) * tq + (col - nctx)
        win_ok = (rk >= ctx_len) & (rk < n_rows) & (jnp.abs(rq - rk) <= WINDOW) & (t * tq >= ctx_len)
        bias = jnp.where((col < nctx) | win_ok, 0.0, NEG).astype(F32)
        out_cols = []
        for g in range(N_KV_HEADS):
            gs = slice(g * LANES, (g + 1) * LANES)
            kg = jnp.concatenate([kc_ref[0, :, gs]] + [part(gs) for part in k_parts], axis=0)
            vg = jnp.concatenate([vc_ref[0, :, gs]] + [part(gs) for part in v_parts], axis=0)
            qa = q_ref[0, q_rows, (2 * g) * LANES:(2 * g + 1) * LANES]
            qb = q_ref[0, q_rows, (2 * g + 1) * LANES:(2 * g + 2) * LANES]
            zero = jnp.zeros_like(qa)
            q4 = jnp.concatenate([jnp.where(lo, qa, zero), jnp.where(lo, zero, qa),
                                  jnp.where(lo, qb, zero), jnp.where(lo, zero, qb)], axis=0)
            s = lax.dot_general(q4, kg, (((1,), (1,)), ((), ())), preferred_element_type=F32)
            ps, dens = [], []
            for hh in range(GQA_GROUP):
                sk = sink_ref[g * GQA_GROUP + hh]
                sh = s[hh * tq:(hh + 1) * tq] + bias
                mx = jnp.maximum(jnp.max(sh, axis=-1, keepdims=True), sk)
                pr = jnp.exp(sh - mx)
                dens.append(jnp.sum(pr, axis=-1, keepdims=True) + jnp.exp(sk - mx))
                ps.append(pr.astype(BF16))
            pv = jnp.dot(jnp.concatenate(ps, axis=0), vg, preferred_element_type=F32)
            oh = [pv[hh * tq:(hh + 1) * tq] / dens[hh] for hh in range(GQA_GROUP)]
            out_cols.append(jnp.where(lo, oh[0], oh[1]))
            out_cols.append(jnp.where(lo, oh[2], oh[3]))
        return jnp.concatenate(out_cols, axis=1).astype(BF16)

    def rows_of(ref, r0):
        return lambda gs: ref[0, r0:r0 + tq, gs]

    t2 = 2 * pl.program_id(1)
    o = jnp.concatenate([
        attend(t2, slice(0, tq), [rows_of(kp_ref, 0), rows_of(km_ref, 0), rows_of(km_ref, tq)],
               [rows_of(vp_ref, 0), rows_of(vm_ref, 0), rows_of(vm_ref, tq)]),
        attend(t2 + 1, slice(tq, 2 * tq), [rows_of(km_ref, 0), rows_of(km_ref, tq), rows_of(kn_ref, 0)],
               [rows_of(vm_ref, 0), rows_of(vm_ref, tq), rows_of(vn_ref, 0)])], axis=0)
    y = jnp.dot(o, wo_ref[...], preferred_element_type=F32)
    m = mod_ref[0, 0]
    x_new = x_ref[0] + m[2:3] * y
    o_ref[0] = x_new
    _route_tile(x_new, m, gffn_ref, wr_ref, rb_ref, carry, first,
                h_ref, e0_ref, e1_ref, p0_ref, p1_ref, gt_ref, cnt_ref)


def _attn_core(x, mods, qkv, p, route_args, ctx_len):
    b, n, d = x.shape
    q, k, v = qkv
    nk = k.shape[-1]
    assert ctx_len == TR and TR == 2 * TQ
    ntq = n // TQ

    prev_spec = pl.BlockSpec((1, TQ, nk), lambda bb, t, s: (bb, jnp.maximum(2 * t - 1, 0), 0))
    next_spec = pl.BlockSpec((1, TQ, nk), lambda bb, t, s: (bb, jnp.minimum(2 * t + 2, ntq - 1), 0))
    ctx_spec = pl.BlockSpec((1, ctx_len, nk), lambda bb, t, s: (bb, 0, 0))
    r_in, r_out, r_shape, r_scratch = _route_io(b, n, d, TR, lambda t: t)
    grid_spec = pltpu.PrefetchScalarGridSpec(
        num_scalar_prefetch=1, grid=(b, n // TR),
        in_specs=[_row_spec(TR, d), _seg_mod_spec(d), _row_spec(TR, d),
                  ctx_spec, ctx_spec, prev_spec, _row_spec(TR, nk), next_spec, prev_spec, _row_spec(TR, nk), next_spec,
                  _full_spec((d, d))] + r_in,
        out_specs=[_row_spec(TR, d)] + r_out,
        scratch_shapes=r_scratch)
    return pl.pallas_call(
        functools.partial(_attn_core_kernel, ctx_len=ctx_len, n_rows=n),
        grid_spec=grid_spec,
        out_shape=[jax.ShapeDtypeStruct((b, n, d), F32)] + r_shape,
        compiler_params=_params(("arbitrary", "arbitrary")),
        name="attn_core",
    )(p["attn_sink"].astype(F32), x, mods, q, k, v, k, k, k, v, v, v, p["attn_w_o"].astype(BF16), *route_args)


def _lru_gates_scan(back, t, u_ref, cw_ref, cb_ref, wg_ref, ba_ref, bx_ref, lam_ref,
                    ubuf, halo, hcar, abuf, bbuf, hbuf):
    tr, d = u_ref.shape[1], u_ref.shape[2]
    nb = d // LRU_BLOCKS

    @pl.when(t <= 1)
    def _():
        halo[...] = jnp.zeros_like(halo)

    @pl.when(t == 0)
    def _():
        hcar[...] = jnp.zeros_like(hcar)

    u = u_ref[0]
    if not back:
        ubuf[0:SUBLANES, :] = halo[...]
        ubuf[SUBLANES:SUBLANES + tr, :] = u
        halo[...] = u[tr - SUBLANES:, :]
        taps = [ubuf[SUBLANES - (LRU_CONV_W - 1) + k:SUBLANES - (LRU_CONV_W - 1) + k + tr, :]
                for k in range(LRU_CONV_W)]
    else:
        ubuf[0:tr, :] = u
        ubuf[tr:tr + SUBLANES, :] = halo[...]
        halo[...] = u[:SUBLANES, :]
        taps = [ubuf[(LRU_CONV_W - 1) - k:(LRU_CONV_W - 1) - k + tr, :] for k in range(LRU_CONV_W)]
    cc = cb_ref[...] + taps[0] * cw_ref[0:1, :]
    for k in range(1, LRU_CONV_W):
        cc = cc + taps[k] * cw_ref[k:k + 1, :]

    ccb = cc.astype(BF16)
    lam = lam_ref[...]
    neg_c_softplus = -LRU_C * (jnp.maximum(-lam, 0.0) + jnp.log(1.0 + jnp.exp(-jnp.abs(lam))))
    for blk in range(LRU_BLOCKS):
        cs = slice(blk * nb, (blk + 1) * nb)
        z = jnp.dot(ccb[:, cs], wg_ref[blk], preferred_element_type=F32)
        r = jax.nn.sigmoid(z[:, :nb] + ba_ref[:, cs])
        gi = jax.nn.sigmoid(z[:, nb:] + bx_ref[:, cs])
        log_a = neg_c_softplus[:, cs] * r
        a = jnp.exp(log_a)
        abuf[:, cs] = a
        y = 1.0 - a * a
        bbuf[:, cs] = jnp.where(y > 0.0, y * lax.rsqrt(y), 0.0) * (gi * cc[:, cs])

    row = lax.broadcasted_iota(I32, (SUBLANES, d), 0)
    nchunk = tr // SUBLANES

    def chunk(i, h):
        ci = (nchunk - 1 - i) if back else i
        r0 = pl.multiple_of(ci * SUBLANES, SUBLANES)
        a = abuf[pl.ds(r0, SUBLANES), :]
        bb = bbuf[pl.ds(r0, SUBLANES), :]
        for k in (1, 2, 4):
            if back:
                sh, ok = SUBLANES - k, row < SUBLANES - k
            else:
                sh, ok = k, row >= k
            a_s = pltpu.roll(a, sh, 0)
            b_s = pltpu.roll(bb, sh, 0)
            bb = jnp.where(ok, a * b_s + bb, bb)
            a = jnp.where(ok, a * a_s, a)
        hh = a * h + bb
        hbuf[pl.ds(r0, SUBLANES), :] = hh
        return hh[0:1, :] if back else hh[SUBLANES - 1:SUBLANES, :]

    hcar[0:1, :] = lax.fori_loop(0, nchunk, chunk, hcar[0:1, :])


def _lru_fwd_kernel(u_ref, cw_ref, cb_ref, wg_ref, ba_ref, bx_ref, lam_ref, hf_ref,
                    ubuf, halo, hcar, abuf, bbuf, hbuf):
    _lru_gates_scan(False, pl.program_id(1), u_ref, cw_ref, cb_ref, wg_ref, ba_ref, bx_ref, lam_ref,
                    ubuf, halo, hcar, abuf, bbuf, hbuf)
    hf_ref[0] = hbuf[...]


def _lru_bwd_kernel(u_ref, cw_ref, cb_ref, wg_ref, ba_ref, bx_ref, lam_ref, hf_ref, gg_ref, x_ref, mod_ref,
                    wo_ref, gffn_ref, wr_ref, rb_ref, o_ref, h_ref, e0_ref, e1_ref, p0_ref, p1_ref, gt_ref, cnt_ref,
                    ubuf, halo, hcar, abuf, bbuf, hbuf, carry):
    first = _first_step()
    _lru_gates_scan(True, pl.program_id(1), u_ref, cw_ref, cb_ref, wg_ref, ba_ref, bx_ref, lam_ref,
                    ubuf, halo, hcar, abuf, bbuf, hbuf)
    y = (hf_ref[0] + hbuf[...]) * gg_ref[0]
    out = jnp.dot(y.astype(BF16), wo_ref[...], preferred_element_type=F32)
    m = mod_ref[0, 0]
    x_new = x_ref[0] + m[2:3] * out
    o_ref[0] = x_new
    _route_tile(x_new, m, gffn_ref, wr_ref, rb_ref, carry, first,
                h_ref, e0_ref, e1_ref, p0_ref, p1_ref, gt_ref, cnt_ref)


def _lru_core(x, mods, ugg, p, route_args):
    b, n, d = x.shape
    u, gg = ugg
    nt = n // TR
    nb = d // LRU_BLOCKS
    wg = jnp.concatenate([p["lru_wa"], p["lru_wx"]], axis=-1).astype(BF16)
    scratch = [pltpu.VMEM((TR + SUBLANES, d), F32), pltpu.VMEM((SUBLANES, d), F32),
               pltpu.VMEM((SUBLANES, d), F32), pltpu.VMEM((TR, d), F32), pltpu.VMEM((TR, d), F32),
               pltpu.VMEM((TR, d), F32)]

    def gate_specs(order):
        return [pl.BlockSpec((1, TR, d), lambda bb, t: (bb, order(t), 0)),
                _full_spec((LRU_CONV_W, d)), _full_spec((1, d)), _full_spec((LRU_BLOCKS, nb, 2 * nb)),
                _full_spec((1, d)), _full_spec((1, d)), _full_spec((1, d))]

    def gate_args(dd):
        return (u, p["lru_conv_w"][dd], p["lru_conv_b"][dd].reshape(1, d), wg[dd], p["lru_ba"][dd].reshape(1, d),
                p["lru_bx"][dd].reshape(1, d), p["lru_lam"][dd].reshape(1, d))

    hf = pl.pallas_call(
        _lru_fwd_kernel, grid=(b, nt),
        in_specs=gate_specs(lambda t: t),
        out_specs=_row_spec(TR, d),
        out_shape=jax.ShapeDtypeStruct((b, n, d), F32),
        scratch_shapes=scratch,
        compiler_params=_params(("arbitrary", "arbitrary")),
        name="lru_fwd",
    )(*gate_args(0))

    def rev(t):
        return jnp.where(t == 0, 0, nt - t)

    def rev_spec():
        return pl.BlockSpec((1, TR, d), lambda bb, t: (bb, rev(t), 0))

    r_in, r_out, r_shape, r_scratch = _route_io(b, n, d, TR, rev)
    return pl.pallas_call(
        _lru_bwd_kernel, grid=(b, nt),
        in_specs=gate_specs(rev) + [rev_spec(), rev_spec(), rev_spec(), _seg_mod_spec(d), _full_spec((d, d))] + r_in,
        out_specs=[rev_spec()] + r_out,
        out_shape=[jax.ShapeDtypeStruct((b, n, d), F32)] + r_shape,
        scratch_shapes=scratch + r_scratch,
        compiler_params=_params(("arbitrary", "arbitrary")),
        name="lru_bwd",
    )(*gate_args(1), hf, gg, x, mods, p["lru_w_out"].astype(BF16), *route_args)


def _dispatch_kernel(slot_ref, zrow_ref, nu_ref, h_ref, xs_ref, ring, zbuf, sem, zsem, *, n_tok, n_blocks):
    s = pl.program_id(0)
    ns = pl.num_programs(0)
    tr = h_ref.shape[0] // SUBLANES
    blk = zbuf.shape[0] // SUBLANES
    par = s % 2

    @pl.when(s == 0)
    def _():
        zbuf[...] = jnp.zeros_like(zbuf)
        for e in range(N_EXPERTS):
            pltpu.make_async_copy(zbuf, _tile_rows(xs_ref, zrow_ref[e], blk), zsem).start()
        for e in range(N_EXPERTS):
            pltpu.make_async_copy(zbuf, _tile_rows(xs_ref, 0, blk), zsem).wait()

        def zero_blk(i, carry):
            pltpu.make_async_copy(zbuf, _tile_rows(xs_ref, i * blk, blk), zsem).start()
            return carry

        def zero_blk_wait(i, carry):
            pltpu.make_async_copy(zbuf, _tile_rows(xs_ref, 0, blk), zsem).wait()
            return carry

        lax.fori_loop(nu_ref[0], n_blocks, zero_blk, 0)
        lax.fori_loop(nu_ref[0], n_blocks, zero_blk_wait, 0)

    def wait_ring(p):
        for _ in range(TOP_K):
            pltpu.make_async_copy(ring.at[p], _tile_rows(xs_ref, 0, tr), sem.at[p]).wait()

    @pl.when(s >= 2)
    def _():
        wait_ring(par)

    ring[par] = h_ref[...]
    base = s * tr

    def issue(i, carry):
        for k in range(TOP_K):
            pltpu.make_async_copy(_tile_rows(ring.at[par], i, 1),
                                  _tile_rows(xs_ref, slot_ref[k * n_tok + base + i], 1),
                                  sem.at[par]).start(priority=k)
        return carry

    lax.fori_loop(0, tr, issue, 0, unroll=8)

    @pl.when(s == ns - 1)
    def _():
        wait_ring(par)

        @pl.when(ns >= 2)
        def _():
            wait_ring(1 - par)


def _expert_kernel(be_ref, nu_ref, xs_ref, w1_ref, w3_ref, w2_ref, ys_ref, xring, sem, w1b, w3b, w2b):
    i = pl.program_id(0)
    nb = pl.num_programs(0)
    blk = xring.shape[1] // SUBLANES
    used = i < nu_ref[0]
    new_expert = jnp.logical_or(i == 0, be_ref[i] != be_ref[jnp.maximum(i - 1, 0)])
    ahead = EXPERT_RING - 1

    def fetch(block, ring):
        return pltpu.make_async_copy(_tile_rows(xs_ref, block * blk, blk), xring.at[ring], sem.at[ring])

    @pl.when(i == 0)
    def _():
        for first in range(ahead):
            fetch(jnp.minimum(first, nb - 1), first).start()

    @pl.when(jnp.logical_and(used, new_expert))
    def _():
        w1b[...] = w1_ref[0, 0].astype(BF16)
        w3b[...] = w3_ref[0, 0].astype(BF16)
        w2b[...] = w2_ref[0, 0].astype(BF16)

    @pl.when(used)
    def _():
        ring = i % EXPERT_RING
        fetch(0, ring).wait()
        fetch(jnp.minimum(i + ahead, nb - 1), (i + ahead) % EXPERT_RING).start()
        xb = _load_token_tiles(xring.at[ring], blk).astype(BF16)
        a = jnp.dot(xb, w1b[...], preferred_element_type=F32)
        b = jnp.dot(xb, w3b[...], preferred_element_type=F32)
        hid = (a * jax.nn.sigmoid(a)) * b
        _store_token_tiles(ys_ref, jnp.dot(hid.astype(BF16), w2b[...], preferred_element_type=F32))

        @pl.when(i == nu_ref[0] - 1)
        def _():
            for later in range(1, EXPERT_RING):
                fetch(0, (i + later) % EXPERT_RING).wait()

    @pl.when(jnp.logical_not(used))
    def _():
        ys_ref[...] = jnp.zeros_like(ys_ref)


def _combine_kernel(slot_ref, x_ref, mod_ref, gt_ref, ys_ref, *refs, n_tok, rows_per_batch, tile_off, tail, n_tail_in):
    if tail is None:
        nf_ref, o_ref, ybuf, sem = refs
    else:
        modn_ref, gn_ref = refs[0], refs[1]
        tail_in = refs[2:2 + n_tail_in]
        o_ref = refs[2 + n_tail_in]
        tail_out = refs[3 + n_tail_in:-2]
        ybuf, sem = refs[-2], refs[-1]
    bb, t = pl.program_id(0), pl.program_id(1)
    nt = pl.num_programs(1)
    tr = x_ref.shape[1]
    step = bb * nt + t
    nsteps = pl.num_programs(0) * nt

    def tok_base(s):
        return (s // nt) * rows_per_batch + (s % nt + tile_off) * tr

    def copy_row(base, i, ring, k):
        pltpu.make_async_copy(_tile_rows(ys_ref, slot_ref[k * n_tok + base + i], 1),
                              _tile_rows(ybuf.at[ring, k], i, 1), sem.at[ring]).start(priority=k)

    def wait_ring(ring):
        for k in range(TOP_K):
            pltpu.make_async_copy(_tile_rows(ys_ref, 0, tr), ybuf.at[ring, k], sem.at[ring]).wait()

    @pl.when(step == 0)
    def _():
        for first in range(COMBINE_RING - 1):
            base = tok_base(jnp.minimum(first, nsteps - 1))

            def one(i, carry, base=base, first=first):
                for k in range(TOP_K):
                    copy_row(base, i, first, k)
                return carry

            lax.fori_loop(0, tr, one, 0, unroll=8)

    ring = step % COMBINE_RING
    wait_ring(ring)
    gt = gt_ref[0]
    g0 = jnp.broadcast_to(gt[:, 0:1], (tr, LANES))
    g1 = jnp.broadcast_to(gt[:, 1:2], (tr, LANES))
    m5 = mod_ref[0, 0][5:6]
    for s in range(SUBLANES):
        cs = slice(s * LANES, (s + 1) * LANES)
        y_s = (g0 * ybuf[ring, 0, pl.ds(s, tr, stride=SUBLANES), :]
               + g1 * ybuf[ring, 1, pl.ds(s, tr, stride=SUBLANES), :])
        o_ref[0, :, cs] = x_ref[0, :, cs] + m5[:, cs] * y_s

    ahead = COMBINE_RING - 1
    fetch_base = tok_base(jnp.minimum(step + ahead, nsteps - 1))
    fetch_ring = (step + ahead) % COMBINE_RING
    for i in range(tr):
        for k in range(TOP_K):
            copy_row(fetch_base, i, fetch_ring, k)

    out = o_ref[0]
    if tail is None:
        o_ref[0] = out * lax.rsqrt(jnp.mean(out * out, axis=-1, keepdims=True) + EPS) * nf_ref[...]
    else:
        mn = modn_ref[0, 0]
        tail(_norm_mod(out, gn_ref[...], mn[0:1], mn[1:2]), *tail_in, *tail_out)

    @pl.when(step == nsteps - 1)
    def _():
        for later in range(1, COMBINE_RING):
            wait_ring((step + later) % COMBINE_RING)


def _moe(x, mods, routed, layer, w1, w3, w2, ctx_len, norm_f=None, next_ip=None, next_mods=None, next_g=None):
    b, n, d = x.shape
    nt = n // TR
    n_tok = b * n
    de = w1.shape[-1]
    h, e0, e1, p0, p1, gtab, counts = routed

    counts = counts[:, 0]
    padded = (counts + MOE_BLK - 1) // MOE_BLK * MOE_BLK
    pend = jnp.cumsum(padded)
    pstart = (pend - padded).astype(I32)
    n_slots = n_tok * TOP_K + N_EXPERTS * MOE_BLK
    n_blocks = n_slots // MOE_BLK
    n_used = (pend[-1] // MOE_BLK).astype(I32).reshape(1)
    blk_start = jnp.arange(n_blocks, dtype=I32) * MOE_BLK
    blk_e = jnp.minimum(jnp.sum((pend[None, :] <= blk_start[:, None]).astype(I32), axis=1), N_EXPERTS - 1)

    def slot_of(e, p):
        e, p = e.reshape(n_tok), p.reshape(n_tok)
        start = jnp.zeros_like(e)
        for j in range(N_EXPERTS):
            start = jnp.where(e == j, pstart[j], start)
        return start + p

    slots = jnp.concatenate([slot_of(e0, p0), slot_of(e1, p1)])
    zrow = (pstart + counts).astype(I32)

    xs = pl.pallas_call(
        functools.partial(_dispatch_kernel, n_tok=n_tok, n_blocks=n_blocks),
        grid_spec=pltpu.PrefetchScalarGridSpec(
            num_scalar_prefetch=3, grid=(n_tok // TR,),
            in_specs=[pl.BlockSpec((TR * SUBLANES, LANES), lambda s, *_: (s, 0))],
            out_specs=pl.BlockSpec(memory_space=pl.ANY),
            scratch_shapes=[pltpu.VMEM((2, TR * SUBLANES, LANES), F32), pltpu.VMEM((MOE_BLK * SUBLANES, LANES), F32),
                            pltpu.SemaphoreType.DMA((2,)), pltpu.SemaphoreType.DMA]),
        out_shape=jax.ShapeDtypeStruct((n_slots * SUBLANES, LANES), F32),
        compiler_params=_params(("arbitrary",)),
        name="moe_dispatch",
    )(slots, zrow, n_used, h)

    def wspec(shape):
        return pl.BlockSpec((1, 1) + shape, lambda i, be, nu: (layer, be[i], 0, 0))

    ys = pl.pallas_call(
        _expert_kernel,
        grid_spec=pltpu.PrefetchScalarGridSpec(
            num_scalar_prefetch=2, grid=(n_blocks,),
            in_specs=[pl.BlockSpec(memory_space=pl.ANY), wspec((d, de)), wspec((d, de)), wspec((de, d))],
            out_specs=pl.BlockSpec((MOE_BLK * SUBLANES, LANES), lambda i, be, nu: (i, 0)),
            scratch_shapes=[pltpu.VMEM((EXPERT_RING, MOE_BLK * SUBLANES, LANES), F32),
                            pltpu.SemaphoreType.DMA((EXPERT_RING,)),
                            pltpu.VMEM((d, de), BF16), pltpu.VMEM((d, de), BF16), pltpu.VMEM((de, d), BF16)]),
        out_shape=jax.ShapeDtypeStruct((n_slots * SUBLANES, LANES), F32),
        compiler_params=_params(("arbitrary",)),
        name="moe_experts",
    )(blk_e, n_used, xs, w1, w3, w2)

    final = next_ip is None
    tile_off = ctx_len // TR if final else 0
    nt_out = nt - tile_off
    common_specs = [_row_spec(TR, d, tile_off), _seg_mod_spec(d, tile_off=tile_off), _row_spec(TR, LANES, tile_off),
                    pl.BlockSpec(memory_space=pl.ANY)]
    scratch = [pltpu.VMEM((COMBINE_RING, TOP_K, TR * SUBLANES, LANES), F32),
               pltpu.SemaphoreType.DMA((COMBINE_RING,))]
    stream_spec = _row_spec(TR, d)
    stream_shape = jax.ShapeDtypeStruct((b, nt_out * TR, d), F32)
    if final:
        return pl.pallas_call(
            functools.partial(_combine_kernel, n_tok=n_tok, rows_per_batch=n, tile_off=tile_off, tail=None,
                              n_tail_in=0),
            grid_spec=pltpu.PrefetchScalarGridSpec(
                num_scalar_prefetch=1, grid=(b, nt_out),
                in_specs=common_specs + [_full_spec((1, d))],
                out_specs=stream_spec, scratch_shapes=scratch),
            out_shape=stream_shape,
            compiler_params=_params(("arbitrary", "arbitrary")),
            name="moe_combine_final",
        )(slots, x, mods, gtab, ys, norm_f.reshape(1, d))
    outs = pl.pallas_call(
        functools.partial(_combine_kernel, n_tok=n_tok, rows_per_batch=n, tile_off=tile_off, tail=next_ip["tail"],
                          n_tail_in=len(next_ip["args"])),
        grid_spec=pltpu.PrefetchScalarGridSpec(
            num_scalar_prefetch=1, grid=(b, nt_out),
            in_specs=common_specs + [_seg_mod_spec(d), _full_spec((1, d))] + next_ip["in_specs"],
            out_specs=[stream_spec] + next_ip["out_specs"], scratch_shapes=scratch),
        out_shape=[stream_shape] + next_ip["out_shape"],
        compiler_params=_params(("arbitrary", "arbitrary")),
        name="moe_combine_in_proj",
    )(slots, x, mods, gtab, ys, next_mods, next_g.reshape(1, d), *next_ip["args"])
    return outs[0], outs[1:]


def kernel(x, c, ctx, c_ctx, w_mod, b_mod, norm_mix, norm_ffn, norm_f, conv_pw1, conv_pw1_b, conv_dw, conv_dw_b, conv_ln_g, conv_ln_b, conv_pw2, conv_pw2_b, attn_w_qkv, attn_w_o, attn_sink, lru_w_in, lru_conv_w, lru_conv_b, lru_wa, lru_ba, lru_wx, lru_bx, lru_lam, lru_w_out, moe_w_router, moe_router_bias, moe_w1, moe_w3, moe_w2):
    b, s, d = x.shape
    ctx_len = ctx.shape[1]
    depth = w_mod.shape[0]
    n = ctx_len + s
    assert ctx_len == TR and s % TR == 0 and d == SUBLANES * LANES
    assert moe_w_router.shape[1] == N_EXPERTS

    mods = _modulation(c, c_ctx, w_mod, b_mod)
    xs = None

    wr = jnp.zeros((EXPERTS_PER_GROUP, SUBLANES, d), F32).at[:, :N_GROUPS].set(
        moe_w_router.T.reshape(N_GROUPS, EXPERTS_PER_GROUP, d).swapaxes(0, 1))
    wr = wr.reshape(ROUTER_ROWS, d).astype(BF16)
    rb = jnp.zeros((EXPERTS_PER_GROUP, SUBLANES), F32).at[:, :N_GROUPS].set(
        moe_router_bias.astype(F32).reshape(N_GROUPS, EXPERTS_PER_GROUP).T).reshape(-1, 1)

    def layer_params(i):
        slot = i // N_MIXERS
        return [dict(conv_pw1=conv_pw1, conv_pw1_b=conv_pw1_b, conv_dw=conv_dw, conv_dw_b=conv_dw_b,
                     conv_ln_g=conv_ln_g, conv_ln_b=conv_ln_b, conv_pw2=conv_pw2, conv_pw2_b=conv_pw2_b),
                dict(attn_w_qkv=attn_w_qkv, attn_w_o=attn_w_o, attn_sink=attn_sink),
                dict(lru_w_in=lru_w_in, lru_conv_w=lru_conv_w, lru_conv_b=lru_conv_b, lru_wa=lru_wa, lru_ba=lru_ba,
                     lru_wx=lru_wx, lru_bx=lru_bx, lru_lam=lru_lam, lru_w_out=lru_w_out)][i % N_MIXERS], slot

    def params_of(i):
        group, slot = layer_params(i)
        return {k: v[slot] for k, v in group.items()}

    p = params_of(0)
    ip = _in_proj(0, p, b, n, d, ctx_len)
    a_out = _in_proj_standalone(ip, ctx, x, mods[0], norm_mix[0])
    if not isinstance(a_out, (list, tuple)):
        a_out = [a_out]
    for i in range(depth):
        kind = i % N_MIXERS
        route_args = (norm_ffn[i].reshape(1, d), wr, rb)
        if kind == 0:
            head, body, body_off = (ctx, x, 1) if i == 0 else (xs, xs, 0)
            res = _conv_core(head, body, body_off, mods[i], a_out[0], p, route_args)
        elif kind == 1:
            res = _attn_core(xs, mods[i], a_out, p, route_args, ctx_len)
        else:
            res = _lru_core(xs, mods[i], a_out, p, route_args)
        x_new, routed = res[0], res[1:]
        if i == depth - 1:
            return _moe(x_new, mods[i], routed, i, moe_w1, moe_w3, moe_w2, ctx_len, norm_f=norm_f)
        p = params_of(i + 1)
        ip = _in_proj((i + 1) % N_MIXERS, p, b, n, d, ctx_len)
        xs, a_out = _moe(x_new, mods[i], routed, i, moe_w1, moe_w3, moe_w2, ctx_len,
                         next_ip=ip, next_mods=mods[i + 1], next_g=norm_mix[i + 1])
```
